```python
import math
import jax, jax.numpy as jnp
from jax import lax
import numpy as np

D_MODEL = 2048
BATCH = 4
SEQ = 2048
DEPTH = 2

HEAD_DIM = 128
ATTN_Q_HEADS = 16
ATTN_KV_HEADS = 4
ATTN_GROUP = ATTN_Q_HEADS // ATTN_KV_HEADS
IDX_HEADS = 16
IDX_DIM = 64
TOPK_MAX = 256
Q_BLOCK = 128
RET_HEADS = 8
RET_QK_DIM = 128
RET_V_DIM = 256
RET_CHUNK = 128
ROPE_BASE = 10000.0
D_FF = 4 * D_MODEL
EPS = 1e-6

ATTN_W = ATTN_Q_HEADS * HEAD_DIM
KV_W = ATTN_KV_HEADS * HEAD_DIM
IDX_Q_W = IDX_HEADS * IDX_DIM
RET_QK_W = RET_HEADS * RET_QK_DIM
RET_V_W = RET_HEADS * RET_V_DIM
IN_SPLITS = (ATTN_W, KV_W, KV_W, IDX_Q_W, IDX_DIM, IDX_HEADS,
             RET_QK_W, RET_QK_W, RET_V_W, RET_V_W, D_MODEL, D_MODEL)
D_IN = ATTN_W + 2 * KV_W + IDX_Q_W + IDX_DIM + IDX_HEADS + 2 * RET_QK_W + 2 * RET_V_W + 2 * D_MODEL

kernel_name = "hybrid_dsa_retention_gated_block"


def split_offsets():
    return np.cumsum(np.array(IN_SPLITS))[:-1].tolist()


def rms_norm(x, g):
    xf = x.astype(jnp.float32)
    r = lax.rsqrt(jnp.mean(xf * xf, axis=-1, keepdims=True) + EPS)
    return (xf * r).astype(x.dtype) * g


def rotary_tables(S, dtype):
    pos = jnp.arange(S, dtype=jnp.float32)
    inv_freq = ROPE_BASE ** (-jnp.arange(0, RET_QK_DIM, 2, dtype=jnp.float32) / RET_QK_DIM)
    ang = pos[:, None] * inv_freq[None, :]
    return jnp.cos(ang)[:, None, :].astype(dtype), jnp.sin(ang)[:, None, :].astype(dtype)


def rotate(x, cos, sin):
    x1, x2 = jnp.split(x, 2, axis=-1)
    return jnp.concatenate([x1 * cos - x2 * sin, x2 * cos + x1 * sin], axis=-1)


def dsa_attention(q, k, v, iq, ik, iw, topk):
    B, S = q.shape[0], q.shape[1]
    n_blocks = S // Q_BLOCK
    scale = HEAD_DIM ** -0.5
    kpos = jnp.arange(S)
    gather = jax.vmap(lambda t, i: t[i])

    def block(i):
        start = i * Q_BLOCK
        qb = lax.dynamic_slice_in_dim(q, start, Q_BLOCK, axis=1)
        iqb = lax.dynamic_slice_in_dim(iq, start, Q_BLOCK, axis=1)
        iwb = lax.dynamic_slice_in_dim(iw, start, Q_BLOCK, axis=1)
        qpos = start + jnp.arange(Q_BLOCK)
        logits = jnp.einsum('bqhd,bsd->bqhs', iqb, ik)
        score = jnp.einsum('bqh,bqhs->bqs', iwb, jax.nn.relu(logits)).astype(jnp.float32)
        causal = kpos[None, :] <= qpos[:, None]
        score = jnp.where(causal[None], score, -jnp.inf)
        _, idx = lax.top_k(score, topk)
        valid = idx <= qpos[None, :, None]
        kg = gather(k, idx)
        vg = gather(v, idx)
        qg = qb.reshape(B, Q_BLOCK, ATTN_KV_HEADS, ATTN_GROUP, HEAD_DIM)
        s = jnp.einsum('bqgrd,bqkgd->bqgrk', qg, kg).astype(jnp.float32) * scale
        s = jnp.where(valid[:, :, None, None, :], s, -jnp.inf)
        p = jax.nn.softmax(s, axis=-1).astype(v.dtype)
        o = jnp.einsum('bqgrk,bqkgd->bqgrd', p, vg)
        return o.reshape(B, Q_BLOCK, ATTN_W)

    out = lax.map(block, jnp.arange(n_blocks))
    return jnp.moveaxis(out, 0, 1).reshape(B, S, ATTN_W)


def retention_chunkwise(q, k, v, log_gamma):
    B, S, H, Dk = q.shape
    Dv = v.shape[-1]
    C = RET_CHUNK
    n = S // C
    dt = v.dtype

    def chunks(t):
        return t.reshape(B, n, C, H, t.shape[-1]).transpose(0, 1, 3, 2, 4)

    qc, kc, vc = chunks(q), chunks(k), chunks(v)
    pos = jnp.arange(C, dtype=jnp.float32)
    lg = log_gamma.astype(jnp.float32)
    diff = pos[:, None] - pos[None, :]
    decay = jnp.where(diff[None] >= 0, jnp.exp(lg[:, None, None] * jnp.maximum(diff, 0.0)[None]), 0.0)
    xi = jnp.exp(lg[:, None] * (pos[None, :] + 1.0))
    zeta = jnp.exp(lg[:, None] * (C - 1.0 - pos[None, :]))
    chunk_decay = jnp.exp(lg * C).astype(dt)

    inner = jnp.einsum('bnhid,bnhjd->bnhij', qc, kc) * decay.astype(dt)
    inner = jnp.einsum('bnhij,bnhje->bnhie', inner, vc)
    upd = jnp.einsum('bnhjd,bnhje->bnhde', kc * zeta[..., None].astype(dt), vc)

    def step(R, u):
        return u + chunk_decay[:, None, None] * R, R

    R0 = jnp.zeros((B, H, Dk, Dv), dt)
    _, R_prev = lax.scan(step, R0, jnp.moveaxis(upd, 1, 0))
    R_prev = jnp.moveaxis(R_prev, 0, 1)
    cross = jnp.einsum('bnhid,bnhde->bnhie', qc, R_prev) * xi[..., None].astype(dt)
    out = inner + cross
    return out.transpose(0, 1, 3, 2, 4).reshape(B, S, H, Dv)


def group_norm_heads(y, g, b):
    B, S = y.shape[0], y.shape[1]
    yf = y.astype(jnp.float32)
    mu = jnp.mean(yf, axis=-1, keepdims=True)
    var = jnp.mean(jnp.square(yf - mu), axis=-1, keepdims=True)
    yn = ((yf - mu) * lax.rsqrt(var + EPS)).astype(y.dtype)
    return yn.reshape(B, S, -1) * g + b


def setup_inputs(seed: int = 0) -> dict:
    key = jax.random.key(seed)
    ks = jax.random.split(key, 14)

    def w(k, shape, fan_in):
        return jax.random.normal(k, shape, jnp.float32) * fan_in ** -0.5

    def gain(k, shape):
        return 1.0 + 0.02 * jax.random.normal(k, shape, jnp.float32)

    return {
        "x": jax.random.normal(ks[0], (BATCH, SEQ, D_MODEL), jnp.float32),
        "ln1_g": gain(ks[1], (DEPTH, D_MODEL)),
        "w_in": w(ks[2], (DEPTH, D_MODEL, D_IN), D_MODEL),
        "q_norm_g": gain(ks[3], (DEPTH, HEAD_DIM)),
        "k_norm_g": gain(ks[4], (DEPTH, HEAD_DIM)),
        "ret_gn_g": gain(ks[5], (DEPTH, RET_V_W)),
        "ret_gn_b": 0.02 * jax.random.normal(ks[6], (DEPTH, RET_V_W), jnp.float32),
        "w_up_attn": w(ks[7], (DEPTH, ATTN_W, D_MODEL), ATTN_W),
        "w_up_ret": w(ks[8], (DEPTH, RET_V_W, D_MODEL), RET_V_W),
        "w_out": w(ks[9], (DEPTH, D_MODEL, D_MODEL), D_MODEL),
        "ln2_g": gain(ks[10], (DEPTH, D_MODEL)),
        "w_ff1": w(ks[11], (DEPTH, D_MODEL, D_FF), D_MODEL),
        "w_ff2": w(ks[12], (DEPTH, D_FF, D_MODEL), D_FF),
    }


def reference(x, ln1_g, w_in, q_norm_g, k_norm_g, ret_gn_g, ret_gn_b,
              w_up_attn, w_up_ret, w_out, ln2_g, w_ff1, w_ff2):
    B, S, _ = x.shape
    topk = min(TOPK_MAX, S // 4)
    cos, sin = rotary_tables(S, x.dtype)
    log_gamma = jnp.log(1.0 - jnp.exp2(-5.0 - jnp.arange(RET_HEADS, dtype=jnp.float32)))
    offsets = split_offsets()

    for l in range(DEPTH):
        h = rms_norm(x, ln1_g[l])
        proj = h @ w_in[l]
        aq, ak, av, iq, ik, iw, rq, rk, rv, rg, ga, gb = jnp.split(proj, offsets, axis=-1)

        aq = rms_norm(aq.reshape(B, S, ATTN_Q_HEADS, HEAD_DIM), q_norm_g[l])
        ak = rms_norm(ak.reshape(B, S, ATTN_KV_HEADS, HEAD_DIM), k_norm_g[l])
        av = av.reshape(B, S, ATTN_KV_HEADS, HEAD_DIM)
        iq = iq.reshape(B, S, IDX_HEADS, IDX_DIM)
        iw = iw * IDX_HEADS ** -0.5
        o_attn = dsa_attention(aq, ak, av, iq, ik, iw, topk)

        rq = rotate(rq.reshape(B, S, RET_HEADS, RET_QK_DIM), cos, sin)
        rk = rotate(rk.reshape(B, S, RET_HEADS, RET_QK_DIM), cos, sin) * RET_QK_DIM ** -0.5
        rv = rv.reshape(B, S, RET_HEADS, RET_V_DIM)
        y_ret = retention_chunkwise(rq, rk, rv, log_gamma)
        o_ret = jax.nn.silu(rg) * group_norm_heads(y_ret, ret_gn_g[l], ret_gn_b[l])

        merged = jax.nn.sigmoid(ga) * (o_attn @ w_up_attn[l]) + jax.nn.sigmoid(gb) * (o_ret @ w_up_ret[l])
        x = x + merged @ w_out[l]

        h2 = rms_norm(x, ln2_g[l])
        x = x + jnp.square(jax.nn.relu(h2 @ w_ff1[l])) @ w_ff2[l]
    return x
```

```python
import functools
import math

import jax
import jax.numpy as jnp
import numpy as np
from jax import lax
from jax.experimental import pallas as pl
from jax.experimental.pallas import tpu as pltpu

D_MODEL = 2048
HEAD_DIM = 128
ATTN_Q_HEADS = 16
ATTN_KV_HEADS = 4
ATTN_GROUP = ATTN_Q_HEADS // ATTN_KV_HEADS
IDX_HEADS = 16
IDX_DIM = 64
TOPK_MAX = 256
RET_HEADS = 8
RET_QK_DIM = 128
RET_V_DIM = 256
RET_CHUNK = 128
ROPE_BASE = 10000.0
D_FF = 4 * D_MODEL
EPS = 1e-6

ATTN_W = ATTN_Q_HEADS * HEAD_DIM
KV_W = ATTN_KV_HEADS * HEAD_DIM
IDX_Q_W = IDX_HEADS * IDX_DIM
RET_QK_W = RET_HEADS * RET_QK_DIM
RET_V_W = RET_HEADS * RET_V_DIM

LANES = 128
IDX_PACK_W = IDX_Q_W + LANES
PLAIN_W = KV_W + 2 * RET_V_W + 2 * D_MODEL
Q_SUPER = 512
NEG_BIG = -1e30
INT_MIN = -(2 ** 31)
INT_MAX = 2 ** 31 - 1
VMEM_LIMIT = 56 * 1024 * 1024

F32 = jnp.float32
BF16 = jnp.bfloat16


def _params(n_axes):
    return pltpu.CompilerParams(dimension_semantics=("arbitrary",) * n_axes,
                                vmem_limit_bytes=VMEM_LIMIT)


def _rmsnorm_kernel(x_ref, g_ref, o_ref):
    x = x_ref[...]
    r = lax.rsqrt(jnp.mean(x * x, axis=-1, keepdims=True) + EPS)
    o_ref[...] = ((x * r) * g_ref[...]).astype(o_ref.dtype)


def _rmsnorm(x, g, tm=512):
    m, d = x.shape
    return pl.pallas_call(
        _rmsnorm_kernel,
        grid=(m // tm,),
        in_specs=[pl.BlockSpec((tm, d), lambda i: (i, 0)),
                  pl.BlockSpec((1, d), lambda i: (0, 0))],
        out_specs=pl.BlockSpec((tm, d), lambda i: (i, 0)),
        out_shape=jax.ShapeDtypeStruct((m, d), BF16),
        compiler_params=_params(1),
        name="rmsnorm",
    )(x, g.reshape(1, d))


def _mm_kernel(*refs, nk, n_extra, epilogue):
    a_ref, w_ref = refs[0], refs[1]
    extra = refs[2:2 + n_extra]
    o_ref = refs[2 + n_extra]
    if nk == 1:
        acc = jnp.dot(a_ref[...], w_ref[...], preferred_element_type=F32)
        o_ref[...] = epilogue(acc, *extra).astype(o_ref.dtype)
        return
    acc_ref = refs[3 + n_extra]
    k = pl.program_id(2)

    @pl.when(k == 0)
    def _():
        acc_ref[...] = jnp.zeros_like(acc_ref)

    acc_ref[...] += jnp.dot(a_ref[...], w_ref[...], preferred_element_type=F32)

    @pl.when(k == nk - 1)
    def _():
        o_ref[...] = epilogue(acc_ref[...], *extra).astype(o_ref.dtype)


def _matmul(a, w, *, out_dtype, epilogue, extras=(), tm=1024, tn=1024, tk=None, name="matmul"):
    m, kdim = a.shape
    n = w.shape[1]
    tk = kdim if tk is None else tk
    nk = kdim // tk
    tn = min(tn, n)
    assert m % tm == 0 and n % tn == 0 and kdim % tk == 0
    in_specs = [pl.BlockSpec((tm, tk), lambda i, j, k: (i, k)),
                pl.BlockSpec((tk, tn), lambda i, j, k: (k, j))]
    for _, bshape, imap in extras:
        in_specs.append(pl.BlockSpec(bshape, functools.partial(lambda i, j, k, f: f(i, j), f=imap)))
    scratch = [pltpu.VMEM((tm, tn), F32)] if nk > 1 else []
    return pl.pallas_call(
        functools.partial(_mm_kernel, nk=nk, n_extra=len(extras), epilogue=epilogue),
        grid=(m // tm, n // tn, nk),
        in_specs=in_specs,
        out_specs=pl.BlockSpec((tm, tn), lambda i, j, k: (i, j)),
        out_shape=jax.ShapeDtypeStruct((m, n), out_dtype),
        scratch_shapes=scratch,
        compiler_params=_params(3),
        name=name,
    )(a, w, *[e[0] for e in extras])


def _epi_plain(acc):
    return acc


def _epi_headnorm(acc, g_ref):
    g = g_ref[...]
    outs = []
    for h in range(acc.shape[1] // HEAD_DIM):
        a = acc[:, h * HEAD_DIM:(h + 1) * HEAD_DIM]
        r = lax.rsqrt(jnp.mean(a * a, axis=-1, keepdims=True) + EPS)
        outs.append((a * r) * g)
    return jnp.concatenate(outs, axis=1)


def _epi_rotary(acc, cos_ref, sin_ref):
    c = cos_ref[...]
    s = sin_ref[...]
    scale = jnp.where(pl.program_id(1) == 0, 1.0, RET_QK_DIM ** -0.5).astype(F32)
    outs = []
    for h in range(acc.shape[1] // RET_QK_DIM):
        a = acc[:, h * RET_QK_DIM:(h + 1) * RET_QK_DIM]
        swapped = pltpu.roll(a, RET_QK_DIM // 2, axis=1)
        outs.append((a * c + swapped * s) * scale)
    return jnp.concatenate(outs, axis=1)


def _epi_residual(acc, x_ref):
    return x_ref[...] + acc


def _epi_relu2(acc):
    r = jnp.maximum(acc, 0.0)
    return r * r


def _topk_mask_kernel(iq_ref, qpack_ref, kpack_ref, bias_ref, key_ref, sel_ref, *, q0, kend, tq, topk):
    qi = pl.program_id(1)
    ik = kpack_ref[:, :IDX_DIM].astype(BF16)
    w_t = qpack_ref[...].T
    score = None
    for h in range(IDX_HEADS):
        iqh = iq_ref[:, h * IDX_DIM:(h + 1) * IDX_DIM].astype(BF16)
        logits = lax.dot_general(ik, iqh, (((1,), (1,)), ((), ())), preferred_element_type=F32)
        w = w_t[IDX_DIM + h:IDX_DIM + h + 1, :] * (IDX_HEADS ** -0.5)
        term = jnp.maximum(logits, 0.0) * w
        score = term if score is None else score + term

    kpos = lax.broadcasted_iota(jnp.int32, (kend, tq), 0)
    qpos = lax.broadcasted_iota(jnp.int32, (kend, tq), 1) + (q0 + qi * tq)
    causal = kpos <= qpos
    score = jnp.where(score == 0.0, 0.0, score)
    bits = pltpu.bitcast(score, jnp.int32)
    key = bits ^ ((bits >> 31) & INT_MAX)
    key_ref[...] = jnp.where(causal, key, INT_MIN)

    def bisect(_, carry):
        lo, hi = carry
        mid = (lo >> 1) + (hi >> 1) + (lo & hi & 1)
        cnt = jnp.sum((key_ref[...] >= mid).astype(jnp.int32), axis=0, keepdims=True)
        ge = cnt >= topk
        return jnp.where(ge, mid, lo), jnp.where(ge, hi, mid)

    lo0 = jnp.full((1, tq), INT_MIN, jnp.int32)
    hi0 = jnp.full((1, tq), INT_MAX, jnp.int32)
    tau, _ = lax.fori_loop(0, 32, bisect, (lo0, hi0))

    key = key_ref[...]
    ge_tau = key >= tau
    sel_ref[...] = jnp.where(ge_tau, 0.0, NEG_BIG)
    cnt_ge = jnp.sum(ge_tau.astype(jnp.int32), axis=0, keepdims=True)
    tied = jnp.where((cnt_ge > topk) & (tau > INT_MIN), 1, 0)

    @pl.when(jnp.max(tied) > 0)
    def _():
        tc = 512
        row = lax.broadcasted_iota(jnp.int32, (tc, tc), 0)
        col = lax.broadcasted_iota(jnp.int32, (tc, tc), 1)
        lower = jnp.where(col <= row, 1.0, 0.0).astype(BF16)
        gt_all = key_ref[...] > tau
        need = (topk - jnp.sum(gt_all.astype(jnp.int32), axis=0, keepdims=True)).astype(F32)
        carry = jnp.zeros((1, tq), F32)
        for c in range(kend // tc):
            kc = key_ref[c * tc:(c + 1) * tc, :]
            eq = jnp.where(kc == tau, 1.0, 0.0)
            prefix = jnp.dot(lower, eq.astype(BF16), preferred_element_type=F32) + carry
            carry = carry + jnp.sum(eq, axis=0, keepdims=True)
            keep = (kc > tau) | ((kc == tau) & (prefix <= need))
            sel_ref[c * tc:(c + 1) * tc, :] = jnp.where(keep, 0.0, NEG_BIG)

    kpos = lax.broadcasted_iota(jnp.int32, (kend, tq), 0)
    qpos = lax.broadcasted_iota(jnp.int32, (kend, tq), 1) + (q0 + qi * tq)
    sel = jnp.where(kpos <= qpos, sel_ref[...], NEG_BIG)
    bias_ref[...] = sel.T.astype(bias_ref.dtype)


def _topk_mask(idx3, *, j, topk, tq=256):
    b, s, _ = idx3.shape
    kend = Q_SUPER * (j + 1)
    q0 = Q_SUPER * j
    nq = Q_SUPER // tq
    pack_blk = IDX_Q_W // LANES
    return pl.pallas_call(
        functools.partial(_topk_mask_kernel, q0=q0, kend=kend, tq=tq, topk=topk),
        grid=(b, nq),
        in_specs=[pl.BlockSpec((None, tq, IDX_Q_W), lambda bi, qi: (bi, q0 // tq + qi, 0)),
                  pl.BlockSpec((None, tq, LANES), lambda bi, qi: (bi, q0 // tq + qi, pack_blk)),
                  pl.BlockSpec((None, kend, LANES), lambda bi, qi: (bi, 0, pack_blk))],
        out_specs=pl.BlockSpec((None, tq, kend), lambda bi, qi: (bi, qi, 0)),
        out_shape=jax.ShapeDtypeStruct((b, Q_SUPER, kend), BF16),
        scratch_shapes=[pltpu.VMEM((kend, tq), jnp.int32), pltpu.VMEM((kend, tq), F32)],
        compiler_params=_params(2),
        name=f"topk_mask_{j}",
    )(idx3, idx3, idx3)


def _attn_kernel(q_ref, k_ref, v_ref, bias_ref, o_ref, *, tq):
    scale = HEAD_DIM ** -0.5
    bias = bias_ref[...].astype(F32)
    for g in range(ATTN_KV_HEADS):
        kg = k_ref[:, g * HEAD_DIM:(g + 1) * HEAD_DIM]
        vg = v_ref[:, g * HEAD_DIM:(g + 1) * HEAD_DIM]
        h0 = g * ATTN_GROUP
        qg = jnp.concatenate([q_ref[:, (h0 + r) * HEAD_DIM:(h0 + r + 1) * HEAD_DIM]
                              for r in range(ATTN_GROUP)], axis=0)
        s = lax.dot_general(qg, kg, (((1,), (1,)), ((), ())), preferred_element_type=F32)
        s = s * scale + jnp.concatenate([bias] * ATTN_GROUP, axis=0)
        m = jnp.max(s, axis=-1, keepdims=True)
        p = jnp.exp(s - m)
        l = jnp.sum(p, axis=-1, keepdims=True)
        o = jnp.dot(p.astype(BF16), vg, preferred_element_type=F32) / l
        for r in range(ATTN_GROUP):
            o_ref[:, (h0 + r) * HEAD_DIM:(h0 + r + 1) * HEAD_DIM] = o[r * tq:(r + 1) * tq].astype(o_ref.dtype)


def _attention(q3, k3, plain3, bias, *, j, tq=128):
    b = q3.shape[0]
    kend = Q_SUPER * (j + 1)
    q0 = Q_SUPER * j
    nq = Q_SUPER // tq
    return pl.pallas_call(
        functools.partial(_attn_kernel, tq=tq),
        grid=(b, nq),
        in_specs=[pl.BlockSpec((None, tq, ATTN_W), lambda bi, qi: (bi, q0 // tq + qi, 0)),
                  pl.BlockSpec((None, kend, KV_W), lambda bi, qi: (bi, 0, 0)),
                  pl.BlockSpec((None, kend, KV_W), lambda bi, qi: (bi, 0, 0)),
                  pl.BlockSpec((None, tq, kend), lambda bi, qi: (bi, qi, 0))],
        out_specs=pl.BlockSpec((None, tq, ATTN_W), lambda bi, qi: (bi, qi, 0)),
        out_shape=jax.ShapeDtypeStruct((b, Q_SUPER, ATTN_W), BF16),
        compiler_params=_params(2),
        name=f"attention_{j}",
    )(q3, k3, plain3, bias)


def _retention_kernel(q_ref, k_ref, v_ref, gate_ref, decay_ref, xi_ref, zeta_ref, cd_ref, g_ref, b_ref,
                      o_ref, *, n_chunks):
    c = RET_CHUNK
    decay = decay_ref[...]
    xi = xi_ref[...]
    zeta = zeta_ref[...]
    cd = cd_ref[...]
    gn_g = g_ref[...]
    gn_b = b_ref[...]
    state = jnp.zeros((RET_QK_DIM, RET_V_DIM), F32)
    for n in range(n_chunks):
        sl = slice(n * c, (n + 1) * c)
        q = q_ref[sl, :]
        k = k_ref[sl, :]
        v = v_ref[sl, :]
        qk = lax.dot_general(q, k, (((1,), (1,)), ((), ())), preferred_element_type=F32) * decay
        inner = jnp.dot(qk.astype(BF16), v, preferred_element_type=F32)
        cross = jnp.dot(q, state.astype(BF16), preferred_element_type=F32) * xi
        kz_t = (k.astype(F32) * zeta).T.astype(BF16)
        upd = jnp.dot(kz_t, v, preferred_element_type=F32)
        state = upd + cd * state
        y = inner + cross
        mu = jnp.mean(y, axis=-1, keepdims=True)
        d = y - mu
        var = jnp.mean(d * d, axis=-1, keepdims=True)
        yn = d * lax.rsqrt(var + EPS)
        z = yn * gn_g + gn_b
        gate = gate_ref[sl, :].astype(F32)
        silu = gate * (1.0 / (1.0 + jnp.exp(-gate)))
        o_ref[sl, :] = (silu * z).astype(o_ref.dtype)


def _retention_tables():
    c = RET_CHUNK
    lg = np.log(1.0 - np.exp2(-5.0 - np.arange(RET_HEADS, dtype=np.float32))).astype(np.float32)
    pos = np.arange(c, dtype=np.float32)
    diff = pos[:, None] - pos[None, :]
    decay = np.where(diff[None] >= 0, np.exp(lg[:, None, None] * np.maximum(diff, 0.0)[None]), 0.0)
    xi = np.exp(lg[:, None] * (pos[None, :] + 1.0))
    zeta = np.exp(lg[:, None] * (c - 1.0 - pos[None, :]))
    cd = np.exp(lg * c)
    f = lambda a: jnp.asarray(a.astype(np.float32))
    return (f(decay),
            f(np.broadcast_to(xi[:, :, None], (RET_HEADS, c, RET_V_DIM))),
            f(np.broadcast_to(zeta[:, :, None], (RET_HEADS, c, RET_QK_DIM))),
            f(np.broadcast_to(cd[:, None, None], (RET_HEADS, 1, RET_V_DIM))))


def _retention(rot3, plain3, gn_g, gn_b):
    b, s, _ = rot3.shape
    decay, xi, zeta, cd = _retention_tables()
    v_blk0 = KV_W // RET_V_DIM
    gate_blk0 = (KV_W + RET_V_W) // RET_V_DIM
    return pl.pallas_call(
        functools.partial(_retention_kernel, n_chunks=s // RET_CHUNK),
        grid=(b, RET_HEADS),
        in_specs=[pl.BlockSpec((None, s, RET_QK_DIM), lambda bi, h: (bi, 0, h)),
                  pl.BlockSpec((None, s, RET_QK_DIM), lambda bi, h: (bi, 0, RET_HEADS + h)),
                  pl.BlockSpec((None, s, RET_V_DIM), lambda bi, h: (bi, 0, v_blk0 + h)),
                  pl.BlockSpec((None, s, RET_V_DIM), lambda bi, h: (bi, 0, gate_blk0 + h)),
                  pl.BlockSpec((None, RET_CHUNK, RET_CHUNK), lambda bi, h: (h, 0, 0)),
                  pl.BlockSpec((None, RET_CHUNK, RET_V_DIM), lambda bi, h: (h, 0, 0)),
                  pl.BlockSpec((None, RET_CHUNK, RET_QK_DIM), lambda bi, h: (h, 0, 0)),
                  pl.BlockSpec((None, 1, RET_V_DIM), lambda bi, h: (h, 0, 0)),
                  pl.BlockSpec((1, RET_V_DIM), lambda bi, h: (0, h)),
                  pl.BlockSpec((1, RET_V_DIM), lambda bi, h: (0, h))],
        out_specs=pl.BlockSpec((None, s, RET_V_DIM), lambda bi, h: (bi, 0, h)),
        out_shape=jax.ShapeDtypeStruct((b, s, RET_V_W), BF16),
        compiler_params=_params(2),
        name="retention",
    )(rot3, rot3, plain3, plain3, decay, xi, zeta, cd, gn_g.reshape(1, -1), gn_b.reshape(1, -1))


def _merge_kernel(oa_ref, or_ref, wa_ref, wr_ref, ga_ref, gb_ref, o_ref):
    ya = jnp.dot(oa_ref[...], wa_ref[...], preferred_element_type=F32)
    yr = jnp.dot(or_ref[...], wr_ref[...], preferred_element_type=F32)
    ga = ga_ref[...].astype(F32)
    gb = gb_ref[...].astype(F32)
    sa = 1.0 / (1.0 + jnp.exp(-ga))
    sb = 1.0 / (1.0 + jnp.exp(-gb))
    o_ref[...] = (sa * ya + sb * yr).astype(o_ref.dtype)


def _merge(o_attn, o_ret, w_ua, w_ur, plain, tm=1024, tn=512):
    m = o_attn.shape[0]
    ga_blk0 = (KV_W + 2 * RET_V_W) // tn
    gb_blk0 = (KV_W + 2 * RET_V_W + D_MODEL) // tn
    return pl.pallas_call(
        _merge_kernel,
        grid=(m // tm, D_MODEL // tn),
        in_specs=[pl.BlockSpec((tm, ATTN_W), lambda i, j: (i, 0)),
                  pl.BlockSpec((tm, RET_V_W), lambda i, j: (i, 0)),
                  pl.BlockSpec((ATTN_W, tn), lambda i, j: (0, j)),
                  pl.BlockSpec((RET_V_W, tn), lambda i, j: (0, j)),
                  pl.BlockSpec((tm, tn), lambda i, j: (i, ga_blk0 + j)),
                  pl.BlockSpec((tm, tn), lambda i, j: (i, gb_blk0 + j))],
        out_specs=pl.BlockSpec((tm, tn), lambda i, j: (i, j)),
        out_shape=jax.ShapeDtypeStruct((m, D_MODEL), BF16),
        compiler_params=_params(2),
        name="gated_merge",
    )(o_attn, o_ret, w_ua, w_ur, plain, plain)


def _rotary_tables(s):
    pos = jnp.arange(s, dtype=F32)
    inv_freq = ROPE_BASE ** (-jnp.arange(0, RET_QK_DIM, 2, dtype=F32) / RET_QK_DIM)
    ang = pos[:, None] * inv_freq[None, :]
    cos, sin = jnp.cos(ang), jnp.sin(ang)
    return jnp.concatenate([cos, cos], axis=1), jnp.concatenate([-sin, sin], axis=1)


def kernel(x, ln1_g, w_in, q_norm_g, k_norm_g, ret_gn_g, ret_gn_b, w_up_attn, w_up_ret, w_out, ln2_g,
           w_ff1, w_ff2):
    b, s, d = x.shape
    m = b * s
    depth = w_in.shape[0]
    topk = min(TOPK_MAX, s // 4)
    assert s % Q_SUPER == 0 and topk <= Q_SUPER
    cos2, sin2 = _rotary_tables(s)
    tm = 1024
    s_tiles = s // tm

    o_q = 0
    o_k = o_q + ATTN_W
    o_v = o_k + KV_W
    o_iq = o_v + KV_W
    o_ik = o_iq + IDX_Q_W
    o_rq = o_ik + IDX_DIM + IDX_HEADS
    o_rv = o_rq + 2 * RET_QK_W
    o_end = o_rv + 2 * RET_V_W + 2 * D_MODEL

    xf = x.reshape(m, d)
    for l in range(depth):
        wl = w_in[l]
        w_q = wl[:, o_q:o_k].astype(BF16)
        w_k = wl[:, o_k:o_v].astype(BF16)
        w_idx = jnp.concatenate(
            [wl[:, o_iq:o_rq], jnp.zeros((d, IDX_PACK_W - (o_rq - o_iq)), F32)], axis=1).astype(BF16)
        w_rot = wl[:, o_rq:o_rv].astype(BF16)
        w_plain = jnp.concatenate([wl[:, o_v:o_iq], wl[:, o_rv:o_end]], axis=1).astype(BF16)

        h = _rmsnorm(xf, ln1_g[l])
        aq = _matmul(h, w_q, out_dtype=BF16, epilogue=_epi_headnorm, tm=tm, tn=1024, name="proj_q",
                     extras=[(q_norm_g[l].reshape(1, HEAD_DIM), (1, HEAD_DIM), lambda i, j: (0, 0))])
        ak = _matmul(h, w_k, out_dtype=BF16, epilogue=_epi_headnorm, tm=tm, tn=KV_W, name="proj_k",
                     extras=[(k_norm_g[l].reshape(1, HEAD_DIM), (1, HEAD_DIM), lambda i, j: (0, 0))])
        idx = _matmul(h, w_idx, out_dtype=F32, epilogue=_epi_plain, tm=tm, tn=IDX_PACK_W, name="proj_idx")
        rot = _matmul(h, w_rot, out_dtype=BF16, epilogue=_epi_rotary, tm=tm, tn=RET_QK_W, name="proj_rot",
                      extras=[(cos2, (tm, RET_QK_DIM), lambda i, j: (i % s_tiles, 0)),
                              (sin2, (tm, RET_QK_DIM), lambda i, j: (i % s_tiles, 0))])
        plain = _matmul(h, w_plain, out_dtype=BF16, epilogue=_epi_plain, tm=tm, tn=512, name="proj_plain")

        aq3 = aq.reshape(b, s, ATTN_W)
        ak3 = ak.reshape(b, s, KV_W)
        idx3 = idx.reshape(b, s, IDX_PACK_W)
        plain3 = plain.reshape(b, s, PLAIN_W)
        rot3 = rot.reshape(b, s, 2 * RET_QK_W)

        o_parts = []
        for j in range(s // Q_SUPER):
            bias = _topk_mask(idx3, j=j, topk=topk)
            o_parts.append(_attention(aq3, ak3, plain3, bias, j=j))
        o_attn = jnp.concatenate(o_parts, axis=1).reshape(m, ATTN_W)

        o_ret = _retention(rot3, plain3, ret_gn_g[l], ret_gn_b[l]).reshape(m, RET_V_W)

        merged = _merge(o_attn, o_ret, w_up_attn[l].astype(BF16), w_up_ret[l].astype(BF16), plain)
        xf = _matmul(merged, w_out[l].astype(BF16), out_dtype=F32, epilogue=_epi_residual, tm=tm, tn=1024,
                     name="proj_out", extras=[(xf, (tm, 1024), lambda i, j: (i, j))])

        h2 = _rmsnorm(xf, ln2_g[l])
        f = _matmul(h2, w_ff1[l].astype(BF16), out_dtype=BF16, epilogue=_epi_relu2, tm=tm, tn=1024, name="ff1")
        xf = _matmul(f, w_ff2[l].astype(BF16), out_dtype=F32, epilogue=_epi_residual, tm=tm, tn=1024, tk=2048,
                     name="ff2", extras=[(xf, (tm, 1024), lambda i, j: (i, j))])
    return xf.reshape(b, s, d)
```

```python
import functools
import math

import jax
import jax.numpy as jnp
import numpy as np
from jax import lax
from jax.experimental import pallas as pl
from jax.experimental.pallas import tpu as pltpu

D_MODEL = 2048
HEAD_DIM = 128
ATTN_Q_HEADS = 16
ATTN_KV_HEADS = 4
ATTN_GROUP = ATTN_Q_HEADS // ATTN_KV_HEADS
IDX_HEADS = 16
IDX_DIM = 64
TOPK_MAX = 256
RET_HEADS = 8
RET_QK_DIM = 128
RET_V_DIM = 256
RET_CHUNK = 128
ROPE_BASE = 10000.0
D_FF = 4 * D_MODEL
EPS = 1e-6

ATTN_W = ATTN_Q_HEADS * HEAD_DIM
KV_W = ATTN_KV_HEADS * HEAD_DIM
IDX_Q_W = IDX_HEADS * IDX_DIM
RET_QK_W = RET_HEADS * RET_QK_DIM
RET_V_W = RET_HEADS * RET_V_DIM

LANES = 128
PLAIN_W = 2 * RET_V_W + 2 * D_MODEL
COUNT_ROWS = 64
BISECT_MAX_STEPS = 2200
Q_SUPER = 512
NEG_BIG = -1e30
INT_MIN = -(2 ** 31)
INT_MAX = 2 ** 31 - 1
VMEM_LIMIT = 56 * 1024 * 1024

F32 = jnp.float32
BF16 = jnp.bfloat16


def _params(n_axes):
    return pltpu.CompilerParams(dimension_semantics=("arbitrary",) * n_axes,
                                vmem_limit_bytes=VMEM_LIMIT)


def _rmsnorm_kernel(x_ref, g_ref, o_ref):
    x = x_ref[...]
    r = lax.rsqrt(jnp.mean(x * x, axis=-1, keepdims=True) + EPS)
    o_ref[...] = ((x * r) * g_ref[...]).astype(o_ref.dtype)


def _rmsnorm(x, g, tm=512):
    m, d = x.shape
    return pl.pallas_call(
        _rmsnorm_kernel,
        grid=(m // tm,),
        in_specs=[pl.BlockSpec((tm, d), lambda i: (i, 0)),
                  pl.BlockSpec((1, d), lambda i: (0, 0))],
        out_specs=pl.BlockSpec((tm, d), lambda i: (i, 0)),
        out_shape=jax.ShapeDtypeStruct((m, d), BF16),
        compiler_params=_params(1),
        name="rmsnorm",
    )(x, g.reshape(1, d))


def _mm_kernel(*refs, nk, n_extra, epilogue):
    a_ref, w_ref = refs[0], refs[1]
    extra = refs[2:2 + n_extra]
    o_ref = refs[2 + n_extra]
    if nk == 1:
        acc = jnp.dot(a_ref[...], w_ref[...], preferred_element_type=F32)
        o_ref[...] = epilogue(acc, *extra).astype(o_ref.dtype)
        return
    acc_ref = refs[3 + n_extra]
    k = pl.program_id(2)

    @pl.when(k == 0)
    def _():
        acc_ref[...] = jnp.zeros_like(acc_ref)

    acc_ref[...] += jnp.dot(a_ref[...], w_ref[...], preferred_element_type=F32)

    @pl.when(k == nk - 1)
    def _():
        o_ref[...] = epilogue(acc_ref[...], *extra).astype(o_ref.dtype)


def _matmul(a, w, *, out_dtype, epilogue, extras=(), tm=1024, tn=1024, tk=None, col0=0, n=None, name="matmul"):
    m, kdim = a.shape
    n = w.shape[1] - col0 if n is None else n
    tk = kdim if tk is None else tk
    nk = kdim // tk
    tn = min(tn, n)
    assert m % tm == 0 and n % tn == 0 and kdim % tk == 0 and col0 % tn == 0
    jb0 = col0 // tn
    in_specs = [pl.BlockSpec((tm, tk), lambda i, j, k: (i, k)),
                pl.BlockSpec((tk, tn), lambda i, j, k: (k, jb0 + j))]
    for _, bshape, imap in extras:
        in_specs.append(pl.BlockSpec(bshape, functools.partial(lambda i, j, k, f: f(i, j), f=imap)))
    scratch = [pltpu.VMEM((tm, tn), F32)] if nk > 1 else []
    return pl.pallas_call(
        functools.partial(_mm_kernel, nk=nk, n_extra=len(extras), epilogue=epilogue),
        grid=(m // tm, n // tn, nk),
        in_specs=in_specs,
        out_specs=pl.BlockSpec((tm, tn), lambda i, j, k: (i, j)),
        out_shape=jax.ShapeDtypeStruct((m, n), out_dtype),
        scratch_shapes=scratch,
        compiler_params=_params(3),
        name=name,
    )(a, w, *[e[0] for e in extras])


def _epi_plain(acc):
    return acc


def _epi_headnorm(acc, g_ref):
    g = g_ref[...]
    outs = []
    for h in range(acc.shape[1] // HEAD_DIM):
        a = acc[:, h * HEAD_DIM:(h + 1) * HEAD_DIM]
        r = lax.rsqrt(jnp.mean(a * a, axis=-1, keepdims=True) + EPS)
        outs.append((a * r) * g)
    return jnp.concatenate(outs, axis=1)


def _epi_kv(acc, g_ref):
    return jnp.where(pl.program_id(1) == 0, _epi_headnorm(acc, g_ref), acc)


def _epi_rotary(acc, cos_ref, sin_ref):
    c = cos_ref[...]
    s = sin_ref[...]
    scale = jnp.where(pl.program_id(1) == 0, 1.0, RET_QK_DIM ** -0.5).astype(F32)
    outs = []
    for h in range(acc.shape[1] // RET_QK_DIM):
        a = acc[:, h * RET_QK_DIM:(h + 1) * RET_QK_DIM]
        swapped = pltpu.roll(a, RET_QK_DIM // 2, axis=1)
        outs.append((a * c + swapped * s) * scale)
    return jnp.concatenate(outs, axis=1)


def _epi_residual(acc, x_ref):
    return x_ref[...] + acc


def _epi_relu2(acc):
    r = jnp.maximum(acc, 0.0)
    return r * r


def _topk_mask_kernel(iq_ref, qpack_ref, kpack_ref, bias_ref, score_ref, sel_ref, *, q0, kend, tq, topk):
    rows = COUNT_ROWS
    qbase = q0 + pl.program_id(1) * tq
    neg_inf = float("-inf")

    ik = kpack_ref[:, :IDX_DIM].astype(BF16)
    w_t = qpack_ref[...].T * (IDX_HEADS ** -0.5)
    score = None
    for h in range(IDX_HEADS):
        iqh = iq_ref[:, h * IDX_DIM:(h + 1) * IDX_DIM].astype(BF16)
        logits = lax.dot_general(ik, iqh, (((1,), (1,)), ((), ())), preferred_element_type=F32)
        term = jnp.maximum(logits, 0.0) * w_t[IDX_DIM + h:IDX_DIM + h + 1, :]
        score = term if score is None else score + term

    def causal_mask():
        kpos = lax.broadcasted_iota(jnp.int32, (kend, tq), 0)
        qcol = lax.broadcasted_iota(jnp.int32, (kend, tq), 1) + qbase
        return kpos <= qcol

    causal = causal_mask()
    score = jnp.where(score == 0.0, 0.0, score)
    lo_fill = jnp.where(causal, score, neg_inf)
    score_ref[...] = lo_fill
    smax = jnp.max(jnp.max(lo_fill.reshape(kend // rows, rows, tq), axis=0), axis=0, keepdims=True)
    hi_fill = jnp.where(causal, score, float("inf"))
    smin = jnp.min(jnp.min(hi_fill.reshape(kend // rows, rows, tq), axis=0), axis=0, keepdims=True)
    n_causal = jnp.minimum(lax.broadcasted_iota(jnp.int32, (1, tq), 1) + (qbase + 1), kend)

    def count_ge(thr):
        thr_b = jnp.broadcast_to(thr, (rows, tq))
        part = jnp.zeros((rows, tq), F32)
        for i in range(kend // rows):
            part = part + jnp.where(score_ref[i * rows:(i + 1) * rows, :] >= thr_b, 1.0, 0.0)
        return jnp.sum(part, axis=0, keepdims=True)

    kf = float(topk)

    def cond(c):
        return (c[1] > 0) & (c[0] < BISECT_MAX_STEPS)

    def body(c):
        it, _, lo, hi, tau, act = c
        first = (jnp.zeros((1, tq), jnp.int32) + it) == 0
        mid = jnp.where(first, hi, 0.5 * lo + 0.5 * hi)
        stuck = jnp.logical_not(first) & ((mid <= lo) | (mid >= hi))
        cnt = count_ge(mid)
        ge = cnt >= kf
        done = stuck | (cnt == kf) | (first & ge)
        tau = jnp.where((act > 0) & done, jnp.where(stuck, lo, mid), tau)
        lo = jnp.where(ge, mid, lo)
        hi = jnp.where(ge, hi, mid)
        act = jnp.where(done, 0, act)
        return it + 1, jnp.sum(act), lo, hi, tau, act

    lo0 = jnp.where(n_causal >= topk, smin, neg_inf)
    init = (jnp.int32(0), jnp.int32(tq), lo0, smax, jnp.full((1, tq), neg_inf, F32), jnp.ones((1, tq), jnp.int32))
    tau = lax.while_loop(cond, body, init)[4]

    sel_ref[...] = jnp.where(score_ref[...] >= tau, 0.0, NEG_BIG)
    tied = jnp.where((count_ge(tau) > kf) & (tau > neg_inf), 1, 0)

    @pl.when(jnp.max(tied) > 0)
    def _():
        tc = 512
        row = lax.broadcasted_iota(jnp.int32, (tc, tc), 0)
        col = lax.broadcasted_iota(jnp.int32, (tc, tc), 1)
        lower = jnp.where(col <= row, 1.0, 0.0).astype(BF16)
        n_gt = jnp.sum(jnp.where(score_ref[...] > tau, 1.0, 0.0), axis=0, keepdims=True)
        need = kf - n_gt
        carry = jnp.zeros((1, tq), F32)
        for c in range(kend // tc):
            sc = score_ref[c * tc:(c + 1) * tc, :]
            eq = jnp.where(sc == tau, 1.0, 0.0)
            prefix = jnp.dot(lower, eq.astype(BF16), preferred_element_type=F32) + carry
            carry = carry + jnp.sum(eq, axis=0, keepdims=True)
            keep = (sc > tau) | ((sc == tau) & (prefix <= need))
            sel_ref[c * tc:(c + 1) * tc, :] = jnp.where(keep, 0.0, NEG_BIG)

    sel = jnp.where(causal_mask(), sel_ref[...], NEG_BIG)
    bias_ref[...] = sel.T.astype(bias_ref.dtype)


def _topk_mask(iq3, ikw3, *, j, topk, tq=256):
    b = iq3.shape[0]
    kend = Q_SUPER * (j + 1)
    q0 = Q_SUPER * j
    nq = Q_SUPER // tq
    return pl.pallas_call(
        functools.partial(_topk_mask_kernel, q0=q0, kend=kend, tq=tq, topk=topk),
        grid=(b, nq),
        in_specs=[pl.BlockSpec((None, tq, IDX_Q_W), lambda bi, qi: (bi, q0 // tq + qi, 0)),
                  pl.BlockSpec((None, tq, LANES), lambda bi, qi: (bi, q0 // tq + qi, 0)),
                  pl.BlockSpec((None, kend, LANES), lambda bi, qi: (bi, 0, 0))],
        out_specs=pl.BlockSpec((None, tq, kend), lambda bi, qi: (bi, qi, 0)),
        out_shape=jax.ShapeDtypeStruct((b, Q_SUPER, kend), BF16),
        scratch_shapes=[pltpu.VMEM((kend, tq), F32), pltpu.VMEM((kend, tq), F32)],
        compiler_params=_params(2),
        name=f"topk_mask_{j}",
    )(iq3, ikw3, ikw3)


def _attn_kernel(q_ref, k_ref, v_ref, bias_ref, o_ref, *, tq):
    scale = HEAD_DIM ** -0.5
    bias = bias_ref[...].astype(F32)
    for g in range(ATTN_KV_HEADS):
        kg = k_ref[:, g * HEAD_DIM:(g + 1) * HEAD_DIM]
        vg = v_ref[:, g * HEAD_DIM:(g + 1) * HEAD_DIM]
        h0 = g * ATTN_GROUP
        qg = jnp.concatenate([q_ref[:, (h0 + r) * HEAD_DIM:(h0 + r + 1) * HEAD_DIM]
                              for r in range(ATTN_GROUP)], axis=0)
        s = lax.dot_general(qg, kg, (((1,), (1,)), ((), ())), preferred_element_type=F32)
        s = s * scale + jnp.concatenate([bias] * ATTN_GROUP, axis=0)
        m = jnp.max(s, axis=-1, keepdims=True)
        p = jnp.exp(s - m)
        l = jnp.sum(p, axis=-1, keepdims=True)
        o = jnp.dot(p.astype(BF16), vg, preferred_element_type=F32) / l
        for r in range(ATTN_GROUP):
            o_ref[:, (h0 + r) * HEAD_DIM:(h0 + r + 1) * HEAD_DIM] = o[r * tq:(r + 1) * tq].astype(o_ref.dtype)


def _attention(q3, kv3, bias, *, j, tq=128):
    b = q3.shape[0]
    kend = Q_SUPER * (j + 1)
    q0 = Q_SUPER * j
    nq = Q_SUPER // tq
    return pl.pallas_call(
        functools.partial(_attn_kernel, tq=tq),
        grid=(b, nq),
        in_specs=[pl.BlockSpec((None, tq, ATTN_W), lambda bi, qi: (bi, q0 // tq + qi, 0)),
                  pl.BlockSpec((None, kend, KV_W), lambda bi, qi: (bi, 0, 0)),
                  pl.BlockSpec((None, kend, KV_W), lambda bi, qi: (bi, 0, 1)),
                  pl.BlockSpec((None, tq, kend), lambda bi, qi: (bi, qi, 0))],
        out_specs=pl.BlockSpec((None, tq, ATTN_W), lambda bi, qi: (bi, qi, 0)),
        out_shape=jax.ShapeDtypeStruct((b, Q_SUPER, ATTN_W), BF16),
        compiler_params=_params(2),
        name=f"attention_{j}",
    )(q3, kv3, kv3, bias)


def _retention_kernel(q_ref, k_ref, v_ref, gate_ref, decay_ref, xi_ref, zeta_ref, cd_ref, g_ref, b_ref,
                      o_ref, *, n_chunks):
    c = RET_CHUNK
    decay = decay_ref[...]
    xi = xi_ref[...]
    zeta = zeta_ref[...]
    cd = cd_ref[...]
    gn_g = g_ref[...]
    gn_b = b_ref[...]
    state = jnp.zeros((RET_QK_DIM, RET_V_DIM), F32)
    for n in range(n_chunks):
        sl = slice(n * c, (n + 1) * c)
        q = q_ref[sl, :]
        k = k_ref[sl, :]
        v = v_ref[sl, :]
        qk = lax.dot_general(q, k, (((1,), (1,)), ((), ())), preferred_element_type=F32) * decay
        inner = jnp.dot(qk.astype(BF16), v, preferred_element_type=F32)
        cross = jnp.dot(q, state.astype(BF16), preferred_element_type=F32) * xi
        kz_t = (k.astype(F32) * zeta).T.astype(BF16)
        upd = jnp.dot(kz_t, v, preferred_element_type=F32)
        state = upd + cd * state
        y = inner + cross
        mu = jnp.mean(y, axis=-1, keepdims=True)
        d = y - mu
        var = jnp.mean(d * d, axis=-1, keepdims=True)
        yn = d * lax.rsqrt(var + EPS)
        z = yn * gn_g + gn_b
        gate = gate_ref[sl, :].astype(F32)
        silu = gate * (1.0 / (1.0 + jnp.exp(-gate)))
        o_ref[sl, :] = (silu * z).astype(o_ref.dtype)


def _retention_tables():
    c = RET_CHUNK
    lg = np.log(1.0 - np.exp2(-5.0 - np.arange(RET_HEADS, dtype=np.float32))).astype(np.float32)
    pos = np.arange(c, dtype=np.float32)
    diff = pos[:, None] - pos[None, :]
    decay = np.where(diff[None] >= 0, np.exp(lg[:, None, None] * np.maximum(diff, 0.0)[None]), 0.0)
    xi = np.exp(lg[:, None] * (pos[None, :] + 1.0))
    zeta = np.exp(lg[:, None] * (c - 1.0 - pos[None, :]))
    cd = np.exp(lg * c)
    f = lambda a: jnp.asarray(a.astype(np.float32))
    return (f(decay),
            f(np.broadcast_to(xi[:, :, None], (RET_HEADS, c, RET_V_DIM))),
            f(np.broadcast_to(zeta[:, :, None], (RET_HEADS, c, RET_QK_DIM))),
            f(np.broadcast_to(cd[:, None, None], (RET_HEADS, 1, RET_V_DIM))))


def _retention(rot3, plain3, gn_g, gn_b):
    b, s, _ = rot3.shape
    decay, xi, zeta, cd = _retention_tables()
    v_blk0 = 0
    gate_blk0 = RET_V_W // RET_V_DIM
    return pl.pallas_call(
        functools.partial(_retention_kernel, n_chunks=s // RET_CHUNK),
        grid=(b, RET_HEADS),
        in_specs=[pl.BlockSpec((None, s, RET_QK_DIM), lambda bi, h: (bi, 0, h)),
                  pl.BlockSpec((None, s, RET_QK_DIM), lambda bi, h: (bi, 0, RET_HEADS + h)),
                  pl.BlockSpec((None, s, RET_V_DIM), lambda bi, h: (bi, 0, v_blk0 + h)),
                  pl.BlockSpec((None, s, RET_V_DIM), lambda bi, h: (bi, 0, gate_blk0 + h)),
                  pl.BlockSpec((None, RET_CHUNK, RET_CHUNK), lambda bi, h: (h, 0, 0)),
                  pl.BlockSpec((None, RET_CHUNK, RET_V_DIM), lambda bi, h: (h, 0, 0)),
                  pl.BlockSpec((None, RET_CHUNK, RET_QK_DIM), lambda bi, h: (h, 0, 0)),
                  pl.BlockSpec((None, 1, RET_V_DIM), lambda bi, h: (h, 0, 0)),
                  pl.BlockSpec((1, RET_V_DIM), lambda bi, h: (0, h)),
                  pl.BlockSpec((1, RET_V_DIM), lambda bi, h: (0, h))],
        out_specs=pl.BlockSpec((None, s, RET_V_DIM), lambda bi, h: (bi, 0, h)),
        out_shape=jax.ShapeDtypeStruct((b, s, RET_V_W), BF16),
        compiler_params=_params(2),
        name="retention",
    )(rot3, rot3, plain3, plain3, decay, xi, zeta, cd, gn_g.reshape(1, -1), gn_b.reshape(1, -1))


def _merge_kernel(oa_ref, or_ref, wa_ref, wr_ref, ga_ref, gb_ref, o_ref):
    ya = jnp.dot(oa_ref[...], wa_ref[...], preferred_element_type=F32)
    yr = jnp.dot(or_ref[...], wr_ref[...], preferred_element_type=F32)
    ga = ga_ref[...].astype(F32)
    gb = gb_ref[...].astype(F32)
    sa = 1.0 / (1.0 + jnp.exp(-ga))
    sb = 1.0 / (1.0 + jnp.exp(-gb))
    o_ref[...] = (sa * ya + sb * yr).astype(o_ref.dtype)


def _merge(o_attn, o_ret, w_ua, w_ur, plain, tm=1024, tn=512):
    m = o_attn.shape[0]
    ga_blk0 = (2 * RET_V_W) // tn
    gb_blk0 = (2 * RET_V_W + D_MODEL) // tn
    return pl.pallas_call(
        _merge_kernel,
        grid=(m // tm, D_MODEL // tn),
        in_specs=[pl.BlockSpec((tm, ATTN_W), lambda i, j: (i, 0)),
                  pl.BlockSpec((tm, RET_V_W), lambda i, j: (i, 0)),
                  pl.BlockSpec((ATTN_W, tn), lambda i, j: (0, j)),
                  pl.BlockSpec((RET_V_W, tn), lambda i, j: (0, j)),
                  pl.BlockSpec((tm, tn), lambda i, j: (i, ga_blk0 + j)),
                  pl.BlockSpec((tm, tn), lambda i, j: (i, gb_blk0 + j))],
        out_specs=pl.BlockSpec((tm, tn), lambda i, j: (i, j)),
        out_shape=jax.ShapeDtypeStruct((m, D_MODEL), BF16),
        compiler_params=_params(2),
        name="gated_merge",
    )(o_attn, o_ret, w_ua, w_ur, plain, plain)


def _rotary_tables(s):
    pos = jnp.arange(s, dtype=F32)
    inv_freq = ROPE_BASE ** (-jnp.arange(0, RET_QK_DIM, 2, dtype=F32) / RET_QK_DIM)
    ang = pos[:, None] * inv_freq[None, :]
    cos, sin = jnp.cos(ang), jnp.sin(ang)
    return jnp.concatenate([cos, cos], axis=1), jnp.concatenate([-sin, sin], axis=1)


def kernel(x, ln1_g, w_in, q_norm_g, k_norm_g, ret_gn_g, ret_gn_b, w_up_attn, w_up_ret, w_out, ln2_g,
           w_ff1, w_ff2):
    b, s, d = x.shape
    m = b * s
    depth = w_in.shape[0]
    topk = min(TOPK_MAX, s // 4)
    assert s % Q_SUPER == 0 and topk <= Q_SUPER
    cos2, sin2 = _rotary_tables(s)
    tm = 1024
    s_tiles = s // tm

    src_ik = ATTN_W + 2 * KV_W + IDX_Q_W
    src_rq = src_ik + IDX_DIM + IDX_HEADS
    c_kv = ATTN_W
    c_iq = c_kv + 2 * KV_W
    c_rot = c_iq + IDX_Q_W
    c_plain = c_rot + 2 * RET_QK_W
    c_ikw = c_plain + PLAIN_W

    xf = x.reshape(m, d)
    for l in range(depth):
        wl = w_in[l]
        w_pack = jnp.concatenate(
            [wl[:, :src_ik], wl[:, src_rq:], wl[:, src_ik:src_rq],
             jnp.zeros((d, LANES - (src_rq - src_ik)), F32)], axis=1).astype(BF16)

        h = _rmsnorm(xf, ln1_g[l])
        aq = _matmul(h, w_pack, out_dtype=BF16, epilogue=_epi_headnorm, tm=tm, tn=1024, col0=0, n=ATTN_W,
                     name="proj_q",
                     extras=[(q_norm_g[l].reshape(1, HEAD_DIM), (1, HEAD_DIM), lambda i, j: (0, 0))])
        kv = _matmul(h, w_pack, out_dtype=BF16, epilogue=_epi_kv, tm=tm, tn=KV_W, col0=c_kv, n=2 * KV_W,
                     name="proj_kv",
                     extras=[(k_norm_g[l].reshape(1, HEAD_DIM), (1, HEAD_DIM), lambda i, j: (0, 0))])
        iq = _matmul(h, w_pack, out_dtype=F32, epilogue=_epi_plain, tm=tm, tn=IDX_Q_W, col0=c_iq, n=IDX_Q_W,
                     name="proj_iq")
        ikw = _matmul(h, w_pack, out_dtype=F32, epilogue=_epi_plain, tm=tm, tn=LANES, col0=c_ikw, n=LANES,
                      name="proj_ikw")
        rot = _matmul(h, w_pack, out_dtype=BF16, epilogue=_epi_rotary, tm=tm, tn=RET_QK_W, col0=c_rot,
                      n=2 * RET_QK_W, name="proj_rot",
                      extras=[(cos2, (tm, RET_QK_DIM), lambda i, j: (i % s_tiles, 0)),
                              (sin2, (tm, RET_QK_DIM), lambda i, j: (i % s_tiles, 0))])
        plain = _matmul(h, w_pack, out_dtype=BF16, epilogue=_epi_plain, tm=tm, tn=1024, col0=c_plain, n=PLAIN_W,
                        name="proj_plain")

        aq3 = aq.reshape(b, s, ATTN_W)
        kv3 = kv.reshape(b, s, 2 * KV_W)
        iq3 = iq.reshape(b, s, IDX_Q_W)
        ikw3 = ikw.reshape(b, s, LANES)
        plain3 = plain.reshape(b, s, PLAIN_W)
        rot3 = rot.reshape(b, s, 2 * RET_QK_W)

        o_parts = []
        for j in range(s // Q_SUPER):
            bias = _topk_mask(iq3, ikw3, j=j, topk=topk)
            o_parts.append(_attention(aq3, kv3, bias, j=j))
        o_attn = jnp.concatenate(o_parts, axis=1).reshape(m, ATTN_W)

        o_ret = _retention(rot3, plain3, ret_gn_g[l], ret_gn_b[l]).reshape(m, RET_V_W)

        merged = _merge(o_attn, o_ret, w_up_attn[l].astype(BF16), w_up_ret[l].astype(BF16), plain)
        xf = _matmul(merged, w_out[l].astype(BF16), out_dtype=F32, epilogue=_epi_residual, tm=tm, tn=1024,
                     name="proj_out", extras=[(xf, (tm, 1024), lambda i, j: (i, j))])

        h2 = _rmsnorm(xf, ln2_g[l])
        f = _matmul(h2, w_ff1[l].astype(BF16), out_dtype=BF16, epilogue=_epi_relu2, tm=tm, tn=1024, name="ff1")
        xf = _matmul(f, w_ff2[l].astype(BF16), out_dtype=F32, epilogue=_epi_residual, tm=tm, tn=1024, tk=2048,
                     name="ff2", extras=[(xf, (tm, 1024), lambda i, j: (i, j))])
    return xf.reshape(b, s, d)
```

```python
import functools
import math

import jax
import jax.numpy as jnp
import numpy as np
from jax import lax
from jax.experimental import pallas as pl
from jax.experimental.pallas import tpu as pltpu

D_MODEL = 2048
HEAD_DIM = 128
ATTN_Q_HEADS = 16
ATTN_KV_HEADS = 4
ATTN_GROUP = ATTN_Q_HEADS // ATTN_KV_HEADS
IDX_HEADS = 16
IDX_DIM = 64
TOPK_MAX = 256
RET_HEADS = 8
RET_QK_DIM = 128
RET_V_DIM = 256
RET_CHUNK = 128
ROPE_BASE = 10000.0
D_FF = 4 * D_MODEL
EPS = 1e-6

ATTN_W = ATTN_Q_HEADS * HEAD_DIM
KV_W = ATTN_KV_HEADS * HEAD_DIM
IDX_Q_W = IDX_HEADS * IDX_DIM
RET_QK_W = RET_HEADS * RET_QK_DIM
RET_V_W = RET_HEADS * RET_V_DIM

LANES = 128
PLAIN_W = 2 * RET_V_W + 2 * D_MODEL
COUNT_ROWS = 64
BISECT_MAX_STEPS = 2200
Q_SUPER = 512
NEG_BIG = -1e30
INT_MIN = -(2 ** 31)
INT_MAX = 2 ** 31 - 1
VMEM_LIMIT = 56 * 1024 * 1024

F32 = jnp.float32
BF16 = jnp.bfloat16


def _params(n_axes):
    return pltpu.CompilerParams(dimension_semantics=("arbitrary",) * n_axes,
                                vmem_limit_bytes=VMEM_LIMIT)


def _rmsnorm_kernel(x_ref, g_ref, o_ref):
    x = x_ref[...]
    r = lax.rsqrt(jnp.mean(x * x, axis=-1, keepdims=True) + EPS)
    o_ref[...] = ((x * r) * g_ref[...]).astype(o_ref.dtype)


def _rmsnorm(x, g, tm=512):
    m, d = x.shape
    return pl.pallas_call(
        _rmsnorm_kernel,
        grid=(m // tm,),
        in_specs=[pl.BlockSpec((tm, d), lambda i: (i, 0)),
                  pl.BlockSpec((1, d), lambda i: (0, 0))],
        out_specs=pl.BlockSpec((tm, d), lambda i: (i, 0)),
        out_shape=jax.ShapeDtypeStruct((m, d), BF16),
        compiler_params=_params(1),
        name="rmsnorm",
    )(x, g.reshape(1, d))


def _mm_kernel(*refs, nk, n_extra, epilogue, cache_w):
    a_ref, w_ref = refs[0], refs[1]
    extra = refs[2:2 + n_extra]
    o_ref = refs[2 + n_extra]
    scratch = refs[3 + n_extra:]
    i = pl.program_id(1)
    k = pl.program_id(2)
    if cache_w:
        wbf_ref = scratch[0]

        @pl.when(i == 0)
        def _():
            wbf_ref[k] = w_ref[...].astype(BF16)

        w = wbf_ref[k]
    else:
        w = w_ref[...]
    prod = jnp.dot(a_ref[...], w, preferred_element_type=F32)
    if nk == 1:
        o_ref[...] = epilogue(prod, *extra).astype(o_ref.dtype)
        return
    acc_ref = scratch[-1]

    @pl.when(k == 0)
    def _():
        acc_ref[...] = prod

    @pl.when(k > 0)
    def _():
        acc_ref[...] += prod

    @pl.when(k == nk - 1)
    def _():
        o_ref[...] = epilogue(acc_ref[...], *extra).astype(o_ref.dtype)


def _matmul(a, w3, layer, *, out_dtype, epilogue, extras=(), tm=1024, tn=1024, tk=None, col0=0, n=None,
            name="matmul"):
    m, kdim = a.shape
    n = w3.shape[2] - col0 if n is None else n
    tk = kdim if tk is None else tk
    nk = kdim // tk
    tn = min(tn, n)
    assert m % tm == 0 and n % tn == 0 and kdim % tk == 0 and col0 % tn == 0
    jb0 = col0 // tn
    cache_w = w3.dtype != BF16
    if cache_w:
        w_map = lambda j, i, k: (layer, jnp.where(i == 0, k, nk - 1), jb0 + j)
    else:
        w_map = lambda j, i, k: (layer, k, jb0 + j)
    in_specs = [pl.BlockSpec((tm, tk), lambda j, i, k: (i, k)),
                pl.BlockSpec((None, tk, tn), w_map)]
    for _, bshape, imap in extras:
        in_specs.append(pl.BlockSpec(bshape, functools.partial(lambda j, i, k, f: f(i, j), f=imap)))
    scratch = []
    if cache_w:
        scratch.append(pltpu.VMEM((nk, tk, tn), BF16))
    if nk > 1:
        scratch.append(pltpu.VMEM((tm, tn), F32))
    return pl.pallas_call(
        functools.partial(_mm_kernel, nk=nk, n_extra=len(extras), epilogue=epilogue, cache_w=cache_w),
        grid=(n // tn, m // tm, nk),
        in_specs=in_specs,
        out_specs=pl.BlockSpec((tm, tn), lambda j, i, k: (i, j)),
        out_shape=jax.ShapeDtypeStruct((m, n), out_dtype),
        scratch_shapes=scratch,
        compiler_params=_params(3),
        name=name,
    )(a, w3, *[e[0] for e in extras])


def _epi_plain(acc):
    return acc


def _epi_headnorm(acc, g_ref):
    g = g_ref[...]
    outs = []
    for h in range(acc.shape[1] // HEAD_DIM):
        a = acc[:, h * HEAD_DIM:(h + 1) * HEAD_DIM]
        r = lax.rsqrt(jnp.mean(a * a, axis=-1, keepdims=True) + EPS)
        outs.append((a * r) * g)
    return jnp.concatenate(outs, axis=1)


def _epi_kv(acc, g_ref):
    return jnp.where(pl.program_id(0) == 0, _epi_headnorm(acc, g_ref), acc)


def _epi_rotary(acc, cos_ref, sin_ref):
    c = cos_ref[...]
    s = sin_ref[...]
    scale = jnp.where(pl.program_id(0) == 0, 1.0, RET_QK_DIM ** -0.5).astype(F32)
    outs = []
    for h in range(acc.shape[1] // RET_QK_DIM):
        a = acc[:, h * RET_QK_DIM:(h + 1) * RET_QK_DIM]
        swapped = pltpu.roll(a, RET_QK_DIM // 2, axis=1)
        outs.append((a * c + swapped * s) * scale)
    return jnp.concatenate(outs, axis=1)


def _epi_residual(acc, x_ref):
    return x_ref[...] + acc


def _epi_relu2(acc):
    r = jnp.maximum(acc, 0.0)
    return r * r


def _topk_mask_kernel(iq_ref, qpack_ref, kpack_ref, bias_ref, score_ref, sel_ref, *, q0, kend, tq, topk):
    rows = COUNT_ROWS
    qbase = q0 + pl.program_id(1) * tq
    neg_inf = float("-inf")

    ik = kpack_ref[:, :IDX_DIM].astype(BF16)
    w_t = qpack_ref[...].T * (IDX_HEADS ** -0.5)
    score = None
    for h in range(IDX_HEADS):
        iqh = iq_ref[:, h * IDX_DIM:(h + 1) * IDX_DIM].astype(BF16)
        logits = lax.dot_general(ik, iqh, (((1,), (1,)), ((), ())), preferred_element_type=F32)
        term = jnp.maximum(logits, 0.0) * w_t[IDX_DIM + h:IDX_DIM + h + 1, :]
        score = term if score is None else score + term

    def causal_mask():
        kpos = lax.broadcasted_iota(jnp.int32, (kend, tq), 0)
        qcol = lax.broadcasted_iota(jnp.int32, (kend, tq), 1) + qbase
        return kpos <= qcol

    causal = causal_mask()
    score = jnp.where(score == 0.0, 0.0, score)
    lo_fill = jnp.where(causal, score, neg_inf)
    score_ref[...] = lo_fill
    smax = jnp.max(jnp.max(lo_fill.reshape(kend // rows, rows, tq), axis=0), axis=0, keepdims=True)
    hi_fill = jnp.where(causal, score, float("inf"))
    smin = jnp.min(jnp.min(hi_fill.reshape(kend // rows, rows, tq), axis=0), axis=0, keepdims=True)
    n_causal = jnp.minimum(lax.broadcasted_iota(jnp.int32, (1, tq), 1) + (qbase + 1), kend)

    def count_ge(thr):
        thr_b = jnp.broadcast_to(thr, (rows, tq))
        part = jnp.zeros((rows, tq), F32)
        for i in range(kend // rows):
            part = part + jnp.where(score_ref[i * rows:(i + 1) * rows, :] >= thr_b, 1.0, 0.0)
        return jnp.sum(part, axis=0, keepdims=True)

    kf = float(topk)

    def cond(c):
        return (c[1] > 0) & (c[0] < BISECT_MAX_STEPS)

    def body(c):
        it, _, lo, hi, tau, act = c
        first = (jnp.zeros((1, tq), jnp.int32) + it) == 0
        mid = jnp.where(first, hi, 0.5 * lo + 0.5 * hi)
        stuck = jnp.logical_not(first) & ((mid <= lo) | (mid >= hi))
        cnt = count_ge(mid)
        ge = cnt >= kf
        done = stuck | (cnt == kf) | (first & ge)
        tau = jnp.where((act > 0) & done, jnp.where(stuck, lo, mid), tau)
        lo = jnp.where(ge, mid, lo)
        hi = jnp.where(ge, hi, mid)
        act = jnp.where(done, 0, act)
        return it + 1, jnp.sum(act), lo, hi, tau, act

    lo0 = jnp.where(n_causal >= topk, smin, neg_inf)
    init = (jnp.int32(0), jnp.int32(tq), lo0, smax, jnp.full((1, tq), neg_inf, F32), jnp.ones((1, tq), jnp.int32))
    tau = lax.while_loop(cond, body, init)[4]

    sel_ref[...] = jnp.where(score_ref[...] >= tau, 0.0, NEG_BIG)
    tied = jnp.where((count_ge(tau) > kf) & (tau > neg_inf), 1, 0)

    @pl.when(jnp.max(tied) > 0)
    def _():
        tc = 512
        row = lax.broadcasted_iota(jnp.int32, (tc, tc), 0)
        col = lax.broadcasted_iota(jnp.int32, (tc, tc), 1)
        lower = jnp.where(col <= row, 1.0, 0.0).astype(BF16)
        n_gt = jnp.sum(jnp.where(score_ref[...] > tau, 1.0, 0.0), axis=0, keepdims=True)
        need = kf - n_gt
        carry = jnp.zeros((1, tq), F32)
        for c in range(kend // tc):
            sc = score_ref[c * tc:(c + 1) * tc, :]
            eq = jnp.where(sc == tau, 1.0, 0.0)
            prefix = jnp.dot(lower, eq.astype(BF16), preferred_element_type=F32) + carry
            carry = carry + jnp.sum(eq, axis=0, keepdims=True)
            keep = (sc > tau) | ((sc == tau) & (prefix <= need))
            sel_ref[c * tc:(c + 1) * tc, :] = jnp.where(keep, 0.0, NEG_BIG)

    sel = jnp.where(causal_mask(), sel_ref[...], NEG_BIG)
    bias_ref[...] = sel.T.astype(bias_ref.dtype)


def _topk_mask(iq3, ikw3, *, j, topk, tq=256):
    b = iq3.shape[0]
    kend = Q_SUPER * (j + 1)
    q0 = Q_SUPER * j
    nq = Q_SUPER // tq
    return pl.pallas_call(
        functools.partial(_topk_mask_kernel, q0=q0, kend=kend, tq=tq, topk=topk),
        grid=(b, nq),
        in_specs=[pl.BlockSpec((None, tq, IDX_Q_W), lambda bi, qi: (bi, q0 // tq + qi, 0)),
                  pl.BlockSpec((None, tq, LANES), lambda bi, qi: (bi, q0 // tq + qi, 0)),
                  pl.BlockSpec((None, kend, LANES), lambda bi, qi: (bi, 0, 0))],
        out_specs=pl.BlockSpec((None, tq, kend), lambda bi, qi: (bi, qi, 0)),
        out_shape=jax.ShapeDtypeStruct((b, Q_SUPER, kend), BF16),
        scratch_shapes=[pltpu.VMEM((kend, tq), F32), pltpu.VMEM((kend, tq), F32)],
        compiler_params=_params(2),
        name=f"topk_mask_{j}",
    )(iq3, ikw3, ikw3)


def _attn_kernel(q_ref, k_ref, v_ref, bias_ref, o_ref, *, tq):
    scale = HEAD_DIM ** -0.5
    bias = bias_ref[...].astype(F32)
    for g in range(ATTN_KV_HEADS):
        kg = k_ref[:, g * HEAD_DIM:(g + 1) * HEAD_DIM]
        vg = v_ref[:, g * HEAD_DIM:(g + 1) * HEAD_DIM]
        h0 = g * ATTN_GROUP
        qg = jnp.concatenate([q_ref[:, (h0 + r) * HEAD_DIM:(h0 + r + 1) * HEAD_DIM]
                              for r in range(ATTN_GROUP)], axis=0)
        s = lax.dot_general(qg, kg, (((1,), (1,)), ((), ())), preferred_element_type=F32)
        s = s * scale + jnp.concatenate([bias] * ATTN_GROUP, axis=0)
        m = jnp.max(s, axis=-1, keepdims=True)
        p = jnp.exp(s - m)
        l = jnp.sum(p, axis=-1, keepdims=True)
        o = jnp.dot(p.astype(BF16), vg, preferred_element_type=F32) / l
        for r in range(ATTN_GROUP):
            o_ref[:, (h0 + r) * HEAD_DIM:(h0 + r + 1) * HEAD_DIM] = o[r * tq:(r + 1) * tq].astype(o_ref.dtype)


def _attention(q3, kv3, bias, *, j, tq=128):
    b = q3.shape[0]
    kend = Q_SUPER * (j + 1)
    q0 = Q_SUPER * j
    nq = Q_SUPER // tq
    return pl.pallas_call(
        functools.partial(_attn_kernel, tq=tq),
        grid=(b, nq),
        in_specs=[pl.BlockSpec((None, tq, ATTN_W), lambda bi, qi: (bi, q0 // tq + qi, 0)),
                  pl.BlockSpec((None, kend, KV_W), lambda bi, qi: (bi, 0, 0)),
                  pl.BlockSpec((None, kend, KV_W), lambda bi, qi: (bi, 0, 1)),
                  pl.BlockSpec((None, tq, kend), lambda bi, qi: (bi, qi, 0))],
        out_specs=pl.BlockSpec((None, tq, ATTN_W), lambda bi, qi: (bi, qi, 0)),
        out_shape=jax.ShapeDtypeStruct((b, Q_SUPER, ATTN_W), BF16),
        compiler_params=_params(2),
        name=f"attention_{j}",
    )(q3, kv3, kv3, bias)


def _retention_kernel(q_ref, k_ref, v_ref, gate_ref, decay_ref, xi_ref, zeta_ref, cd_ref, g_ref, b_ref,
                      o_ref, *, n_chunks):
    c = RET_CHUNK
    decay = decay_ref[...]
    xi = xi_ref[...]
    zeta = zeta_ref[...]
    cd = cd_ref[...]
    gn_g = g_ref[...]
    gn_b = b_ref[...]
    state = jnp.zeros((RET_QK_DIM, RET_V_DIM), F32)
    for n in range(n_chunks):
        sl = slice(n * c, (n + 1) * c)
        q = q_ref[sl, :]
        k = k_ref[sl, :]
        v = v_ref[sl, :]
        qk = lax.dot_general(q, k, (((1,), (1,)), ((), ())), preferred_element_type=F32) * decay
        inner = jnp.dot(qk.astype(BF16), v, preferred_element_type=F32)
        cross = jnp.dot(q, state.astype(BF16), preferred_element_type=F32) * xi
        kz_t = (k.astype(F32) * zeta).T.astype(BF16)
        upd = jnp.dot(kz_t, v, preferred_element_type=F32)
        state = upd + cd * state
        y = inner + cross
        mu = jnp.mean(y, axis=-1, keepdims=True)
        d = y - mu
        var = jnp.mean(d * d, axis=-1, keepdims=True)
        yn = d * lax.rsqrt(var + EPS)
        z = yn * gn_g + gn_b
        gate = gate_ref[sl, :].astype(F32)
        silu = gate * (1.0 / (1.0 + jnp.exp(-gate)))
        o_ref[sl, :] = (silu * z).astype(o_ref.dtype)


def _retention_tables():
    c = RET_CHUNK
    lg = np.log(1.0 - np.exp2(-5.0 - np.arange(RET_HEADS, dtype=np.float32))).astype(np.float32)
    pos = np.arange(c, dtype=np.float32)
    diff = pos[:, None] - pos[None, :]
    decay = np.where(diff[None] >= 0, np.exp(lg[:, None, None] * np.maximum(diff, 0.0)[None]), 0.0)
    xi = np.exp(lg[:, None] * (pos[None, :] + 1.0))
    zeta = np.exp(lg[:, None] * (c - 1.0 - pos[None, :]))
    cd = np.exp(lg * c)
    f = lambda a: jnp.asarray(a.astype(np.float32))
    return (f(decay),
            f(np.broadcast_to(xi[:, :, None], (RET_HEADS, c, RET_V_DIM))),
            f(np.broadcast_to(zeta[:, :, None], (RET_HEADS, c, RET_QK_DIM))),
            f(np.broadcast_to(cd[:, None, None], (RET_HEADS, 1, RET_V_DIM))))


def _retention(rot3, plain3, gn_g, gn_b):
    b, s, _ = rot3.shape
    decay, xi, zeta, cd = _retention_tables()
    v_blk0 = 0
    gate_blk0 = RET_V_W // RET_V_DIM
    return pl.pallas_call(
        functools.partial(_retention_kernel, n_chunks=s // RET_CHUNK),
        grid=(b, RET_HEADS),
        in_specs=[pl.BlockSpec((None, s, RET_QK_DIM), lambda bi, h: (bi, 0, h)),
                  pl.BlockSpec((None, s, RET_QK_DIM), lambda bi, h: (bi, 0, RET_HEADS + h)),
                  pl.BlockSpec((None, s, RET_V_DIM), lambda bi, h: (bi, 0, v_blk0 + h)),
                  pl.BlockSpec((None, s, RET_V_DIM), lambda bi, h: (bi, 0, gate_blk0 + h)),
                  pl.BlockSpec((None, RET_CHUNK, RET_CHUNK), lambda bi, h: (h, 0, 0)),
                  pl.BlockSpec((None, RET_CHUNK, RET_V_DIM), lambda bi, h: (h, 0, 0)),
                  pl.BlockSpec((None, RET_CHUNK, RET_QK_DIM), lambda bi, h: (h, 0, 0)),
                  pl.BlockSpec((None, 1, RET_V_DIM), lambda bi, h: (h, 0, 0)),
                  pl.BlockSpec((1, RET_V_DIM), lambda bi, h: (0, h)),
                  pl.BlockSpec((1, RET_V_DIM), lambda bi, h: (0, h))],
        out_specs=pl.BlockSpec((None, s, RET_V_DIM), lambda bi, h: (bi, 0, h)),
        out_shape=jax.ShapeDtypeStruct((b, s, RET_V_W), BF16),
        compiler_params=_params(2),
        name="retention",
    )(rot3, rot3, plain3, plain3, decay, xi, zeta, cd, gn_g.reshape(1, -1), gn_b.reshape(1, -1))


def _merge_kernel(oa_ref, or_ref, wa_ref, wr_ref, ga_ref, gb_ref, o_ref, wa_bf, wr_bf):
    @pl.when(pl.program_id(1) == 0)
    def _():
        wa_bf[...] = wa_ref[...].astype(BF16)
        wr_bf[...] = wr_ref[...].astype(BF16)

    ya = jnp.dot(oa_ref[...], wa_bf[...], preferred_element_type=F32)
    yr = jnp.dot(or_ref[...], wr_bf[...], preferred_element_type=F32)
    ga = ga_ref[...].astype(F32)
    gb = gb_ref[...].astype(F32)
    sa = 1.0 / (1.0 + jnp.exp(-ga))
    sb = 1.0 / (1.0 + jnp.exp(-gb))
    o_ref[...] = (sa * ya + sb * yr).astype(o_ref.dtype)


def _merge(o_attn, o_ret, w_ua3, w_ur3, layer, plain, tm=1024, tn=512):
    m = o_attn.shape[0]
    ga_blk0 = (2 * RET_V_W) // tn
    gb_blk0 = (2 * RET_V_W + D_MODEL) // tn
    return pl.pallas_call(
        _merge_kernel,
        grid=(D_MODEL // tn, m // tm),
        in_specs=[pl.BlockSpec((tm, ATTN_W), lambda j, i: (i, 0)),
                  pl.BlockSpec((tm, RET_V_W), lambda j, i: (i, 0)),
                  pl.BlockSpec((None, ATTN_W, tn), lambda j, i: (layer, 0, j)),
                  pl.BlockSpec((None, RET_V_W, tn), lambda j, i: (layer, 0, j)),
                  pl.BlockSpec((tm, tn), lambda j, i: (i, ga_blk0 + j)),
                  pl.BlockSpec((tm, tn), lambda j, i: (i, gb_blk0 + j))],
        out_specs=pl.BlockSpec((tm, tn), lambda j, i: (i, j)),
        out_shape=jax.ShapeDtypeStruct((m, D_MODEL), BF16),
        scratch_shapes=[pltpu.VMEM((ATTN_W, tn), BF16), pltpu.VMEM((RET_V_W, tn), BF16)],
        compiler_params=_params(2),
        name="gated_merge",
    )(o_attn, o_ret, w_ua3, w_ur3, plain, plain)


def _rotary_tables(s):
    pos = jnp.arange(s, dtype=F32)
    inv_freq = ROPE_BASE ** (-jnp.arange(0, RET_QK_DIM, 2, dtype=F32) / RET_QK_DIM)
    ang = pos[:, None] * inv_freq[None, :]
    cos, sin = jnp.cos(ang), jnp.sin(ang)
    return jnp.concatenate([cos, cos], axis=1), jnp.concatenate([-sin, sin], axis=1)


def kernel(x, ln1_g, w_in, q_norm_g, k_norm_g, ret_gn_g, ret_gn_b, w_up_attn, w_up_ret, w_out, ln2_g,
           w_ff1, w_ff2):
    b, s, d = x.shape
    m = b * s
    depth = w_in.shape[0]
    topk = min(TOPK_MAX, s // 4)
    assert s % Q_SUPER == 0 and topk <= Q_SUPER
    cos2, sin2 = _rotary_tables(s)
    tm = 1024
    s_tiles = s // tm

    c_kv = ATTN_W
    c_iq = c_kv + 2 * KV_W
    c_ikw = c_iq + IDX_Q_W
    src_rq = c_ikw + IDX_DIM + IDX_HEADS
    w_tail = w_in[:, :, src_rq:].astype(BF16)
    c_rot = 0
    c_plain = 2 * RET_QK_W

    xf = x.reshape(m, d)
    for l in range(depth):
        h = _rmsnorm(xf, ln1_g[l])
        aq = _matmul(h, w_in, l, out_dtype=BF16, epilogue=_epi_headnorm, tm=tm, tn=1024, col0=0, n=ATTN_W,
                     name="proj_q",
                     extras=[(q_norm_g[l].reshape(1, HEAD_DIM), (1, HEAD_DIM), lambda i, j: (0, 0))])
        kv = _matmul(h, w_in, l, out_dtype=BF16, epilogue=_epi_kv, tm=tm, tn=KV_W, col0=c_kv, n=2 * KV_W,
                     name="proj_kv",
                     extras=[(k_norm_g[l].reshape(1, HEAD_DIM), (1, HEAD_DIM), lambda i, j: (0, 0))])
        iq = _matmul(h, w_in, l, out_dtype=F32, epilogue=_epi_plain, tm=tm, tn=IDX_Q_W, col0=c_iq, n=IDX_Q_W,
                     name="proj_iq")
        ikw = _matmul(h, w_in, l, out_dtype=F32, epilogue=_epi_plain, tm=tm, tn=LANES, col0=c_ikw, n=LANES,
                      name="proj_ikw")
        rot = _matmul(h, w_tail, l, out_dtype=BF16, epilogue=_epi_rotary, tm=tm, tn=RET_QK_W, col0=c_rot,
                      n=2 * RET_QK_W, name="proj_rot",
                      extras=[(cos2, (tm, RET_QK_DIM), lambda i, j: (i % s_tiles, 0)),
                              (sin2, (tm, RET_QK_DIM), lambda i, j: (i % s_tiles, 0))])
        plain = _matmul(h, w_tail, l, out_dtype=BF16, epilogue=_epi_plain, tm=tm, tn=1024, col0=c_plain,
                        n=PLAIN_W, name="proj_plain")

        aq3 = aq.reshape(b, s, ATTN_W)
        kv3 = kv.reshape(b, s, 2 * KV_W)
        iq3 = iq.reshape(b, s, IDX_Q_W)
        ikw3 = ikw.reshape(b, s, LANES)
        plain3 = plain.reshape(b, s, PLAIN_W)
        rot3 = rot.reshape(b, s, 2 * RET_QK_W)

        o_parts = []
        for j in range(s // Q_SUPER):
            bias = _topk_mask(iq3, ikw3, j=j, topk=topk)
            o_parts.append(_attention(aq3, kv3, bias, j=j))
        o_attn = jnp.concatenate(o_parts, axis=1).reshape(m, ATTN_W)

        o_ret = _retention(rot3, plain3, ret_gn_g[l], ret_gn_b[l]).reshape(m, RET_V_W)

        merged = _merge(o_attn, o_ret, w_up_attn, w_up_ret, l, plain)
        xf = _matmul(merged, w_out, l, out_dtype=F32, epilogue=_epi_residual, tm=tm, tn=1024,
                     name="proj_out", extras=[(xf, (tm, 1024), lambda i, j: (i, j))])

        h2 = _rmsnorm(xf, ln2_g[l])
        f = _matmul(h2, w_ff1, l, out_dtype=BF16, epilogue=_epi_relu2, tm=tm, tn=1024, name="ff1")
        xf = _matmul(f, w_ff2, l, out_dtype=F32, epilogue=_epi_residual, tm=512, tn=1024, tk=2048,
                     name="ff2", extras=[(xf, (512, 1024), lambda i, j: (i, j))])
    return xf.reshape(b, s, d)
```

```python
import functools
import math

import jax
import jax.numpy as jnp
import numpy as np
from jax import lax
from jax.experimental import pallas as pl
from jax.experimental.pallas import tpu as pltpu

D_MODEL = 2048
HEAD_DIM = 128
ATTN_Q_HEADS = 16
ATTN_KV_HEADS = 4
ATTN_GROUP = ATTN_Q_HEADS // ATTN_KV_HEADS
IDX_HEADS = 16
IDX_DIM = 64
TOPK_MAX = 256
RET_HEADS = 8
RET_QK_DIM = 128
RET_V_DIM = 256
RET_CHUNK = 128
ROPE_BASE = 10000.0
D_FF = 4 * D_MODEL
EPS = 1e-6

ATTN_W = ATTN_Q_HEADS * HEAD_DIM
KV_W = ATTN_KV_HEADS * HEAD_DIM
IDX_Q_W = IDX_HEADS * IDX_DIM
RET_QK_W = RET_HEADS * RET_QK_DIM
RET_V_W = RET_HEADS * RET_V_DIM

LANES = 128
PLAIN_W = 2 * RET_V_W + 2 * D_MODEL
COUNT_ROWS = 64
BISECT_MAX_STEPS = 2200
Q_SUPER = 512
NEG_BIG = -1e30
INT_MIN = -(2 ** 31)
INT_MAX = 2 ** 31 - 1
VMEM_LIMIT = 56 * 1024 * 1024

F32 = jnp.float32
BF16 = jnp.bfloat16


def _params(n_axes):
    return pltpu.CompilerParams(dimension_semantics=("arbitrary",) * n_axes,
                                vmem_limit_bytes=VMEM_LIMIT)


def _rmsnorm_kernel(x_ref, g_ref, o_ref):
    x = x_ref[...]
    r = lax.rsqrt(jnp.mean(x * x, axis=-1, keepdims=True) + EPS)
    o_ref[...] = ((x * r) * g_ref[...]).astype(o_ref.dtype)


def _rmsnorm(x, g, tm=512):
    m, d = x.shape
    return pl.pallas_call(
        _rmsnorm_kernel,
        grid=(m // tm,),
        in_specs=[pl.BlockSpec((tm, d), lambda i: (i, 0)),
                  pl.BlockSpec((1, d), lambda i: (0, 0))],
        out_specs=pl.BlockSpec((tm, d), lambda i: (i, 0)),
        out_shape=jax.ShapeDtypeStruct((m, d), BF16),
        compiler_params=_params(1),
        name="rmsnorm",
    )(x, g.reshape(1, d))


def _mm_kernel(*refs, nk, n_extra, epilogue, cache_w, w_is_nk):
    a_ref, w_ref = refs[0], refs[1]
    extra = refs[2:2 + n_extra]
    o_ref = refs[2 + n_extra]
    scratch = refs[3 + n_extra:]
    i = pl.program_id(1)
    k = pl.program_id(2)
    if cache_w:
        wbf_ref = scratch[0]

        @pl.when(i == 0)
        def _():
            wbf_ref[k] = (w_ref[0].T if w_is_nk else w_ref[...]).astype(BF16)

        w = wbf_ref[k]
    else:
        w = w_ref[...]
    prod = jnp.dot(a_ref[...], w, preferred_element_type=F32)
    if nk == 1:
        o_ref[...] = epilogue(prod, *extra).astype(o_ref.dtype)
        return
    acc_ref = scratch[-1]

    @pl.when(k == 0)
    def _():
        acc_ref[...] = prod

    @pl.when(k > 0)
    def _():
        acc_ref[...] += prod

    @pl.when(k == nk - 1)
    def _():
        o_ref[...] = epilogue(acc_ref[...], *extra).astype(o_ref.dtype)


def _matmul(a, w3, layer, *, out_dtype, epilogue, extras=(), tm=1024, tn=1024, tk=None, col0=0, n=None,
            w_is_nk=False, name="matmul"):
    m, kdim = a.shape
    n_total = w3.shape[1] if w_is_nk else w3.shape[2]
    n = n_total - col0 if n is None else n
    tk = kdim if tk is None else tk
    nk = kdim // tk
    tn = min(tn, n)
    assert m % tm == 0 and n % tn == 0 and kdim % tk == 0
    cache_w = w3.dtype != BF16
    if w_is_nk:
        assert cache_w and nk == 1 and col0 % 8 == 0
        w_spec = pl.BlockSpec((pl.Element(1), pl.Element(tn), pl.Element(tk)),
                              lambda j, i, k: (layer, pl.multiple_of(col0 + j * tn, 8), 0))
    else:
        assert col0 % tn == 0
        jb0 = col0 // tn
        if cache_w:
            w_map = lambda j, i, k: (layer, jnp.where(i == 0, k, nk - 1), jb0 + j)
        else:
            w_map = lambda j, i, k: (layer, k, jb0 + j)
        w_spec = pl.BlockSpec((None, tk, tn), w_map)
    in_specs = [pl.BlockSpec((tm, tk), lambda j, i, k: (i, k)), w_spec]
    for _, bshape, imap in extras:
        in_specs.append(pl.BlockSpec(bshape, functools.partial(lambda j, i, k, f: f(i, j), f=imap)))
    scratch = []
    if cache_w:
        scratch.append(pltpu.VMEM((nk, tk, tn), BF16))
    if nk > 1:
        scratch.append(pltpu.VMEM((tm, tn), F32))
    return pl.pallas_call(
        functools.partial(_mm_kernel, nk=nk, n_extra=len(extras), epilogue=epilogue, cache_w=cache_w,
                          w_is_nk=w_is_nk),
        grid=(n // tn, m // tm, nk),
        in_specs=in_specs,
        out_specs=pl.BlockSpec((tm, tn), lambda j, i, k: (i, j)),
        out_shape=jax.ShapeDtypeStruct((m, n), out_dtype),
        scratch_shapes=scratch,
        compiler_params=_params(3),
        name=name,
    )(a, w3, *[e[0] for e in extras])


def _epi_plain(acc):
    return acc


def _epi_headnorm(acc, g_ref):
    g = g_ref[...]
    outs = []
    for h in range(acc.shape[1] // HEAD_DIM):
        a = acc[:, h * HEAD_DIM:(h + 1) * HEAD_DIM]
        r = lax.rsqrt(jnp.mean(a * a, axis=-1, keepdims=True) + EPS)
        outs.append((a * r) * g)
    return jnp.concatenate(outs, axis=1)


def _epi_kv(acc, g_ref):
    return jnp.where(pl.program_id(0) == 0, _epi_headnorm(acc, g_ref), acc)


def _epi_rotary(acc, cos_ref, sin_ref):
    c = cos_ref[...]
    s = sin_ref[...]
    scale = jnp.where(pl.program_id(0) == 0, 1.0, RET_QK_DIM ** -0.5).astype(F32)
    outs = []
    for h in range(acc.shape[1] // RET_QK_DIM):
        a = acc[:, h * RET_QK_DIM:(h + 1) * RET_QK_DIM]
        swapped = pltpu.roll(a, RET_QK_DIM // 2, axis=1)
        outs.append((a * c + swapped * s) * scale)
    return jnp.concatenate(outs, axis=1)


def _epi_residual(acc, x_ref):
    return x_ref[...] + acc


def _epi_relu2(acc):
    r = jnp.maximum(acc, 0.0)
    return r * r


def _topk_mask_kernel(iq_ref, qpack_ref, kpack_ref, bias_ref, score_ref, sel_ref, *, q0, kend, tq, topk):
    rows = COUNT_ROWS
    qbase = q0 + pl.program_id(1) * tq
    neg_inf = float("-inf")

    ik = kpack_ref[:, :IDX_DIM].astype(BF16)
    w_t = qpack_ref[...].T * (IDX_HEADS ** -0.5)
    score = None
    for h in range(IDX_HEADS):
        iqh = iq_ref[:, h * IDX_DIM:(h + 1) * IDX_DIM].astype(BF16)
        logits = lax.dot_general(ik, iqh, (((1,), (1,)), ((), ())), preferred_element_type=F32)
        term = jnp.maximum(logits, 0.0) * w_t[IDX_DIM + h:IDX_DIM + h + 1, :]
        score = term if score is None else score + term

    def causal_mask():
        kpos = lax.broadcasted_iota(jnp.int32, (kend, tq), 0)
        qcol = lax.broadcasted_iota(jnp.int32, (kend, tq), 1) + qbase
        return kpos <= qcol

    causal = causal_mask()
    score = jnp.where(score == 0.0, 0.0, score)
    lo_fill = jnp.where(causal, score, neg_inf)
    score_ref[...] = lo_fill
    smax = jnp.max(jnp.max(lo_fill.reshape(kend // rows, rows, tq), axis=0), axis=0, keepdims=True)
    hi_fill = jnp.where(causal, score, float("inf"))
    smin = jnp.min(jnp.min(hi_fill.reshape(kend // rows, rows, tq), axis=0), axis=0, keepdims=True)
    n_causal = jnp.minimum(lax.broadcasted_iota(jnp.int32, (1, tq), 1) + (qbase + 1), kend)

    def count_ge(thr):
        thr_b = jnp.broadcast_to(thr, (rows, tq))
        part = jnp.zeros((rows, tq), F32)
        for i in range(kend // rows):
            part = part + jnp.where(score_ref[i * rows:(i + 1) * rows, :] >= thr_b, 1.0, 0.0)
        return jnp.sum(part, axis=0, keepdims=True)

    kf = float(topk)

    def cond(c):
        return (c[1] > 0) & (c[0] < BISECT_MAX_STEPS)

    def body(c):
        it, _, lo, hi, tau, act = c
        first = (jnp.zeros((1, tq), jnp.int32) + it) == 0
        mid = jnp.where(first, hi, 0.5 * lo + 0.5 * hi)
        stuck = jnp.logical_not(first) & ((mid <= lo) | (mid >= hi))
        cnt = count_ge(mid)
        ge = cnt >= kf
        done = stuck | (cnt == kf) | (first & ge)
        tau = jnp.where((act > 0) & done, jnp.where(stuck, lo, mid), tau)
        lo = jnp.where(ge, mid, lo)
        hi = jnp.where(ge, hi, mid)
        act = jnp.where(done, 0, act)
        return it + 1, jnp.sum(act), lo, hi, tau, act

    lo0 = jnp.where(n_causal >= topk, smin, neg_inf)
    init = (jnp.int32(0), jnp.int32(tq), lo0, smax, jnp.full((1, tq), neg_inf, F32), jnp.ones((1, tq), jnp.int32))
    tau = lax.while_loop(cond, body, init)[4]

    sel_ref[...] = jnp.where(score_ref[...] >= tau, 0.0, NEG_BIG)
    tied = jnp.where((count_ge(tau) > kf) & (tau > neg_inf), 1, 0)

    @pl.when(jnp.max(tied) > 0)
    def _():
        tc = 512
        row = lax.broadcasted_iota(jnp.int32, (tc, tc), 0)
        col = lax.broadcasted_iota(jnp.int32, (tc, tc), 1)
        lower = jnp.where(col <= row, 1.0, 0.0).astype(BF16)
        n_gt = jnp.sum(jnp.where(score_ref[...] > tau, 1.0, 0.0), axis=0, keepdims=True)
        need = kf - n_gt
        carry = jnp.zeros((1, tq), F32)
        for c in range(kend // tc):
            sc = score_ref[c * tc:(c + 1) * tc, :]
            eq = jnp.where(sc == tau, 1.0, 0.0)
            prefix = jnp.dot(lower, eq.astype(BF16), preferred_element_type=F32) + carry
            carry = carry + jnp.sum(eq, axis=0, keepdims=True)
            keep = (sc > tau) | ((sc == tau) & (prefix <= need))
            sel_ref[c * tc:(c + 1) * tc, :] = jnp.where(keep, 0.0, NEG_BIG)

    sel = jnp.where(causal_mask(), sel_ref[...], NEG_BIG)
    bias_ref[...] = sel.T.astype(bias_ref.dtype)


def _topk_mask(iq3, ikw3, *, j, topk, tq=256):
    b = iq3.shape[0]
    kend = Q_SUPER * (j + 1)
    q0 = Q_SUPER * j
    nq = Q_SUPER // tq
    return pl.pallas_call(
        functools.partial(_topk_mask_kernel, q0=q0, kend=kend, tq=tq, topk=topk),
        grid=(b, nq),
        in_specs=[pl.BlockSpec((None, tq, IDX_Q_W), lambda bi, qi: (bi, q0 // tq + qi, 0)),
                  pl.BlockSpec((None, tq, LANES), lambda bi, qi: (bi, q0 // tq + qi, 0)),
                  pl.BlockSpec((None, kend, LANES), lambda bi, qi: (bi, 0, 0))],
        out_specs=pl.BlockSpec((None, tq, kend), lambda bi, qi: (bi, qi, 0)),
        out_shape=jax.ShapeDtypeStruct((b, Q_SUPER, kend), BF16),
        scratch_shapes=[pltpu.VMEM((kend, tq), F32), pltpu.VMEM((kend, tq), F32)],
        compiler_params=_params(2),
        name=f"topk_mask_{j}",
    )(iq3, ikw3, ikw3)


def _attn_kernel(q_ref, k_ref, v_ref, bias_ref, o_ref, *, tq):
    scale = HEAD_DIM ** -0.5
    bias = bias_ref[...].astype(F32)
    for g in range(ATTN_KV_HEADS):
        kg = k_ref[:, g * HEAD_DIM:(g + 1) * HEAD_DIM]
        vg = v_ref[:, g * HEAD_DIM:(g + 1) * HEAD_DIM]
        h0 = g * ATTN_GROUP
        qg = jnp.concatenate([q_ref[:, (h0 + r) * HEAD_DIM:(h0 + r + 1) * HEAD_DIM]
                              for r in range(ATTN_GROUP)], axis=0)
        s = lax.dot_general(qg, kg, (((1,), (1,)), ((), ())), preferred_element_type=F32)
        s = s * scale + jnp.concatenate([bias] * ATTN_GROUP, axis=0)
        m = jnp.max(s, axis=-1, keepdims=True)
        p = jnp.exp(s - m)
        l = jnp.sum(p, axis=-1, keepdims=True)
        o = jnp.dot(p.astype(BF16), vg, preferred_element_type=F32) / l
        for r in range(ATTN_GROUP):
            o_ref[:, (h0 + r) * HEAD_DIM:(h0 + r + 1) * HEAD_DIM] = o[r * tq:(r + 1) * tq].astype(o_ref.dtype)


def _attention(q3, kv3, bias, *, j, tq=128):
    b = q3.shape[0]
    kend = Q_SUPER * (j + 1)
    q0 = Q_SUPER * j
    nq = Q_SUPER // tq
    return pl.pallas_call(
        functools.partial(_attn_kernel, tq=tq),
        grid=(b, nq),
        in_specs=[pl.BlockSpec((None, tq, ATTN_W), lambda bi, qi: (bi, q0 // tq + qi, 0)),
                  pl.BlockSpec((None, kend, KV_W), lambda bi, qi: (bi, 0, 0)),
                  pl.BlockSpec((None, kend, KV_W), lambda bi, qi: (bi, 0, 1)),
                  pl.BlockSpec((None, tq, kend), lambda bi, qi: (bi, qi, 0))],
        out_specs=pl.BlockSpec((None, tq, ATTN_W), lambda bi, qi: (bi, qi, 0)),
        out_shape=jax.ShapeDtypeStruct((b, Q_SUPER, ATTN_W), BF16),
        compiler_params=_params(2),
        name=f"attention_{j}",
    )(q3, kv3, kv3, bias)


def _retention_kernel(q_ref, k_ref, v_ref, gate_ref, decay_ref, xi_ref, zeta_ref, cd_ref, g_ref, b_ref,
                      o_ref, *, n_chunks):
    c = RET_CHUNK
    decay = decay_ref[...]
    xi = xi_ref[...]
    zeta = zeta_ref[...]
    cd = cd_ref[...]
    gn_g = g_ref[...]
    gn_b = b_ref[...]
    state = jnp.zeros((RET_QK_DIM, RET_V_DIM), F32)
    for n in range(n_chunks):
        sl = slice(n * c, (n + 1) * c)
        q = q_ref[sl, :]
        k = k_ref[sl, :]
        v = v_ref[sl, :]
        qk = lax.dot_general(q, k, (((1,), (1,)), ((), ())), preferred_element_type=F32) * decay
        inner = jnp.dot(qk.astype(BF16), v, preferred_element_type=F32)
        cross = jnp.dot(q, state.astype(BF16), preferred_element_type=F32) * xi
        kz_t = (k.astype(F32) * zeta).T.astype(BF16)
        upd = jnp.dot(kz_t, v, preferred_element_type=F32)
        state = upd + cd * state
        y = inner + cross
        mu = jnp.mean(y, axis=-1, keepdims=True)
        d = y - mu
        var = jnp.mean(d * d, axis=-1, keepdims=True)
        yn = d * lax.rsqrt(var + EPS)
        z = yn * gn_g + gn_b
        gate = gate_ref[sl, :].astype(F32)
        silu = gate * (1.0 / (1.0 + jnp.exp(-gate)))
        o_ref[sl, :] = (silu * z).astype(o_ref.dtype)


def _retention_tables():
    c = RET_CHUNK
    lg = np.log(1.0 - np.exp2(-5.0 - np.arange(RET_HEADS, dtype=np.float32))).astype(np.float32)
    pos = np.arange(c, dtype=np.float32)
    diff = pos[:, None] - pos[None, :]
    decay = np.where(diff[None] >= 0, np.exp(lg[:, None, None] * np.maximum(diff, 0.0)[None]), 0.0)
    xi = np.exp(lg[:, None] * (pos[None, :] + 1.0))
    zeta = np.exp(lg[:, None] * (c - 1.0 - pos[None, :]))
    cd = np.exp(lg * c)
    f = lambda a: jnp.asarray(a.astype(np.float32))
    return (f(decay),
            f(np.broadcast_to(xi[:, :, None], (RET_HEADS, c, RET_V_DIM))),
            f(np.broadcast_to(zeta[:, :, None], (RET_HEADS, c, RET_QK_DIM))),
            f(np.broadcast_to(cd[:, None, None], (RET_HEADS, 1, RET_V_DIM))))


def _retention(rot3, plain3, gn_g, gn_b):
    b, s, _ = rot3.shape
    decay, xi, zeta, cd = _retention_tables()
    v_blk0 = 0
    gate_blk0 = RET_V_W // RET_V_DIM
    return pl.pallas_call(
        functools.partial(_retention_kernel, n_chunks=s // RET_CHUNK),
        grid=(b, RET_HEADS),
        in_specs=[pl.BlockSpec((None, s, RET_QK_DIM), lambda bi, h: (bi, 0, h)),
                  pl.BlockSpec((None, s, RET_QK_DIM), lambda bi, h: (bi, 0, RET_HEADS + h)),
                  pl.BlockSpec((None, s, RET_V_DIM), lambda bi, h: (bi, 0, v_blk0 + h)),
                  pl.BlockSpec((None, s, RET_V_DIM), lambda bi, h: (bi, 0, gate_blk0 + h)),
                  pl.BlockSpec((None, RET_CHUNK, RET_CHUNK), lambda bi, h: (h, 0, 0)),
                  pl.BlockSpec((None, RET_CHUNK, RET_V_DIM), lambda bi, h: (h, 0, 0)),
                  pl.BlockSpec((None, RET_CHUNK, RET_QK_DIM), lambda bi, h: (h, 0, 0)),
                  pl.BlockSpec((None, 1, RET_V_DIM), lambda bi, h: (h, 0, 0)),
                  pl.BlockSpec((1, RET_V_DIM), lambda bi, h: (0, h)),
                  pl.BlockSpec((1, RET_V_DIM), lambda bi, h: (0, h))],
        out_specs=pl.BlockSpec((None, s, RET_V_DIM), lambda bi, h: (bi, 0, h)),
        out_shape=jax.ShapeDtypeStruct((b, s, RET_V_W), BF16),
        compiler_params=_params(2),
        name="retention",
    )(rot3, rot3, plain3, plain3, decay, xi, zeta, cd, gn_g.reshape(1, -1), gn_b.reshape(1, -1))


def _merge_kernel(oa_ref, or_ref, wa_ref, wr_ref, ga_ref, gb_ref, o_ref, wa_bf, wr_bf):
    @pl.when(pl.program_id(1) == 0)
    def _():
        wa_bf[...] = wa_ref[...].astype(BF16)
        wr_bf[...] = wr_ref[...].astype(BF16)

    ya = jnp.dot(oa_ref[...], wa_bf[...], preferred_element_type=F32)
    yr = jnp.dot(or_ref[...], wr_bf[...], preferred_element_type=F32)
    ga = ga_ref[...].astype(F32)
    gb = gb_ref[...].astype(F32)
    sa = 1.0 / (1.0 + jnp.exp(-ga))
    sb = 1.0 / (1.0 + jnp.exp(-gb))
    o_ref[...] = (sa * ya + sb * yr).astype(o_ref.dtype)


def _merge(o_attn, o_ret, w_ua3, w_ur3, layer, plain, tm=1024, tn=512):
    m = o_attn.shape[0]
    ga_blk0 = (2 * RET_V_W) // tn
    gb_blk0 = (2 * RET_V_W + D_MODEL) // tn
    return pl.pallas_call(
        _merge_kernel,
        grid=(D_MODEL // tn, m // tm),
        in_specs=[pl.BlockSpec((tm, ATTN_W), lambda j, i: (i, 0)),
                  pl.BlockSpec((tm, RET_V_W), lambda j, i: (i, 0)),
                  pl.BlockSpec((None, ATTN_W, tn), lambda j, i: (layer, 0, j)),
                  pl.BlockSpec((None, RET_V_W, tn), lambda j, i: (layer, 0, j)),
                  pl.BlockSpec((tm, tn), lambda j, i: (i, ga_blk0 + j)),
                  pl.BlockSpec((tm, tn), lambda j, i: (i, gb_blk0 + j))],
        out_specs=pl.BlockSpec((tm, tn), lambda j, i: (i, j)),
        out_shape=jax.ShapeDtypeStruct((m, D_MODEL), BF16),
        scratch_shapes=[pltpu.VMEM((ATTN_W, tn), BF16), pltpu.VMEM((RET_V_W, tn), BF16)],
        compiler_params=_params(2),
        name="gated_merge",
    )(o_attn, o_ret, w_ua3, w_ur3, plain, plain)


def _rotary_tables(s):
    pos = jnp.arange(s, dtype=F32)
    inv_freq = ROPE_BASE ** (-jnp.arange(0, RET_QK_DIM, 2, dtype=F32) / RET_QK_DIM)
    ang = pos[:, None] * inv_freq[None, :]
    cos, sin = jnp.cos(ang), jnp.sin(ang)
    return jnp.concatenate([cos, cos], axis=1), jnp.concatenate([-sin, sin], axis=1)


def kernel(x, ln1_g, w_in, q_norm_g, k_norm_g, ret_gn_g, ret_gn_b, w_up_attn, w_up_ret, w_out, ln2_g,
           w_ff1, w_ff2):
    b, s, d = x.shape
    m = b * s
    depth = w_in.shape[0]
    topk = min(TOPK_MAX, s // 4)
    assert s % Q_SUPER == 0 and topk <= Q_SUPER
    cos2, sin2 = _rotary_tables(s)
    tm = 1024
    s_tiles = s // tm

    w_in_t = jnp.swapaxes(w_in, 1, 2)
    c_kv = ATTN_W
    c_iq = c_kv + 2 * KV_W
    c_ikw = c_iq + IDX_Q_W
    c_rot = c_ikw + IDX_DIM + IDX_HEADS
    c_plain = c_rot + 2 * RET_QK_W
    in_proj = functools.partial(_matmul, w3=w_in_t, w_is_nk=True, tm=tm)

    xf = x.reshape(m, d)
    for l in range(depth):
        h = _rmsnorm(xf, ln1_g[l])
        aq = in_proj(h, layer=l, out_dtype=BF16, epilogue=_epi_headnorm, tn=1024, col0=0, n=ATTN_W,
                     name="proj_q",
                     extras=[(q_norm_g[l].reshape(1, HEAD_DIM), (1, HEAD_DIM), lambda i, j: (0, 0))])
        kv = in_proj(h, layer=l, out_dtype=BF16, epilogue=_epi_kv, tn=KV_W, col0=c_kv, n=2 * KV_W,
                     name="proj_kv",
                     extras=[(k_norm_g[l].reshape(1, HEAD_DIM), (1, HEAD_DIM), lambda i, j: (0, 0))])
        iq = in_proj(h, layer=l, out_dtype=F32, epilogue=_epi_plain, tn=IDX_Q_W, col0=c_iq, n=IDX_Q_W,
                     name="proj_iq")
        ikw = in_proj(h, layer=l, out_dtype=F32, epilogue=_epi_plain, tn=LANES, col0=c_ikw, n=LANES,
                      name="proj_ikw")
        rot = in_proj(h, layer=l, out_dtype=BF16, epilogue=_epi_rotary, tn=RET_QK_W, col0=c_rot,
                      n=2 * RET_QK_W, name="proj_rot",
                      extras=[(cos2, (tm, RET_QK_DIM), lambda i, j: (i % s_tiles, 0)),
                              (sin2, (tm, RET_QK_DIM), lambda i, j: (i % s_tiles, 0))])
        plain = in_proj(h, layer=l, out_dtype=BF16, epilogue=_epi_plain, tn=1024, col0=c_plain,
                        n=PLAIN_W, name="proj_plain")

        aq3 = aq.reshape(b, s, ATTN_W)
        kv3 = kv.reshape(b, s, 2 * KV_W)
        iq3 = iq.reshape(b, s, IDX_Q_W)
        ikw3 = ikw.reshape(b, s, LANES)
        plain3 = plain.reshape(b, s, PLAIN_W)
        rot3 = rot.reshape(b, s, 2 * RET_QK_W)

        o_parts = []
        for j in range(s // Q_SUPER):
            bias = _topk_mask(iq3, ikw3, j=j, topk=topk)
            o_parts.append(_attention(aq3, kv3, bias, j=j))
        o_attn = jnp.concatenate(o_parts, axis=1).reshape(m, ATTN_W)

        o_ret = _retention(rot3, plain3, ret_gn_g[l], ret_gn_b[l]).reshape(m, RET_V_W)

        merged = _merge(o_attn, o_ret, w_up_attn, w_up_ret, l, plain)
        xf = _matmul(merged, w_out, l, out_dtype=F32, epilogue=_epi_residual, tm=tm, tn=1024,
                     name="proj_out", extras=[(xf, (tm, 1024), lambda i, j: (i, j))])

        h2 = _rmsnorm(xf, ln2_g[l])
        f = _matmul(h2, w_ff1, l, out_dtype=BF16, epilogue=_epi_relu2, tm=tm, tn=1024, name="ff1")
        xf = _matmul(f, w_ff2, l, out_dtype=F32, epilogue=_epi_residual, tm=512, tn=1024, tk=2048,
                     name="ff2", extras=[(xf, (512, 1024), lambda i, j: (i, j))])
    return xf.reshape(b, s, d)
```

```python
import functools
import math

import jax
import jax.numpy as jnp
import numpy as np
from jax import lax
from jax.experimental import pallas as pl
from jax.experimental.pallas import tpu as pltpu

D_MODEL = 2048
HEAD_DIM = 128
ATTN_Q_HEADS = 16
ATTN_KV_HEADS = 4
ATTN_GROUP = ATTN_Q_HEADS // ATTN_KV_HEADS
IDX_HEADS = 16
IDX_DIM = 64
TOPK_MAX = 256
RET_HEADS = 8
RET_QK_DIM = 128
RET_V_DIM = 256
RET_CHUNK = 128
ROPE_BASE = 10000.0
D_FF = 4 * D_MODEL
EPS = 1e-6

ATTN_W = ATTN_Q_HEADS * HEAD_DIM
KV_W = ATTN_KV_HEADS * HEAD_DIM
IDX_Q_W = IDX_HEADS * IDX_DIM
RET_QK_W = RET_HEADS * RET_QK_DIM
RET_V_W = RET_HEADS * RET_V_DIM

LANES = 128
PLAIN_W = 2 * RET_V_W + 2 * D_MODEL
COUNT_ROWS = 64
BISECT_MAX_STEPS = 2200
Q_SUPER = 512
NEG_BIG = -1e30
INT_MIN = -(2 ** 31)
INT_MAX = 2 ** 31 - 1
VMEM_LIMIT = 56 * 1024 * 1024

F32 = jnp.float32
BF16 = jnp.bfloat16


def _params(n_axes):
    return pltpu.CompilerParams(dimension_semantics=("arbitrary",) * n_axes,
                                vmem_limit_bytes=VMEM_LIMIT)


def _rmsnorm_kernel(x_ref, g_ref, o_ref):
    x = x_ref[...]
    r = lax.rsqrt(jnp.mean(x * x, axis=-1, keepdims=True) + EPS)
    o_ref[...] = ((x * r) * g_ref[...]).astype(o_ref.dtype)


def _rmsnorm(x, g, tm=512):
    m, d = x.shape
    return pl.pallas_call(
        _rmsnorm_kernel,
        grid=(m // tm,),
        in_specs=[pl.BlockSpec((tm, d), lambda i: (i, 0)),
                  pl.BlockSpec((1, d), lambda i: (0, 0))],
        out_specs=pl.BlockSpec((tm, d), lambda i: (i, 0)),
        out_shape=jax.ShapeDtypeStruct((m, d), BF16),
        compiler_params=_params(1),
        name="rmsnorm",
    )(x, g.reshape(1, d))


def _mm_kernel(*refs, nk, n_extra, epilogue, cache_w, w_is_nk):
    a_ref, w_ref = refs[0], refs[1]
    extra = refs[2:2 + n_extra]
    o_ref = refs[2 + n_extra]
    scratch = refs[3 + n_extra:]
    i = pl.program_id(1)
    k = pl.program_id(2)
    if cache_w:
        wbf_ref = scratch[0]

        @pl.when(i == 0)
        def _():
            wbf_ref[k] = (w_ref[0].T if w_is_nk else w_ref[...]).astype(BF16)

        w = wbf_ref[k]
    else:
        w = w_ref[...]
    prod = jnp.dot(a_ref[...], w, preferred_element_type=F32)
    if nk == 1:
        o_ref[...] = epilogue(prod, *extra).astype(o_ref.dtype)
        return
    acc_ref = scratch[-1]

    @pl.when(k == 0)
    def _():
        acc_ref[...] = prod

    @pl.when(k > 0)
    def _():
        acc_ref[...] += prod

    @pl.when(k == nk - 1)
    def _():
        o_ref[...] = epilogue(acc_ref[...], *extra).astype(o_ref.dtype)


def _matmul(a, w3, layer, *, out_dtype, epilogue, extras=(), tm=1024, tn=1024, tk=None, col0=0, n=None,
            w_is_nk=False, name="matmul"):
    m, kdim = a.shape
    n_total = w3.shape[1] if w_is_nk else w3.shape[2]
    n = n_total - col0 if n is None else n
    tk = kdim if tk is None else tk
    nk = kdim // tk
    tn = min(tn, n)
    assert m % tm == 0 and n % tn == 0 and kdim % tk == 0
    cache_w = w3.dtype != BF16
    if w_is_nk:
        assert cache_w and nk == 1 and col0 % 8 == 0
        w_spec = pl.BlockSpec((pl.Element(1), pl.Element(tn), pl.Element(tk)),
                              lambda j, i, k: (layer, pl.multiple_of(col0 + j * tn, 8), 0))
    else:
        assert col0 % tn == 0
        jb0 = col0 // tn
        if cache_w:
            w_map = lambda j, i, k: (layer, jnp.where(i == 0, k, nk - 1), jb0 + j)
        else:
            w_map = lambda j, i, k: (layer, k, jb0 + j)
        w_spec = pl.BlockSpec((None, tk, tn), w_map)
    in_specs = [pl.BlockSpec((tm, tk), lambda j, i, k: (i, k)), w_spec]
    for _, bshape, imap in extras:
        in_specs.append(pl.BlockSpec(bshape, functools.partial(lambda j, i, k, f: f(i, j), f=imap)))
    scratch = []
    if cache_w:
        scratch.append(pltpu.VMEM((nk, tk, tn), BF16))
    if nk > 1:
        scratch.append(pltpu.VMEM((tm, tn), F32))
    return pl.pallas_call(
        functools.partial(_mm_kernel, nk=nk, n_extra=len(extras), epilogue=epilogue, cache_w=cache_w,
                          w_is_nk=w_is_nk),
        grid=(n // tn, m // tm, nk),
        in_specs=in_specs,
        out_specs=pl.BlockSpec((tm, tn), lambda j, i, k: (i, j)),
        out_shape=jax.ShapeDtypeStruct((m, n), out_dtype),
        scratch_shapes=scratch,
        compiler_params=_params(3),
        name=name,
    )(a, w3, *[e[0] for e in extras])


def _epi_plain(acc):
    return acc


def _epi_headnorm(acc, g_ref):
    g = g_ref[...]
    outs = []
    for h in range(acc.shape[1] // HEAD_DIM):
        a = acc[:, h * HEAD_DIM:(h + 1) * HEAD_DIM]
        r = lax.rsqrt(jnp.mean(a * a, axis=-1, keepdims=True) + EPS)
        outs.append((a * r) * g)
    return jnp.concatenate(outs, axis=1)


def _epi_kv(acc, g_ref):
    return jnp.where(pl.program_id(0) == 0, _epi_headnorm(acc, g_ref), acc)


def _epi_rotary(acc, cos_ref, sin_ref):
    c = cos_ref[...]
    s = sin_ref[...]
    scale = jnp.where(pl.program_id(0) == 0, 1.0, RET_QK_DIM ** -0.5).astype(F32)
    outs = []
    for h in range(acc.shape[1] // RET_QK_DIM):
        a = acc[:, h * RET_QK_DIM:(h + 1) * RET_QK_DIM]
        swapped = pltpu.roll(a, RET_QK_DIM // 2, axis=1)
        outs.append((a * c + swapped * s) * scale)
    return jnp.concatenate(outs, axis=1)


def _epi_residual(acc, x_ref):
    return x_ref[...] + acc


def _epi_relu2(acc):
    r = jnp.maximum(acc, 0.0)
    return r * r


def _topk_mask_kernel(iq_ref, qpack_ref, kpack_ref, bias_ref, score_ref, sel_ref, *, q0, kend, tq, topk):
    rows = COUNT_ROWS
    qbase = q0 + pl.program_id(1) * tq
    neg_inf = float("-inf")

    ik = kpack_ref[:, :IDX_DIM].astype(BF16)
    w_t = qpack_ref[...].T * (IDX_HEADS ** -0.5)
    score = None
    for h in range(IDX_HEADS):
        iqh = iq_ref[:, h * IDX_DIM:(h + 1) * IDX_DIM].astype(BF16)
        logits = lax.dot_general(ik, iqh, (((1,), (1,)), ((), ())), preferred_element_type=F32)
        term = jnp.maximum(logits, 0.0) * w_t[IDX_DIM + h:IDX_DIM + h + 1, :]
        score = term if score is None else score + term

    def causal_mask():
        kpos = lax.broadcasted_iota(jnp.int32, (kend, tq), 0)
        qcol = lax.broadcasted_iota(jnp.int32, (kend, tq), 1) + qbase
        return kpos <= qcol

    causal = causal_mask()
    score = jnp.where(score == 0.0, 0.0, score)
    lo_fill = jnp.where(causal, score, neg_inf)
    score_ref[...] = lo_fill
    smax = jnp.max(jnp.max(lo_fill.reshape(kend // rows, rows, tq), axis=0), axis=0, keepdims=True)
    hi_fill = jnp.where(causal, score, float("inf"))
    smin = jnp.min(jnp.min(hi_fill.reshape(kend // rows, rows, tq), axis=0), axis=0, keepdims=True)
    n_causal = jnp.minimum(lax.broadcasted_iota(jnp.int32, (1, tq), 1) + (qbase + 1), kend)

    def count_part(thr, c0, width):
        thr_b = jnp.broadcast_to(thr, (rows, width))
        part = jnp.zeros((rows, width), F32)
        for i in range(kend // rows):
            part = part + jnp.where(score_ref[i * rows:(i + 1) * rows, c0:c0 + width] >= thr_b, 1.0, 0.0)
        return part

    def count_ge(thr):
        return jnp.sum(count_part(thr, 0, tq), axis=0, keepdims=True)

    kf = float(topk)
    hw = tq // 2

    def step(part, state, first):
        lo, hi, tau, act, mid = state
        cnt = jnp.sum(part, axis=0, keepdims=True)
        ge = cnt >= kf
        hit = (cnt == kf) | (first & ge)
        lo = jnp.where(ge, mid, lo)
        hi = jnp.where(ge, hi, mid)
        nxt = 0.5 * lo + 0.5 * hi
        stuck = (nxt <= lo) | (nxt >= hi)
        live = act > 0
        tau = jnp.where(live & hit, mid, jnp.where(live & stuck, lo, tau))
        act = jnp.where(hit | stuck, 0, act)
        return lo, hi, tau, act, nxt

    def cond(c):
        return (c[1] > 0) & (c[0] < BISECT_MAX_STEPS)

    def body(c):
        it, _, part_a, st_a, st_b = c
        first = (jnp.zeros((1, hw), jnp.int32) + it) == 0
        st_a = step(part_a, st_a, first)
        part_b = count_part(st_b[4], hw, hw)
        st_b = step(part_b, st_b, first)
        part_a = count_part(st_a[4], 0, hw)
        return it + 1, jnp.sum(st_a[3]) + jnp.sum(st_b[3]), part_a, st_a, st_b

    lo0 = jnp.where(n_causal >= topk, smin, neg_inf)

    def init_state(c0):
        return (lo0[:, c0:c0 + hw], smax[:, c0:c0 + hw], jnp.full((1, hw), neg_inf, F32),
                jnp.ones((1, hw), jnp.int32), smax[:, c0:c0 + hw])

    st_a0, st_b0 = init_state(0), init_state(hw)
    out = lax.while_loop(cond, body, (jnp.int32(0), jnp.int32(tq), count_part(st_a0[4], 0, hw), st_a0, st_b0))
    tau = jnp.concatenate([out[3][2], out[4][2]], axis=1)

    sel_ref[...] = jnp.where(score_ref[...] >= tau, 0.0, NEG_BIG)
    tied = jnp.where((count_ge(tau) > kf) & (tau > neg_inf), 1, 0)

    @pl.when(jnp.max(tied) > 0)
    def _():
        tc = 512
        row = lax.broadcasted_iota(jnp.int32, (tc, tc), 0)
        col = lax.broadcasted_iota(jnp.int32, (tc, tc), 1)
        lower = jnp.where(col <= row, 1.0, 0.0).astype(BF16)
        n_gt = jnp.sum(jnp.where(score_ref[...] > tau, 1.0, 0.0), axis=0, keepdims=True)
        need = kf - n_gt
        carry = jnp.zeros((1, tq), F32)
        for c in range(kend // tc):
            sc = score_ref[c * tc:(c + 1) * tc, :]
            eq = jnp.where(sc == tau, 1.0, 0.0)
            prefix = jnp.dot(lower, eq.astype(BF16), preferred_element_type=F32) + carry
            carry = carry + jnp.sum(eq, axis=0, keepdims=True)
            keep = (sc > tau) | ((sc == tau) & (prefix <= need))
            sel_ref[c * tc:(c + 1) * tc, :] = jnp.where(keep, 0.0, NEG_BIG)

    bias_ref[...] = jnp.where(causal_mask(), sel_ref[...], NEG_BIG).astype(bias_ref.dtype)


def _topk_mask(iq3, ikw3, *, j, topk, tq=256):
    b = iq3.shape[0]
    kend = Q_SUPER * (j + 1)
    q0 = Q_SUPER * j
    nq = Q_SUPER // tq
    return pl.pallas_call(
        functools.partial(_topk_mask_kernel, q0=q0, kend=kend, tq=tq, topk=topk),
        grid=(b, nq),
        in_specs=[pl.BlockSpec((None, tq, IDX_Q_W), lambda bi, qi: (bi, q0 // tq + qi, 0)),
                  pl.BlockSpec((None, tq, LANES), lambda bi, qi: (bi, q0 // tq + qi, 0)),
                  pl.BlockSpec((None, kend, LANES), lambda bi, qi: (bi, 0, 0))],
        out_specs=pl.BlockSpec((None, kend, tq), lambda bi, qi: (bi, 0, qi)),
        out_shape=jax.ShapeDtypeStruct((b, kend, Q_SUPER), BF16),
        scratch_shapes=[pltpu.VMEM((kend, tq), F32), pltpu.VMEM((kend, tq), F32)],
        compiler_params=_params(2),
        name=f"topk_mask_{j}",
    )(iq3, ikw3, ikw3)


def _attn_kernel(q_ref, k_ref, v_ref, bias_t_ref, o_ref, *, tq):
    kend = k_ref.shape[0]
    exp2_scale = (HEAD_DIM ** -0.5) * math.log2(math.e)
    bias_t = bias_t_ref[...]
    row = lax.broadcasted_iota(jnp.int32, (tq, tq), 0)
    col = lax.broadcasted_iota(jnp.int32, (tq, tq), 1)
    eye = jnp.where(row == col, 1.0, 0.0).astype(BF16)
    eye_rows = jnp.concatenate([eye] * ATTN_GROUP, axis=0)
    ones = jnp.ones((kend, HEAD_DIM), BF16)
    for g in range(ATTN_KV_HEADS):
        kg = k_ref[:, g * HEAD_DIM:(g + 1) * HEAD_DIM]
        vg = v_ref[:, g * HEAD_DIM:(g + 1) * HEAD_DIM]
        h0 = g * ATTN_GROUP
        qg = jnp.concatenate([q_ref[:, (h0 + r) * HEAD_DIM:(h0 + r + 1) * HEAD_DIM]
                              for r in range(ATTN_GROUP)], axis=0)
        q_ext = jnp.concatenate([qg, eye_rows], axis=1)
        k_ext = jnp.concatenate([kg, bias_t], axis=1)
        t = lax.dot_general(q_ext, k_ext, (((1,), (1,)), ((), ())), preferred_element_type=F32)
        m = jnp.max(t, axis=-1, keepdims=True)
        p = jnp.exp2((t - m) * exp2_scale).astype(BF16)
        v_ext = jnp.concatenate([vg, ones], axis=1)
        o_ext = jnp.dot(p, v_ext, preferred_element_type=F32)
        o = o_ext[:, :HEAD_DIM] / o_ext[:, HEAD_DIM:]
        for r in range(ATTN_GROUP):
            o_ref[:, (h0 + r) * HEAD_DIM:(h0 + r + 1) * HEAD_DIM] = o[r * tq:(r + 1) * tq].astype(o_ref.dtype)


def _attention(q3, kv3, bias_t, *, j):
    b = q3.shape[0]
    kend = Q_SUPER * (j + 1)
    q0 = Q_SUPER * j
    tq = HEAD_DIM
    nq = Q_SUPER // tq
    return pl.pallas_call(
        functools.partial(_attn_kernel, tq=tq),
        grid=(b, nq),
        in_specs=[pl.BlockSpec((None, tq, ATTN_W), lambda bi, qi: (bi, q0 // tq + qi, 0)),
                  pl.BlockSpec((None, kend, KV_W), lambda bi, qi: (bi, 0, 0)),
                  pl.BlockSpec((None, kend, KV_W), lambda bi, qi: (bi, 0, 1)),
                  pl.BlockSpec((None, kend, tq), lambda bi, qi: (bi, 0, qi))],
        out_specs=pl.BlockSpec((None, tq, ATTN_W), lambda bi, qi: (bi, qi, 0)),
        out_shape=jax.ShapeDtypeStruct((b, Q_SUPER, ATTN_W), BF16),
        compiler_params=_params(2),
        name=f"attention_{j}",
    )(q3, kv3, kv3, bias_t)


def _retention_kernel(q_ref, k_ref, v_ref, gate_ref, decay_ref, xi_ref, zeta_ref, cd_ref, g_ref, b_ref,
                      o_ref, *, n_chunks):
    c = RET_CHUNK
    decay = decay_ref[...]
    xi = xi_ref[...]
    zeta = zeta_ref[...]
    cd = cd_ref[...]
    gn_g = g_ref[...]
    gn_b = b_ref[...]
    state = jnp.zeros((RET_QK_DIM, RET_V_DIM), F32)
    for n in range(n_chunks):
        sl = slice(n * c, (n + 1) * c)
        q = q_ref[sl, :]
        k = k_ref[sl, :]
        v = v_ref[sl, :]
        qk = lax.dot_general(q, k, (((1,), (1,)), ((), ())), preferred_element_type=F32) * decay
        inner = jnp.dot(qk.astype(BF16), v, preferred_element_type=F32)
        cross = jnp.dot(q, state.astype(BF16), preferred_element_type=F32) * xi
        kz_t = (k.astype(F32) * zeta).T.astype(BF16)
        upd = jnp.dot(kz_t, v, preferred_element_type=F32)
        state = upd + cd * state
        y = inner + cross
        mu = jnp.mean(y, axis=-1, keepdims=True)
        d = y - mu
        var = jnp.mean(d * d, axis=-1, keepdims=True)
        yn = d * lax.rsqrt(var + EPS)
        z = yn * gn_g + gn_b
        gate = gate_ref[sl, :].astype(F32)
        silu = gate * (1.0 / (1.0 + jnp.exp(-gate)))
        o_ref[sl, :] = (silu * z).astype(o_ref.dtype)


def _retention_tables():
    c = RET_CHUNK
    lg = np.log(1.0 - np.exp2(-5.0 - np.arange(RET_HEADS, dtype=np.float32))).astype(np.float32)
    pos = np.arange(c, dtype=np.float32)
    diff = pos[:, None] - pos[None, :]
    decay = np.where(diff[None] >= 0, np.exp(lg[:, None, None] * np.maximum(diff, 0.0)[None]), 0.0)
    xi = np.exp(lg[:, None] * (pos[None, :] + 1.0))
    zeta = np.exp(lg[:, None] * (c - 1.0 - pos[None, :]))
    cd = np.exp(lg * c)
    f = lambda a: jnp.asarray(a.astype(np.float32))
    return (f(decay),
            f(np.broadcast_to(xi[:, :, None], (RET_HEADS, c, RET_V_DIM))),
            f(np.broadcast_to(zeta[:, :, None], (RET_HEADS, c, RET_QK_DIM))),
            f(np.broadcast_to(cd[:, None, None], (RET_HEADS, 1, RET_V_DIM))))


def _retention(rot3, plain3, gn_g, gn_b):
    b, s, _ = rot3.shape
    decay, xi, zeta, cd = _retention_tables()
    v_blk0 = 0
    gate_blk0 = RET_V_W // RET_V_DIM
    return pl.pallas_call(
        functools.partial(_retention_kernel, n_chunks=s // RET_CHUNK),
        grid=(b, RET_HEADS),
        in_specs=[pl.BlockSpec((None, s, RET_QK_DIM), lambda bi, h: (bi, 0, h)),
                  pl.BlockSpec((None, s, RET_QK_DIM), lambda bi, h: (bi, 0, RET_HEADS + h)),
                  pl.BlockSpec((None, s, RET_V_DIM), lambda bi, h: (bi, 0, v_blk0 + h)),
                  pl.BlockSpec((None, s, RET_V_DIM), lambda bi, h: (bi, 0, gate_blk0 + h)),
                  pl.BlockSpec((None, RET_CHUNK, RET_CHUNK), lambda bi, h: (h, 0, 0)),
                  pl.BlockSpec((None, RET_CHUNK, RET_V_DIM), lambda bi, h: (h, 0, 0)),
                  pl.BlockSpec((None, RET_CHUNK, RET_QK_DIM), lambda bi, h: (h, 0, 0)),
                  pl.BlockSpec((None, 1, RET_V_DIM), lambda bi, h: (h, 0, 0)),
                  pl.BlockSpec((1, RET_V_DIM), lambda bi, h: (0, h)),
                  pl.BlockSpec((1, RET_V_DIM), lambda bi, h: (0, h))],
        out_specs=pl.BlockSpec((None, s, RET_V_DIM), lambda bi, h: (bi, 0, h)),
        out_shape=jax.ShapeDtypeStruct((b, s, RET_V_W), BF16),
        compiler_params=_params(2),
        name="retention",
    )(rot3, rot3, plain3, plain3, decay, xi, zeta, cd, gn_g.reshape(1, -1), gn_b.reshape(1, -1))


def _merge_kernel(oa_ref, or_ref, wa_ref, wr_ref, ga_ref, gb_ref, o_ref, wa_bf, wr_bf):
    @pl.when(pl.program_id(1) == 0)
    def _():
        wa_bf[...] = wa_ref[...].astype(BF16)
        wr_bf[...] = wr_ref[...].astype(BF16)

    ya = jnp.dot(oa_ref[...], wa_bf[...], preferred_element_type=F32)
    yr = jnp.dot(or_ref[...], wr_bf[...], preferred_element_type=F32)
    ga = ga_ref[...].astype(F32)
    gb = gb_ref[...].astype(F32)
    sa = 1.0 / (1.0 + jnp.exp(-ga))
    sb = 1.0 / (1.0 + jnp.exp(-gb))
    o_ref[...] = (sa * ya + sb * yr).astype(o_ref.dtype)


def _merge(o_attn, o_ret, w_ua3, w_ur3, layer, plain, tm=1024, tn=512):
    m = o_attn.shape[0]
    ga_blk0 = (2 * RET_V_W) // tn
    gb_blk0 = (2 * RET_V_W + D_MODEL) // tn
    return pl.pallas_call(
        _merge_kernel,
        grid=(D_MODEL // tn, m // tm),
        in_specs=[pl.BlockSpec((tm, ATTN_W), lambda j, i: (i, 0)),
                  pl.BlockSpec((tm, RET_V_W), lambda j, i: (i, 0)),
                  pl.BlockSpec((None, ATTN_W, tn), lambda j, i: (layer, 0, j)),
                  pl.BlockSpec((None, RET_V_W, tn), lambda j, i: (layer, 0, j)),
                  pl.BlockSpec((tm, tn), lambda j, i: (i, ga_blk0 + j)),
                  pl.BlockSpec((tm, tn), lambda j, i: (i, gb_blk0 + j))],
        out_specs=pl.BlockSpec((tm, tn), lambda j, i: (i, j)),
        out_shape=jax.ShapeDtypeStruct((m, D_MODEL), BF16),
        scratch_shapes=[pltpu.VMEM((ATTN_W, tn), BF16), pltpu.VMEM((RET_V_W, tn), BF16)],
        compiler_params=_params(2),
        name="gated_merge",
    )(o_attn, o_ret, w_ua3, w_ur3, plain, plain)


def _rotary_tables(s):
    pos = jnp.arange(s, dtype=F32)
    inv_freq = ROPE_BASE ** (-jnp.arange(0, RET_QK_DIM, 2, dtype=F32) / RET_QK_DIM)
    ang = pos[:, None] * inv_freq[None, :]
    cos, sin = jnp.cos(ang), jnp.sin(ang)
    return jnp.concatenate([cos, cos], axis=1), jnp.concatenate([-sin, sin], axis=1)


def kernel(x, ln1_g, w_in, q_norm_g, k_norm_g, ret_gn_g, ret_gn_b, w_up_attn, w_up_ret, w_out, ln2_g,
           w_ff1, w_ff2):
    b, s, d = x.shape
    m = b * s
    depth = w_in.shape[0]
    topk = min(TOPK_MAX, s // 4)
    assert s % Q_SUPER == 0 and topk <= Q_SUPER
    cos2, sin2 = _rotary_tables(s)
    tm = 1024
    s_tiles = s // tm

    w_in_t = jnp.swapaxes(w_in, 1, 2)
    c_kv = ATTN_W
    c_iq = c_kv + 2 * KV_W
    c_ikw = c_iq + IDX_Q_W
    c_rot = c_ikw + IDX_DIM + IDX_HEADS
    c_plain = c_rot + 2 * RET_QK_W
    in_proj = functools.partial(_matmul, w3=w_in_t, w_is_nk=True, tm=tm)

    xf = x.reshape(m, d)
    for l in range(depth):
        h = _rmsnorm(xf, ln1_g[l])
        aq = in_proj(h, layer=l, out_dtype=BF16, epilogue=_epi_headnorm, tn=1024, col0=0, n=ATTN_W,
                     name="proj_q",
                     extras=[(q_norm_g[l].reshape(1, HEAD_DIM), (1, HEAD_DIM), lambda i, j: (0, 0))])
        kv = in_proj(h, layer=l, out_dtype=BF16, epilogue=_epi_kv, tn=KV_W, col0=c_kv, n=2 * KV_W,
                     name="proj_kv",
                     extras=[(k_norm_g[l].reshape(1, HEAD_DIM), (1, HEAD_DIM), lambda i, j: (0, 0))])
        iq = in_proj(h, layer=l, out_dtype=F32, epilogue=_epi_plain, tn=IDX_Q_W, col0=c_iq, n=IDX_Q_W,
                     name="proj_iq")
        ikw = in_proj(h, layer=l, out_dtype=F32, epilogue=_epi_plain, tn=LANES, col0=c_ikw, n=LANES,
                      name="proj_ikw")
        rot = in_proj(h, layer=l, out_dtype=BF16, epilogue=_epi_rotary, tn=RET_QK_W, col0=c_rot,
                      n=2 * RET_QK_W, name="proj_rot",
                      extras=[(cos2, (tm, RET_QK_DIM), lambda i, j: (i % s_tiles, 0)),
                              (sin2, (tm, RET_QK_DIM), lambda i, j: (i % s_tiles, 0))])
        plain = in_proj(h, layer=l, out_dtype=BF16, epilogue=_epi_plain, tn=1024, col0=c_plain,
                        n=PLAIN_W, name="proj_plain")

        aq3 = aq.reshape(b, s, ATTN_W)
        kv3 = kv.reshape(b, s, 2 * KV_W)
        iq3 = iq.reshape(b, s, IDX_Q_W)
        ikw3 = ikw.reshape(b, s, LANES)
        plain3 = plain.reshape(b, s, PLAIN_W)
        rot3 = rot.reshape(b, s, 2 * RET_QK_W)

        o_parts = []
        for j in range(s // Q_SUPER):
            bias = _topk_mask(iq3, ikw3, j=j, topk=topk)
            o_parts.append(_attention(aq3, kv3, bias, j=j))
        o_attn = jnp.concatenate(o_parts, axis=1).reshape(m, ATTN_W)

        o_ret = _retention(rot3, plain3, ret_gn_g[l], ret_gn_b[l]).reshape(m, RET_V_W)

        merged = _merge(o_attn, o_ret, w_up_attn, w_up_ret, l, plain)
        xf = _matmul(merged, w_out, l, out_dtype=F32, epilogue=_epi_residual, tm=tm, tn=1024,
                     name="proj_out", extras=[(xf, (tm, 1024), lambda i, j: (i, j))])

        h2 = _rmsnorm(xf, ln2_g[l])
        f = _matmul(h2, w_ff1, l, out_dtype=BF16, epilogue=_epi_relu2, tm=tm, tn=1024, name="ff1")
        xf = _matmul(f, w_ff2, l, out_dtype=F32, epilogue=_epi_residual, tm=512, tn=1024, tk=2048,
                     name="ff2", extras=[(xf, (512, 1024), lambda i, j: (i, j))])
    return xf.reshape(b, s, d)
```

```python
import functools
import math

import jax
import jax.numpy as jnp
import numpy as np
from jax import lax
from jax.experimental import pallas as pl
from jax.experimental.pallas import tpu as pltpu

D_MODEL = 2048
HEAD_DIM = 128
ATTN_Q_HEADS = 16
ATTN_KV_HEADS = 4
ATTN_GROUP = ATTN_Q_HEADS // ATTN_KV_HEADS
IDX_HEADS = 16
IDX_DIM = 64
TOPK_MAX = 256
RET_HEADS = 8
RET_QK_DIM = 128
RET_V_DIM = 256
RET_CHUNK = 128
ROPE_BASE = 10000.0
D_FF = 4 * D_MODEL
EPS = 1e-6

ATTN_W = ATTN_Q_HEADS * HEAD_DIM
KV_W = ATTN_KV_HEADS * HEAD_DIM
IDX_Q_W = IDX_HEADS * IDX_DIM
RET_QK_W = RET_HEADS * RET_QK_DIM
RET_V_W = RET_HEADS * RET_V_DIM

LANES = 128
PLAIN_W = 2 * RET_V_W + 2 * D_MODEL
COUNT_ROWS = 64
BISECT_MAX_STEPS = 2200
BISECT_UNROLL = 4
Q_SUPER = 512
NEG_BIG = -1e30
VMEM_LIMIT = 56 * 1024 * 1024

F32 = jnp.float32
BF16 = jnp.bfloat16


def _params(n_axes):
    return pltpu.CompilerParams(dimension_semantics=("arbitrary",) * n_axes,
                                vmem_limit_bytes=VMEM_LIMIT)


def _row_sumsq(x):
    return jnp.broadcast_to(jnp.sum(x * x, axis=-1, keepdims=True), (x.shape[0], LANES))


def _norm_prep_kernel(x_ref, g_ref, xg_ref, ss_ref):
    x = x_ref[...]
    xg_ref[...] = (x * g_ref[...]).astype(xg_ref.dtype)
    ss_ref[...] = _row_sumsq(x)


def _norm_prep(x, g, tm=512):
    m, d = x.shape
    return pl.pallas_call(
        _norm_prep_kernel,
        grid=(m // tm,),
        in_specs=[pl.BlockSpec((tm, d), lambda i: (i, 0)),
                  pl.BlockSpec((1, d), lambda i: (0, 0))],
        out_specs=[pl.BlockSpec((tm, d), lambda i: (i, 0)),
                   pl.BlockSpec((None, tm, LANES), lambda i: (0, i, 0))],
        out_shape=[jax.ShapeDtypeStruct((m, d), BF16), jax.ShapeDtypeStruct((1, m, LANES), F32)],
        compiler_params=_params(1),
        name="norm_prep",
    )(x, g.reshape(1, d))


def _mm_kernel(*refs, nk, n_extra, epilogue, cache_w, w_is_nk, row_scaled, norm_out, d_norm):
    a_ref, w_ref = refs[0], refs[1]
    n_in = 2 + int(row_scaled) + n_extra + int(norm_out)
    ss_in_ref = refs[2] if row_scaled else None
    extra = refs[2 + int(row_scaled):2 + int(row_scaled) + n_extra]
    gain_ref = refs[n_in - 1] if norm_out else None
    o_ref = refs[n_in]
    scratch = refs[n_in + (3 if norm_out else 1):]
    i = pl.program_id(1)
    k = pl.program_id(2)
    if cache_w:
        wbf_ref = scratch[0]

        @pl.when(i == 0)
        def _():
            wbf_ref[k] = (w_ref[0].T if w_is_nk else w_ref[...]).astype(BF16)

        w = wbf_ref[k]
    else:
        w = w_ref[...]
    prod = jnp.dot(a_ref[...], w, preferred_element_type=F32)

    def finish(acc):
        if row_scaled:
            r = lax.rsqrt(jnp.sum(ss_in_ref[...], axis=0) * (1.0 / d_norm) + EPS)
            acc = acc * jnp.concatenate([r] * (acc.shape[1] // LANES), axis=1)
        out = epilogue(acc, *extra)
        o_ref[...] = out.astype(o_ref.dtype)
        if norm_out:
            xg_ref, ss_out_ref = refs[n_in + 1], refs[n_in + 2]
            xg_ref[...] = (out * gain_ref[...]).astype(xg_ref.dtype)
            ss_out_ref[...] = _row_sumsq(out)

    if nk == 1:
        finish(prod)
        return
    acc_ref = scratch[-1]

    @pl.when(k == 0)
    def _():
        acc_ref[...] = prod

    @pl.when(k > 0)
    def _():
        acc_ref[...] += prod

    @pl.when(k == nk - 1)
    def _():
        finish(acc_ref[...])


def _matmul(a, w3, layer, *, out_dtype, epilogue, extras=(), tm=1024, tn=1024, tk=None, col0=0, n=None,
            w_is_nk=False, row_ss=None, norm_gain=None, name="matmul"):
    m, kdim = a.shape
    n_total = w3.shape[1] if w_is_nk else w3.shape[2]
    n = n_total - col0 if n is None else n
    tk = kdim if tk is None else tk
    nk = kdim // tk
    tn = min(tn, n)
    assert m % tm == 0 and n % tn == 0 and kdim % tk == 0
    cache_w = w3.dtype != BF16
    if w_is_nk:
        assert cache_w and nk == 1 and col0 % 8 == 0
        w_spec = pl.BlockSpec((pl.Element(1), pl.Element(tn), pl.Element(tk)),
                              lambda j, i, k: (layer, pl.multiple_of(col0 + j * tn, 8), 0))
    else:
        assert col0 % tn == 0
        jb0 = col0 // tn
        if cache_w:
            w_map = lambda j, i, k: (layer, jnp.where(i == 0, k, nk - 1), jb0 + j)
        else:
            w_map = lambda j, i, k: (layer, k, jb0 + j)
        w_spec = pl.BlockSpec((None, tk, tn), w_map)
    operands = [a, w3]
    in_specs = [pl.BlockSpec((tm, tk), lambda j, i, k: (i, k)), w_spec]
    if row_ss is not None:
        operands.append(row_ss)
        in_specs.append(pl.BlockSpec((row_ss.shape[0], tm, LANES), lambda j, i, k: (0, i, 0)))
    for arr, bshape, imap in extras:
        operands.append(arr)
        in_specs.append(pl.BlockSpec(bshape, functools.partial(lambda j, i, k, f: f(i, j), f=imap)))
    out_specs = [pl.BlockSpec((tm, tn), lambda j, i, k: (i, j))]
    out_shape = [jax.ShapeDtypeStruct((m, n), out_dtype)]
    if norm_gain is not None:
        operands.append(norm_gain)
        in_specs.append(pl.BlockSpec((1, tn), lambda j, i, k: (0, j)))
        out_specs += [pl.BlockSpec((tm, tn), lambda j, i, k: (i, j)),
                      pl.BlockSpec((None, tm, LANES), lambda j, i, k: (j, i, 0))]
        out_shape += [jax.ShapeDtypeStruct((m, n), BF16), jax.ShapeDtypeStruct((n // tn, m, LANES), F32)]
    scratch = []
    if cache_w:
        scratch.append(pltpu.VMEM((nk, tk, tn), BF16))
    if nk > 1:
        scratch.append(pltpu.VMEM((tm, tn), F32))
    res = pl.pallas_call(
        functools.partial(_mm_kernel, nk=nk, n_extra=len(extras), epilogue=epilogue, cache_w=cache_w,
                          w_is_nk=w_is_nk, row_scaled=row_ss is not None, norm_out=norm_gain is not None,
                          d_norm=kdim),
        grid=(n // tn, m // tm, nk),
        in_specs=in_specs,
        out_specs=out_specs,
        out_shape=out_shape,
        scratch_shapes=scratch,
        compiler_params=_params(3),
        name=name,
    )(*operands)
    return res if norm_gain is not None else res[0]


def _epi_plain(acc):
    return acc


def _epi_headnorm(acc, g_ref):
    g = g_ref[...]
    outs = []
    for h in range(acc.shape[1] // HEAD_DIM):
        a = acc[:, h * HEAD_DIM:(h + 1) * HEAD_DIM]
        r = lax.rsqrt(jnp.mean(a * a, axis=-1, keepdims=True) + EPS)
        outs.append((a * r) * g)
    return jnp.concatenate(outs, axis=1)


def _epi_kv(acc, g_ref):
    return jnp.where(pl.program_id(0) == 0, _epi_headnorm(acc, g_ref), acc)


def _epi_rotary(acc, cos_ref, sin_ref):
    c = cos_ref[...]
    s = sin_ref[...]
    scale = jnp.where(pl.program_id(0) == 0, 1.0, RET_QK_DIM ** -0.5).astype(F32)
    outs = []
    for h in range(acc.shape[1] // RET_QK_DIM):
        a = acc[:, h * RET_QK_DIM:(h + 1) * RET_QK_DIM]
        swapped = pltpu.roll(a, RET_QK_DIM // 2, axis=1)
        outs.append((a * c + swapped * s) * scale)
    return jnp.concatenate(outs, axis=1)


def _epi_residual(acc, x_ref):
    return x_ref[...] + acc


def _epi_relu2(acc):
    r = jnp.maximum(acc, 0.0)
    return r * r


def _topk_mask_kernel(iq_ref, qpack_ref, kpack_ref, bias_ref, score_ref, sel_ref, *, q0, kend, tq, topk):
    rows = COUNT_ROWS
    qbase = q0 + pl.program_id(1) * tq
    neg_inf = float("-inf")

    ik = kpack_ref[:, :IDX_DIM].astype(BF16)
    w_t = qpack_ref[...].T * (IDX_HEADS ** -0.5)
    score = None
    for h in range(IDX_HEADS):
        iqh = iq_ref[:, h * IDX_DIM:(h + 1) * IDX_DIM].astype(BF16)
        logits = lax.dot_general(ik, iqh, (((1,), (1,)), ((), ())), preferred_element_type=F32)
        term = jnp.maximum(logits, 0.0) * w_t[IDX_DIM + h:IDX_DIM + h + 1, :]
        score = term if score is None else score + term

    def causal_mask():
        kpos = lax.broadcasted_iota(jnp.int32, (kend, tq), 0)
        qcol = lax.broadcasted_iota(jnp.int32, (kend, tq), 1) + qbase
        return kpos <= qcol

    causal = causal_mask()
    score = jnp.where(score == 0.0, 0.0, score)
    lo_fill = jnp.where(causal, score, neg_inf)
    score_ref[...] = lo_fill
    smax = jnp.max(jnp.max(lo_fill.reshape(kend // rows, rows, tq), axis=0), axis=0, keepdims=True)
    hi_fill = jnp.where(causal, score, float("inf"))
    smin = jnp.min(jnp.min(hi_fill.reshape(kend // rows, rows, tq), axis=0), axis=0, keepdims=True)
    n_causal = jnp.minimum(lax.broadcasted_iota(jnp.int32, (1, tq), 1) + (qbase + 1), kend)

    def count_part(thr, c0, width):
        thr_b = jnp.broadcast_to(thr, (rows, width))
        part = jnp.zeros((rows, width), F32)
        for i in range(kend // rows):
            part = part + jnp.where(score_ref[i * rows:(i + 1) * rows, c0:c0 + width] >= thr_b, 1.0, 0.0)
        return part

    def count_ge(thr):
        return jnp.sum(count_part(thr, 0, tq), axis=0, keepdims=True)

    kf = float(topk)
    hw = tq // 2

    def step(part, state, first):
        lo, hi, tau, act, mid = state
        cnt = jnp.sum(part, axis=0, keepdims=True)
        ge = cnt >= kf
        hit = (cnt == kf) | (first & ge)
        lo = jnp.where(ge, mid, lo)
        hi = jnp.where(ge, hi, mid)
        nxt = 0.5 * lo + 0.5 * hi
        stuck = (nxt <= lo) | (nxt >= hi)
        live = act > 0
        tau = jnp.where(live & hit, mid, jnp.where(live & stuck, lo, tau))
        act = jnp.where(hit | stuck, 0, act)
        return lo, hi, tau, act, nxt

    def cond(c):
        return (c[1] > 0) & (c[0] < BISECT_MAX_STEPS)

    def body(c):
        it, _, part_a, st_a, st_b = c
        for u in range(BISECT_UNROLL):
            first = ((jnp.zeros((1, hw), jnp.int32) + it) == 0) if u == 0 else jnp.zeros((1, hw), jnp.bool_)
            st_a = step(part_a, st_a, first)
            part_b = count_part(st_b[4], hw, hw)
            st_b = step(part_b, st_b, first)
            part_a = count_part(st_a[4], 0, hw)
        return it + BISECT_UNROLL, jnp.sum(st_a[3]) + jnp.sum(st_b[3]), part_a, st_a, st_b

    lo0 = jnp.where(n_causal >= topk, smin, neg_inf)

    def init_state(c0):
        return (lo0[:, c0:c0 + hw], smax[:, c0:c0 + hw], jnp.full((1, hw), neg_inf, F32),
                jnp.ones((1, hw), jnp.int32), smax[:, c0:c0 + hw])

    st_a0, st_b0 = init_state(0), init_state(hw)
    out = lax.while_loop(cond, body, (jnp.int32(0), jnp.int32(tq), count_part(st_a0[4], 0, hw), st_a0, st_b0))
    tau = jnp.concatenate([out[3][2], out[4][2]], axis=1)

    sel_ref[...] = jnp.where(score_ref[...] >= tau, 0.0, NEG_BIG)
    tied = jnp.where((count_ge(tau) > kf) & (tau > neg_inf), 1, 0)

    @pl.when(jnp.max(tied) > 0)
    def _():
        tc = 512
        row = lax.broadcasted_iota(jnp.int32, (tc, tc), 0)
        col = lax.broadcasted_iota(jnp.int32, (tc, tc), 1)
        lower = jnp.where(col <= row, 1.0, 0.0).astype(BF16)
        n_gt = jnp.sum(jnp.where(score_ref[...] > tau, 1.0, 0.0), axis=0, keepdims=True)
        need = kf - n_gt
        carry = jnp.zeros((1, tq), F32)
        for c in range(kend // tc):
            sc = score_ref[c * tc:(c + 1) * tc, :]
            eq = jnp.where(sc == tau, 1.0, 0.0)
            prefix = jnp.dot(lower, eq.astype(BF16), preferred_element_type=F32) + carry
            carry = carry + jnp.sum(eq, axis=0, keepdims=True)
            keep = (sc > tau) | ((sc == tau) & (prefix <= need))
            sel_ref[c * tc:(c + 1) * tc, :] = jnp.where(keep, 0.0, NEG_BIG)

    bias_ref[...] = jnp.where(causal_mask(), sel_ref[...], NEG_BIG).astype(bias_ref.dtype)


def _topk_mask(iq3, ikw3, *, j, topk, tq=256):
    b = iq3.shape[0]
    kend = Q_SUPER * (j + 1)
    q0 = Q_SUPER * j
    nq = Q_SUPER // tq
    return pl.pallas_call(
        functools.partial(_topk_mask_kernel, q0=q0, kend=kend, tq=tq, topk=topk),
        grid=(b, nq),
        in_specs=[pl.BlockSpec((None, tq, IDX_Q_W), lambda bi, qi: (bi, q0 // tq + qi, 0)),
                  pl.BlockSpec((None, tq, LANES), lambda bi, qi: (bi, q0 // tq + qi, 0)),
                  pl.BlockSpec((None, kend, LANES), lambda bi, qi: (bi, 0, 0))],
        out_specs=pl.BlockSpec((None, kend, tq), lambda bi, qi: (bi, 0, qi)),
        out_shape=jax.ShapeDtypeStruct((b, kend, Q_SUPER), BF16),
        scratch_shapes=[pltpu.VMEM((kend, tq), F32), pltpu.VMEM((kend, tq), F32)],
        compiler_params=_params(2),
        name=f"topk_mask_{j}",
    )(iq3, ikw3, ikw3)


def _attn_kernel(q_ref, k_ref, v_ref, bias_t_ref, o_ref, *, tq):
    kend = k_ref.shape[0]
    exp2_scale = (HEAD_DIM ** -0.5) * math.log2(math.e)
    bias_t = bias_t_ref[...]
    row = lax.broadcasted_iota(jnp.int32, (tq, tq), 0)
    col = lax.broadcasted_iota(jnp.int32, (tq, tq), 1)
    eye = jnp.where(row == col, 1.0, 0.0).astype(BF16)
    eye_rows = jnp.concatenate([eye] * ATTN_GROUP, axis=0)
    ones = jnp.ones((kend, HEAD_DIM), BF16)
    for g in range(ATTN_KV_HEADS):
        kg = k_ref[:, g * HEAD_DIM:(g + 1) * HEAD_DIM]
        vg = v_ref[:, g * HEAD_DIM:(g + 1) * HEAD_DIM]
        h0 = g * ATTN_GROUP
        qg = jnp.concatenate([q_ref[:, (h0 + r) * HEAD_DIM:(h0 + r + 1) * HEAD_DIM]
                              for r in range(ATTN_GROUP)], axis=0)
        q_ext = jnp.concatenate([qg, eye_rows], axis=1)
        k_ext = jnp.concatenate([kg, bias_t], axis=1)
        t = lax.dot_general(q_ext, k_ext, (((1,), (1,)), ((), ())), preferred_element_type=F32)
        m = jnp.max(t, axis=-1, keepdims=True)
        p = jnp.exp2((t - m) * exp2_scale).astype(BF16)
        v_ext = jnp.concatenate([vg, ones], axis=1)
        o_ext = jnp.dot(p, v_ext, preferred_element_type=F32)
        o = o_ext[:, :HEAD_DIM] / o_ext[:, HEAD_DIM:]
        for r in range(ATTN_GROUP):
            o_ref[:, (h0 + r) * HEAD_DIM:(h0 + r + 1) * HEAD_DIM] = o[r * tq:(r + 1) * tq].astype(o_ref.dtype)


def _attention(q3, kv3, bias_t, *, j):
    b = q3.shape[0]
    kend = Q_SUPER * (j + 1)
    q0 = Q_SUPER * j
    tq = HEAD_DIM
    nq = Q_SUPER // tq
    return pl.pallas_call(
        functools.partial(_attn_kernel, tq=tq),
        grid=(b, nq),
        in_specs=[pl.BlockSpec((None, tq, ATTN_W), lambda bi, qi: (bi, q0 // tq + qi, 0)),
                  pl.BlockSpec((None, kend, KV_W), lambda bi, qi: (bi, 0, 0)),
                  pl.BlockSpec((None, kend, KV_W), lambda bi, qi: (bi, 0, 1)),
                  pl.BlockSpec((None, kend, tq), lambda bi, qi: (bi, 0, qi))],
        out_specs=pl.BlockSpec((None, tq, ATTN_W), lambda bi, qi: (bi, qi, 0)),
        out_shape=jax.ShapeDtypeStruct((b, Q_SUPER, ATTN_W), BF16),
        compiler_params=_params(2),
        name=f"attention_{j}",
    )(q3, kv3, kv3, bias_t)


def _retention_kernel(q_ref, k_ref, v_ref, gate_ref, decay_ref, xi_ref, zeta_ref, cd_ref, g_ref, b_ref,
                      o_ref, *, n_chunks):
    c = RET_CHUNK
    decay = decay_ref[...]
    xi = xi_ref[...]
    zeta = zeta_ref[...]
    cd = cd_ref[...]
    gn_g = g_ref[...]
    gn_b = b_ref[...]
    state = jnp.zeros((RET_QK_DIM, RET_V_DIM), F32)
    for n in range(n_chunks):
        sl = slice(n * c, (n + 1) * c)
        q = q_ref[sl, :]
        k = k_ref[sl, :]
        v = v_ref[sl, :]
        qk = lax.dot_general(q, k, (((1,), (1,)), ((), ())), preferred_element_type=F32) * decay
        inner = jnp.dot(qk.astype(BF16), v, preferred_element_type=F32)
        cross = jnp.dot(q, state.astype(BF16), preferred_element_type=F32) * xi
        kz_t = (k.astype(F32) * zeta).T.astype(BF16)
        upd = jnp.dot(kz_t, v, preferred_element_type=F32)
        state = upd + cd * state
        y = inner + cross
        mu = jnp.mean(y, axis=-1, keepdims=True)
        d = y - mu
        var = jnp.mean(d * d, axis=-1, keepdims=True)
        yn = d * lax.rsqrt(var + EPS)
        z = yn * gn_g + gn_b
        gate = gate_ref[sl, :].astype(F32)
        silu = gate * (1.0 / (1.0 + jnp.exp(-gate)))
        o_ref[sl, :] = (silu * z).astype(o_ref.dtype)


def _retention_tables():
    c = RET_CHUNK
    lg = np.log(1.0 - np.exp2(-5.0 - np.arange(RET_HEADS, dtype=np.float32))).astype(np.float32)
    pos = np.arange(c, dtype=np.float32)
    diff = pos[:, None] - pos[None, :]
    decay = np.where(diff[None] >= 0, np.exp(lg[:, None, None] * np.maximum(diff, 0.0)[None]), 0.0)
    xi = np.exp(lg[:, None] * (pos[None, :] + 1.0))
    zeta = np.exp(lg[:, None] * (c - 1.0 - pos[None, :]))
    cd = np.exp(lg * c)
    f = lambda a: jnp.asarray(a.astype(np.float32))
    return (f(decay),
            f(np.broadcast_to(xi[:, :, None], (RET_HEADS, c, RET_V_DIM))),
            f(np.broadcast_to(zeta[:, :, None], (RET_HEADS, c, RET_QK_DIM))),
            f(np.broadcast_to(cd[:, None, None], (RET_HEADS, 1, RET_V_DIM))))


def _retention(rot3, plain3, gn_g, gn_b):
    b, s, _ = rot3.shape
    decay, xi, zeta, cd = _retention_tables()
    v_blk0 = 0
    gate_blk0 = RET_V_W // RET_V_DIM
    return pl.pallas_call(
        functools.partial(_retention_kernel, n_chunks=s // RET_CHUNK),
        grid=(b, RET_HEADS),
        in_specs=[pl.BlockSpec((None, s, RET_QK_DIM), lambda bi, h: (bi, 0, h)),
                  pl.BlockSpec((None, s, RET_QK_DIM), lambda bi, h: (bi, 0, RET_HEADS + h)),
                  pl.BlockSpec((None, s, RET_V_DIM), lambda bi, h: (bi, 0, v_blk0 + h)),
                  pl.BlockSpec((None, s, RET_V_DIM), lambda bi, h: (bi, 0, gate_blk0 + h)),
                  pl.BlockSpec((None, RET_CHUNK, RET_CHUNK), lambda bi, h: (h, 0, 0)),
                  pl.BlockSpec((None, RET_CHUNK, RET_V_DIM), lambda bi, h: (h, 0, 0)),
                  pl.BlockSpec((None, RET_CHUNK, RET_QK_DIM), lambda bi, h: (h, 0, 0)),
                  pl.BlockSpec((None, 1, RET_V_DIM), lambda bi, h: (h, 0, 0)),
                  pl.BlockSpec((1, RET_V_DIM), lambda bi, h: (0, h)),
                  pl.BlockSpec((1, RET_V_DIM), lambda bi, h: (0, h))],
        out_specs=pl.BlockSpec((None, s, RET_V_DIM), lambda bi, h: (bi, 0, h)),
        out_shape=jax.ShapeDtypeStruct((b, s, RET_V_W), BF16),
        compiler_params=_params(2),
        name="retention",
    )(rot3, rot3, plain3, plain3, decay, xi, zeta, cd, gn_g.reshape(1, -1), gn_b.reshape(1, -1))


def _merge_kernel(oa_ref, or_ref, wa_ref, wr_ref, ga_ref, gb_ref, o_ref, wa_bf, wr_bf):
    @pl.when(pl.program_id(1) == 0)
    def _():
        wa_bf[...] = wa_ref[...].astype(BF16)
        wr_bf[...] = wr_ref[...].astype(BF16)

    ya = jnp.dot(oa_ref[...], wa_bf[...], preferred_element_type=F32)
    yr = jnp.dot(or_ref[...], wr_bf[...], preferred_element_type=F32)
    ga = ga_ref[...].astype(F32)
    gb = gb_ref[...].astype(F32)
    sa = 1.0 / (1.0 + jnp.exp(-ga))
    sb = 1.0 / (1.0 + jnp.exp(-gb))
    o_ref[...] = (sa * ya + sb * yr).astype(o_ref.dtype)


def _merge(o_attn, o_ret, w_ua3, w_ur3, layer, plain, tm=1024, tn=512):
    m = o_attn.shape[0]
    ga_blk0 = (2 * RET_V_W) // tn
    gb_blk0 = (2 * RET_V_W + D_MODEL) // tn
    return pl.pallas_call(
        _merge_kernel,
        grid=(D_MODEL // tn, m // tm),
        in_specs=[pl.BlockSpec((tm, ATTN_W), lambda j, i: (i, 0)),
                  pl.BlockSpec((tm, RET_V_W), lambda j, i: (i, 0)),
                  pl.BlockSpec((None, ATTN_W, tn), lambda j, i: (layer, 0, j)),
                  pl.BlockSpec((None, RET_V_W, tn), lambda j, i: (layer, 0, j)),
                  pl.BlockSpec((tm, tn), lambda j, i: (i, ga_blk0 + j)),
                  pl.BlockSpec((tm, tn), lambda j, i: (i, gb_blk0 + j))],
        out_specs=pl.BlockSpec((tm, tn), lambda j, i: (i, j)),
        out_shape=jax.ShapeDtypeStruct((m, D_MODEL), BF16),
        scratch_shapes=[pltpu.VMEM((ATTN_W, tn), BF16), pltpu.VMEM((RET_V_W, tn), BF16)],
        compiler_params=_params(2),
        name="gated_merge",
    )(o_attn, o_ret, w_ua3, w_ur3, plain, plain)


def _rotary_tables(s):
    pos = jnp.arange(s, dtype=F32)
    inv_freq = ROPE_BASE ** (-jnp.arange(0, RET_QK_DIM, 2, dtype=F32) / RET_QK_DIM)
    ang = pos[:, None] * inv_freq[None, :]
    cos, sin = jnp.cos(ang), jnp.sin(ang)
    return jnp.concatenate([cos, cos], axis=1), jnp.concatenate([-sin, sin], axis=1)


def kernel(x, ln1_g, w_in, q_norm_g, k_norm_g, ret_gn_g, ret_gn_b, w_up_attn, w_up_ret, w_out, ln2_g,
           w_ff1, w_ff2):
    b, s, d = x.shape
    m = b * s
    depth = w_in.shape[0]
    topk = min(TOPK_MAX, s // 4)
    assert s % Q_SUPER == 0 and topk <= Q_SUPER
    cos2, sin2 = _rotary_tables(s)
    tm = 1024
    s_tiles = s // tm

    w_in_t = jnp.swapaxes(w_in, 1, 2)
    c_kv = ATTN_W
    c_iq = c_kv + 2 * KV_W
    c_ikw = c_iq + IDX_Q_W
    c_rot = c_ikw + IDX_DIM + IDX_HEADS
    c_plain = c_rot + 2 * RET_QK_W
    in_proj = functools.partial(_matmul, w3=w_in_t, w_is_nk=True, tm=tm)

    xf = x.reshape(m, d)
    xg, ss = _norm_prep(xf, ln1_g[0])
    for l in range(depth):
        aq =in_proj(xg, layer=l, row_ss=ss, out_dtype=BF16, epilogue=_epi_headnorm, tn=1024, col0=0, n=ATTN_W,
                     name="proj_q",
                     extras=[(q_norm_g[l].reshape(1, HEAD_DIM), (1, HEAD_DIM), lambda i, j: (0, 0))])
        kv = in_proj(xg, layer=l, row_ss=ss, out_dtype=BF16, epilogue=_epi_kv, tn=KV_W, col0=c_kv, n=2 * KV_W,
                     name="proj_kv",
                     extras=[(k_norm_g[l].reshape(1, HEAD_DIM), (1, HEAD_DIM), lambda i, j: (0, 0))])
        iq = in_proj(xg, layer=l, row_ss=ss, out_dtype=F32, epilogue=_epi_plain, tn=IDX_Q_W, col0=c_iq, n=IDX_Q_W,
                     name="proj_iq")
        ikw = in_proj(xg, layer=l, row_ss=ss, out_dtype=F32, epilogue=_epi_plain, tn=LANES, col0=c_ikw, n=LANES,
                      name="proj_ikw")
        rot = in_proj(xg, layer=l, row_ss=ss, out_dtype=BF16, epilogue=_epi_rotary, tn=RET_QK_W, col0=c_rot,
                      n=2 * RET_QK_W, name="proj_rot",
                      extras=[(cos2, (tm, RET_QK_DIM), lambda i, j: (i % s_tiles, 0)),
                              (sin2, (tm, RET_QK_DIM), lambda i, j: (i % s_tiles, 0))])
        plain = in_proj(xg, layer=l, row_ss=ss, out_dtype=BF16, epilogue=_epi_plain, tn=1024, col0=c_plain,
                        n=PLAIN_W, name="proj_plain")

        aq3 = aq.reshape(b, s, ATTN_W)
        kv3 = kv.reshape(b, s, 2 * KV_W)
        iq3 = iq.reshape(b, s, IDX_Q_W)
        ikw3 = ikw.reshape(b, s, LANES)
        plain3 = plain.reshape(b, s, PLAIN_W)
        rot3 = rot.reshape(b, s, 2 * RET_QK_W)

        o_parts = []
        for j in range(s // Q_SUPER):
            bias = _topk_mask(iq3, ikw3, j=j, topk=topk)
            o_parts.append(_attention(aq3, kv3, bias, j=j))
        o_attn = jnp.concatenate(o_parts, axis=1).reshape(m, ATTN_W)

        o_ret = _retention(rot3, plain3, ret_gn_g[l], ret_gn_b[l]).reshape(m, RET_V_W)

        merged = _merge(o_attn, o_ret, w_up_attn, w_up_ret, l, plain)
        xf, xg2, ss2 = _matmul(merged, w_out, l, out_dtype=F32, epilogue=_epi_residual, tm=tm, tn=512,
                               name="proj_out", extras=[(xf, (tm, 512), lambda i, j: (i, j))],
                               norm_gain=ln2_g[l].reshape(1, d))

        f = _matmul(xg2, w_ff1, l, out_dtype=BF16, epilogue=_epi_relu2, tm=tm, tn=1024, row_ss=ss2, name="ff1")
        next_gain = ln1_g[l + 1].reshape(1, d) if l + 1 < depth else None
        res = _matmul(f, w_ff2, l, out_dtype=F32, epilogue=_epi_residual, tm=512, tn=1024, tk=2048,
                      name="ff2", extras=[(xf, (512, 1024), lambda i, j: (i, j))], norm_gain=next_gain)
        xf, xg, ss = res if next_gain is not None else (res, None, None)
    return xf.reshape(b, s, d)
```

```python
import functools
import math

import jax
import jax.numpy as jnp
import numpy as np
from jax import lax
from jax.experimental import pallas as pl
from jax.experimental.pallas import tpu as pltpu

D_MODEL = 2048
HEAD_DIM = 128
ATTN_Q_HEADS = 16
ATTN_KV_HEADS = 4
ATTN_GROUP = ATTN_Q_HEADS // ATTN_KV_HEADS
IDX_HEADS = 16
IDX_DIM = 64
TOPK_MAX = 256
RET_HEADS = 8
RET_QK_DIM = 128
RET_V_DIM = 256
RET_CHUNK = 128
ROPE_BASE = 10000.0
D_FF = 4 * D_MODEL
EPS = 1e-6

ATTN_W = ATTN_Q_HEADS * HEAD_DIM
KV_W = ATTN_KV_HEADS * HEAD_DIM
IDX_Q_W = IDX_HEADS * IDX_DIM
RET_QK_W = RET_HEADS * RET_QK_DIM
RET_V_W = RET_HEADS * RET_V_DIM

LANES = 128
PLAIN_W = 2 * RET_V_W + 2 * D_MODEL
COUNT_ROWS = 64
BISECT_MAX_STEPS = 2200
BISECT_UNROLL = 4
Q_SUPER = 512
HALF_ROWS = Q_SUPER // 2
NEG_BIG = -1e30
VMEM_LIMIT = 56 * 1024 * 1024

F32 = jnp.float32
BF16 = jnp.bfloat16


def _params(n_axes):
    return pltpu.CompilerParams(dimension_semantics=("arbitrary",) * n_axes,
                                vmem_limit_bytes=VMEM_LIMIT)


def _row_sumsq(x):
    return jnp.broadcast_to(jnp.sum(x * x, axis=-1, keepdims=True), (x.shape[0], LANES))


def _norm_prep_kernel(x_ref, g_ref, xg_ref, ss_ref):
    x = x_ref[...]
    xg_ref[...] = (x * g_ref[...]).astype(xg_ref.dtype)
    ss_ref[...] = _row_sumsq(x)


def _norm_prep(x, g, tm=512):
    m, d = x.shape
    return pl.pallas_call(
        _norm_prep_kernel,
        grid=(m // tm,),
        in_specs=[pl.BlockSpec((tm, d), lambda i: (i, 0)),
                  pl.BlockSpec((1, d), lambda i: (0, 0))],
        out_specs=[pl.BlockSpec((tm, d), lambda i: (i, 0)),
                   pl.BlockSpec((None, tm, LANES), lambda i: (0, i, 0))],
        out_shape=[jax.ShapeDtypeStruct((m, d), BF16), jax.ShapeDtypeStruct((1, m, LANES), F32)],
        compiler_params=_params(1),
        name="norm_prep",
    )(x, g.reshape(1, d))


def _mm_kernel(*refs, nk, n_extra, epilogue, cache_w, w_is_nk, row_scaled, norm_out, d_norm):
    a_ref, w_ref = refs[0], refs[1]
    n_in = 2 + int(row_scaled) + n_extra + int(norm_out)
    ss_in_ref = refs[2] if row_scaled else None
    extra = refs[2 + int(row_scaled):2 + int(row_scaled) + n_extra]
    gain_ref = refs[n_in - 1] if norm_out else None
    o_ref = refs[n_in]
    scratch = refs[n_in + (3 if norm_out else 1):]
    i = pl.program_id(1)
    k = pl.program_id(2)
    if cache_w:
        wbf_ref = scratch[0]

        @pl.when(i == 0)
        def _():
            wbf_ref[k] = (w_ref[0].T if w_is_nk else w_ref[...]).astype(BF16)

        w = wbf_ref[k]
    else:
        w = w_ref[...]
    prod = jnp.dot(a_ref[...], w, preferred_element_type=F32)

    def finish(acc):
        if row_scaled:
            r = lax.rsqrt(jnp.sum(ss_in_ref[...], axis=0) * (1.0 / d_norm) + EPS)
            acc = acc * jnp.concatenate([r] * (acc.shape[1] // LANES), axis=1)
        out = epilogue(acc, *extra)
        o_ref[...] = out.astype(o_ref.dtype)
        if norm_out:
            xg_ref, ss_out_ref = refs[n_in + 1], refs[n_in + 2]
            xg_ref[...] = (out * gain_ref[...]).astype(xg_ref.dtype)
            ss_out_ref[...] = _row_sumsq(out)

    if nk == 1:
        finish(prod)
        return
    acc_ref = scratch[-1]

    @pl.when(k == 0)
    def _():
        acc_ref[...] = prod

    @pl.when(k > 0)
    def _():
        acc_ref[...] += prod

    @pl.when(k == nk - 1)
    def _():
        finish(acc_ref[...])


def _matmul(a, w3, layer, *, out_dtype, epilogue, extras=(), tm=1024, tn=1024, tk=None, col0=0, n=None,
            w_is_nk=False, row_ss=None, norm_gain=None, name="matmul"):
    m, kdim = a.shape
    n_total = w3.shape[1] if w_is_nk else w3.shape[2]
    n = n_total - col0 if n is None else n
    tk = kdim if tk is None else tk
    nk = kdim // tk
    tn = min(tn, n)
    assert m % tm == 0 and n % tn == 0 and kdim % tk == 0
    cache_w = w3.dtype != BF16
    if w_is_nk:
        assert cache_w and nk == 1 and col0 % 8 == 0
        w_spec = pl.BlockSpec((pl.Element(1), pl.Element(tn), pl.Element(tk)),
                              lambda j, i, k: (layer, pl.multiple_of(col0 + j * tn, 8), 0))
    else:
        assert col0 % tn == 0
        jb0 = col0 // tn
        if cache_w:
            w_map = lambda j, i, k: (layer, jnp.where(i == 0, k, nk - 1), jb0 + j)
        else:
            w_map = lambda j, i, k: (layer, k, jb0 + j)
        w_spec = pl.BlockSpec((None, tk, tn), w_map)
    operands = [a, w3]
    in_specs = [pl.BlockSpec((tm, tk), lambda j, i, k: (i, k)), w_spec]
    if row_ss is not None:
        operands.append(row_ss)
        in_specs.append(pl.BlockSpec((row_ss.shape[0], tm, LANES), lambda j, i, k: (0, i, 0)))
    for arr, bshape, imap in extras:
        operands.append(arr)
        in_specs.append(pl.BlockSpec(bshape, functools.partial(lambda j, i, k, f: f(i, j), f=imap)))
    out_specs = [pl.BlockSpec((tm, tn), lambda j, i, k: (i, j))]
    out_shape = [jax.ShapeDtypeStruct((m, n), out_dtype)]
    if norm_gain is not None:
        operands.append(norm_gain)
        in_specs.append(pl.BlockSpec((1, tn), lambda j, i, k: (0, j)))
        out_specs += [pl.BlockSpec((tm, tn), lambda j, i, k: (i, j)),
                      pl.BlockSpec((None, tm, LANES), lambda j, i, k: (j, i, 0))]
        out_shape += [jax.ShapeDtypeStruct((m, n), BF16), jax.ShapeDtypeStruct((n // tn, m, LANES), F32)]
    scratch = []
    if cache_w:
        scratch.append(pltpu.VMEM((nk, tk, tn), BF16))
    if nk > 1:
        scratch.append(pltpu.VMEM((tm, tn), F32))
    res = pl.pallas_call(
        functools.partial(_mm_kernel, nk=nk, n_extra=len(extras), epilogue=epilogue, cache_w=cache_w,
                          w_is_nk=w_is_nk, row_scaled=row_ss is not None, norm_out=norm_gain is not None,
                          d_norm=kdim),
        grid=(n // tn, m // tm, nk),
        in_specs=in_specs,
        out_specs=out_specs,
        out_shape=out_shape,
        scratch_shapes=scratch,
        compiler_params=_params(3),
        name=name,
    )(*operands)
    return res if norm_gain is not None else res[0]


def _epi_plain(acc):
    return acc


def _epi_headnorm(acc, g_ref):
    g = g_ref[...]
    outs = []
    for h in range(acc.shape[1] // HEAD_DIM):
        a = acc[:, h * HEAD_DIM:(h + 1) * HEAD_DIM]
        r = lax.rsqrt(jnp.mean(a * a, axis=-1, keepdims=True) + EPS)
        outs.append((a * r) * g)
    return jnp.concatenate(outs, axis=1)


def _epi_kv(acc, g_ref):
    return jnp.where(pl.program_id(0) == 0, _epi_headnorm(acc, g_ref), acc)


def _epi_rotary(acc, cos_ref, sin_ref):
    c = cos_ref[...]
    s = sin_ref[...]
    scale = jnp.where(pl.program_id(0) == 0, 1.0, RET_QK_DIM ** -0.5).astype(F32)
    outs = []
    for h in range(acc.shape[1] // RET_QK_DIM):
        a = acc[:, h * RET_QK_DIM:(h + 1) * RET_QK_DIM]
        swapped = pltpu.roll(a, RET_QK_DIM // 2, axis=1)
        outs.append((a * c + swapped * s) * scale)
    return jnp.concatenate(outs, axis=1)


def _epi_residual(acc, x_ref):
    return x_ref[...] + acc


def _epi_relu2(acc):
    r = jnp.maximum(acc, 0.0)
    return r * r


def _topk_tile(iq_ref, qpack_ref, kpack_ref, bias_ref, score_ref, sel_ref, *, qbase, keys, tq, topk):
    rows = COUNT_ROWS
    neg_inf = float("-inf")

    ik = kpack_ref[:keys, :IDX_DIM].astype(BF16)
    w_t = qpack_ref[...].T * (IDX_HEADS ** -0.5)
    score = None
    for h in range(IDX_HEADS):
        iqh = iq_ref[:, h * IDX_DIM:(h + 1) * IDX_DIM].astype(BF16)
        logits = lax.dot_general(ik, iqh, (((1,), (1,)), ((), ())), preferred_element_type=F32)
        term = jnp.maximum(logits, 0.0) * w_t[IDX_DIM + h:IDX_DIM + h + 1, :]
        score = term if score is None else score + term

    def causal_mask():
        kpos = lax.broadcasted_iota(jnp.int32, (keys, tq), 0)
        qcol = lax.broadcasted_iota(jnp.int32, (keys, tq), 1) + qbase
        return kpos <= qcol

    causal = causal_mask()
    score = jnp.where(score == 0.0, 0.0, score)
    lo_fill = jnp.where(causal, score, neg_inf)
    score_ref[:keys, :] = lo_fill
    smax = jnp.max(jnp.max(lo_fill.reshape(keys // rows, rows, tq), axis=0), axis=0, keepdims=True)
    hi_fill = jnp.where(causal, score, float("inf"))
    smin = jnp.min(jnp.min(hi_fill.reshape(keys // rows, rows, tq), axis=0), axis=0, keepdims=True)
    n_causal = jnp.minimum(lax.broadcasted_iota(jnp.int32, (1, tq), 1) + (qbase + 1), keys)

    def count_part(thr, c0, width):
        thr_b = jnp.broadcast_to(thr, (rows, width))
        part = jnp.zeros((rows, width), F32)
        for i in range(keys // rows):
            part = part + jnp.where(score_ref[i * rows:(i + 1) * rows, c0:c0 + width] >= thr_b, 1.0, 0.0)
        return part

    def count_ge(thr):
        return jnp.sum(count_part(thr, 0, tq), axis=0, keepdims=True)

    kf = float(topk)
    hw = tq // 2

    def step(part, state, first):
        lo, hi, tau, act, mid = state
        cnt = jnp.sum(part, axis=0, keepdims=True)
        ge = cnt >= kf
        hit = (cnt == kf) | (first & ge)
        lo = jnp.where(ge, mid, lo)
        hi = jnp.where(ge, hi, mid)
        nxt = 0.5 * lo + 0.5 * hi
        stuck = (nxt <= lo) | (nxt >= hi)
        live = act > 0
        tau = jnp.where(live & hit, mid, jnp.where(live & stuck, lo, tau))
        act = jnp.where(hit | stuck, 0, act)
        return lo, hi, tau, act, nxt

    def cond(c):
        return (c[1] > 0) & (c[0] < BISECT_MAX_STEPS)

    def body(c):
        it, _, part_a, st_a, st_b = c
        for u in range(BISECT_UNROLL):
            first = ((jnp.zeros((1, hw), jnp.int32) + it) == 0) if u == 0 else jnp.zeros((1, hw), jnp.bool_)
            st_a = step(part_a, st_a, first)
            part_b = count_part(st_b[4], hw, hw)
            st_b = step(part_b, st_b, first)
            part_a = count_part(st_a[4], 0, hw)
        return it + BISECT_UNROLL, jnp.sum(st_a[3]) + jnp.sum(st_b[3]), part_a, st_a, st_b

    lo0 = jnp.where(n_causal >= topk, smin, neg_inf)

    def init_state(c0):
        return (lo0[:, c0:c0 + hw], smax[:, c0:c0 + hw], jnp.full((1, hw), neg_inf, F32),
                jnp.ones((1, hw), jnp.int32), smax[:, c0:c0 + hw])

    st_a0, st_b0 = init_state(0), init_state(hw)
    out = lax.while_loop(cond, body, (jnp.int32(0), jnp.int32(tq), count_part(st_a0[4], 0, hw), st_a0, st_b0))
    tau = jnp.concatenate([out[3][2], out[4][2]], axis=1)

    sel_ref[:keys, :] = jnp.where(score_ref[:keys, :] >= tau, 0.0, NEG_BIG)
    tied = jnp.where((count_ge(tau) > kf) & (tau > neg_inf), 1, 0)

    @pl.when(jnp.max(tied) > 0)
    def _():
        tc = 256
        row = lax.broadcasted_iota(jnp.int32, (tc, tc), 0)
        col = lax.broadcasted_iota(jnp.int32, (tc, tc), 1)
        lower = jnp.where(col <= row, 1.0, 0.0).astype(BF16)
        n_gt = jnp.sum(jnp.where(score_ref[:keys, :] > tau, 1.0, 0.0), axis=0, keepdims=True)
        need = kf - n_gt
        carry = jnp.zeros((1, tq), F32)
        for c in range(keys // tc):
            sc = score_ref[c * tc:(c + 1) * tc, :]
            eq = jnp.where(sc == tau, 1.0, 0.0)
            prefix = jnp.dot(lower, eq.astype(BF16), preferred_element_type=F32) + carry
            carry = carry + jnp.sum(eq, axis=0, keepdims=True)
            keep = (sc > tau) | ((sc == tau) & (prefix <= need))
            sel_ref[c * tc:(c + 1) * tc, :] = jnp.where(keep, 0.0, NEG_BIG)

    bias_ref[:keys, :] = jnp.where(causal_mask(), sel_ref[:keys, :], NEG_BIG).astype(bias_ref.dtype)


def _topk_mask_kernel(iq_ref, qpack_ref, kpack_ref, bias_ref, score_ref, sel_ref, *, q0, kend, tq, topk):
    half = pl.program_id(1)
    for which, keys in ((0, kend - tq), (1, kend)):
        @pl.when(half == which)
        def _():
            _topk_tile(iq_ref, qpack_ref, kpack_ref, bias_ref, score_ref, sel_ref,
                       qbase=q0 + which * tq, keys=keys, tq=tq, topk=topk)
            if keys < kend:
                bias_ref[keys:, :] = jnp.full((kend - keys, tq), NEG_BIG, bias_ref.dtype)


def _topk_mask(iq3, ikw3, *, j, topk):
    b = iq3.shape[0]
    kend = Q_SUPER * (j + 1)
    q0 = Q_SUPER * j
    tq = HALF_ROWS
    nq = Q_SUPER // tq
    return pl.pallas_call(
        functools.partial(_topk_mask_kernel, q0=q0, kend=kend, tq=tq, topk=topk),
        grid=(b, nq),
        in_specs=[pl.BlockSpec((None, tq, IDX_Q_W), lambda bi, qi: (bi, q0 // tq + qi, 0)),
                  pl.BlockSpec((None, tq, LANES), lambda bi, qi: (bi, q0 // tq + qi, 0)),
                  pl.BlockSpec((None, kend, LANES), lambda bi, qi: (bi, 0, 0))],
        out_specs=pl.BlockSpec((None, kend, tq), lambda bi, qi: (bi, 0, qi)),
        out_shape=jax.ShapeDtypeStruct((b, kend, Q_SUPER), BF16),
        scratch_shapes=[pltpu.VMEM((kend, tq), F32), pltpu.VMEM((kend, tq), F32)],
        compiler_params=_params(2),
        name=f"topk_mask_{j}",
    )(iq3, ikw3, ikw3)


def _attn_tile(q_ref, k_ref, v_ref, bias_t_ref, o_ref, *, row0, keys):
    tq = HEAD_DIM
    rows = slice(row0, row0 + tq)
    exp2_scale = (HEAD_DIM ** -0.5) * math.log2(math.e)
    bias_t = bias_t_ref[:keys, rows]
    row = lax.broadcasted_iota(jnp.int32, (tq, tq), 0)
    col = lax.broadcasted_iota(jnp.int32, (tq, tq), 1)
    eye = jnp.where(row == col, 1.0, 0.0).astype(BF16)
    eye_rows = jnp.concatenate([eye] * ATTN_GROUP, axis=0)
    ones = jnp.ones((keys, HEAD_DIM), BF16)
    for g in range(ATTN_KV_HEADS):
        kg = k_ref[:keys, g * HEAD_DIM:(g + 1) * HEAD_DIM]
        vg = v_ref[:keys, g * HEAD_DIM:(g + 1) * HEAD_DIM]
        h0 = g * ATTN_GROUP
        qg = jnp.concatenate([q_ref[rows, (h0 + r) * HEAD_DIM:(h0 + r + 1) * HEAD_DIM]
                              for r in range(ATTN_GROUP)], axis=0)
        q_ext = jnp.concatenate([qg, eye_rows], axis=1)
        k_ext = jnp.concatenate([kg, bias_t], axis=1)
        t = lax.dot_general(q_ext, k_ext, (((1,), (1,)), ((), ())), preferred_element_type=F32)
        m = jnp.max(t, axis=-1, keepdims=True)
        p = jnp.exp2((t - m) * exp2_scale).astype(BF16)
        v_ext = jnp.concatenate([vg, ones], axis=1)
        o_ext = jnp.dot(p, v_ext, preferred_element_type=F32)
        o = o_ext[:, :HEAD_DIM] / o_ext[:, HEAD_DIM:]
        for r in range(ATTN_GROUP):
            o_ref[rows, (h0 + r) * HEAD_DIM:(h0 + r + 1) * HEAD_DIM] = o[r * tq:(r + 1) * tq].astype(o_ref.dtype)


def _attn_kernel(q_ref, k_ref, v_ref, bias_t_ref, o_ref, *, kend):
    half = pl.program_id(1)
    for which, keys in ((0, kend - HALF_ROWS), (1, kend)):
        @pl.when(half == which)
        def _():
            for row0 in range(0, HALF_ROWS, HEAD_DIM):
                _attn_tile(q_ref, k_ref, v_ref, bias_t_ref, o_ref, row0=row0, keys=keys)


def _attention(q3, kv3, bias_t, *, j):
    b = q3.shape[0]
    kend = Q_SUPER * (j + 1)
    q0 = Q_SUPER * j
    tq = HALF_ROWS
    nq = Q_SUPER // tq
    return pl.pallas_call(
        functools.partial(_attn_kernel, kend=kend),
        grid=(b, nq),
        in_specs=[pl.BlockSpec((None, tq, ATTN_W), lambda bi, qi: (bi, q0 // tq + qi, 0)),
                  pl.BlockSpec((None, kend, KV_W), lambda bi, qi: (bi, 0, 0)),
                  pl.BlockSpec((None, kend, KV_W), lambda bi, qi: (bi, 0, 1)),
                  pl.BlockSpec((None, kend, tq), lambda bi, qi: (bi, 0, qi))],
        out_specs=pl.BlockSpec((None, tq, ATTN_W), lambda bi, qi: (bi, qi, 0)),
        out_shape=jax.ShapeDtypeStruct((b, Q_SUPER, ATTN_W), BF16),
        compiler_params=_params(2),
        name=f"attention_{j}",
    )(q3, kv3, kv3, bias_t)


def _retention_kernel(q_ref, k_ref, v_ref, gate_ref, decay_ref, xi_ref, zeta_ref, cd_ref, g_ref, b_ref,
                      o_ref, *, n_chunks):
    c = RET_CHUNK
    decay = decay_ref[...]
    xi = xi_ref[...]
    zeta = zeta_ref[...]
    cd = cd_ref[...]
    gn_g = g_ref[...]
    gn_b = b_ref[...]
    state = jnp.zeros((RET_QK_DIM, RET_V_DIM), F32)
    for n in range(n_chunks):
        sl = slice(n * c, (n + 1) * c)
        q = q_ref[sl, :]
        k = k_ref[sl, :]
        v = v_ref[sl, :]
        qk = lax.dot_general(q, k, (((1,), (1,)), ((), ())), preferred_element_type=F32) * decay
        inner = jnp.dot(qk.astype(BF16), v, preferred_element_type=F32)
        cross = jnp.dot(q, state.astype(BF16), preferred_element_type=F32) * xi
        kz_t = (k.astype(F32) * zeta).T.astype(BF16)
        upd = jnp.dot(kz_t, v, preferred_element_type=F32)
        state = upd + cd * state
        y = inner + cross
        mu = jnp.mean(y, axis=-1, keepdims=True)
        d = y - mu
        var = jnp.mean(d * d, axis=-1, keepdims=True)
        yn = d * lax.rsqrt(var + EPS)
        z = yn * gn_g + gn_b
        gate = gate_ref[sl, :].astype(F32)
        silu = gate * (1.0 / (1.0 + jnp.exp(-gate)))
        o_ref[sl, :] = (silu * z).astype(o_ref.dtype)


def _retention_tables():
    c = RET_CHUNK
    lg = np.log(1.0 - np.exp2(-5.0 - np.arange(RET_HEADS, dtype=np.float32))).astype(np.float32)
    pos = np.arange(c, dtype=np.float32)
    diff = pos[:, None] - pos[None, :]
    decay = np.where(diff[None] >= 0, np.exp(lg[:, None, None] * np.maximum(diff, 0.0)[None]), 0.0)
    xi = np.exp(lg[:, None] * (pos[None, :] + 1.0))
    zeta = np.exp(lg[:, None] * (c - 1.0 - pos[None, :]))
    cd = np.exp(lg * c)
    f = lambda a: jnp.asarray(a.astype(np.float32))
    return (f(decay),
            f(np.broadcast_to(xi[:, :, None], (RET_HEADS, c, RET_V_DIM))),
            f(np.broadcast_to(zeta[:, :, None], (RET_HEADS, c, RET_QK_DIM))),
            f(np.broadcast_to(cd[:, None, None], (RET_HEADS, 1, RET_V_DIM))))


def _retention(rot3, plain3, gn_g, gn_b):
    b, s, _ = rot3.shape
    decay, xi, zeta, cd = _retention_tables()
    v_blk0 = 0
    gate_blk0 = RET_V_W // RET_V_DIM
    return pl.pallas_call(
        functools.partial(_retention_kernel, n_chunks=s // RET_CHUNK),
        grid=(b, RET_HEADS),
        in_specs=[pl.BlockSpec((None, s, RET_QK_DIM), lambda bi, h: (bi, 0, h)),
                  pl.BlockSpec((None, s, RET_QK_DIM), lambda bi, h: (bi, 0, RET_HEADS + h)),
                  pl.BlockSpec((None, s, RET_V_DIM), lambda bi, h: (bi, 0, v_blk0 + h)),
                  pl.BlockSpec((None, s, RET_V_DIM), lambda bi, h: (bi, 0, gate_blk0 + h)),
                  pl.BlockSpec((None, RET_CHUNK, RET_CHUNK), lambda bi, h: (h, 0, 0)),
                  pl.BlockSpec((None, RET_CHUNK, RET_V_DIM), lambda bi, h: (h, 0, 0)),
                  pl.BlockSpec((None, RET_CHUNK, RET_QK_DIM), lambda bi, h: (h, 0, 0)),
                  pl.BlockSpec((None, 1, RET_V_DIM), lambda bi, h: (h, 0, 0)),
                  pl.BlockSpec((1, RET_V_DIM), lambda bi, h: (0, h)),
                  pl.BlockSpec((1, RET_V_DIM), lambda bi, h: (0, h))],
        out_specs=pl.BlockSpec((None, s, RET_V_DIM), lambda bi, h: (bi, 0, h)),
        out_shape=jax.ShapeDtypeStruct((b, s, RET_V_W), BF16),
        compiler_params=_params(2),
        name="retention",
    )(rot3, rot3, plain3, plain3, decay, xi, zeta, cd, gn_g.reshape(1, -1), gn_b.reshape(1, -1))


def _merge_kernel(oa_ref, or_ref, wa_ref, wr_ref, ga_ref, gb_ref, o_ref, wa_bf, wr_bf):
    @pl.when(pl.program_id(1) == 0)
    def _():
        wa_bf[...] = wa_ref[...].astype(BF16)
        wr_bf[...] = wr_ref[...].astype(BF16)

    ya = jnp.dot(oa_ref[...], wa_bf[...], preferred_element_type=F32)
    yr = jnp.dot(or_ref[...], wr_bf[...], preferred_element_type=F32)
    ga = ga_ref[...].astype(F32)
    gb = gb_ref[...].astype(F32)
    sa = 1.0 / (1.0 + jnp.exp(-ga))
    sb = 1.0 / (1.0 + jnp.exp(-gb))
    o_ref[...] = (sa * ya + sb * yr).astype(o_ref.dtype)


def _merge(o_attn, o_ret, w_ua3, w_ur3, layer, plain, tm=1024, tn=512):
    m = o_attn.shape[0]
    ga_blk0 = (2 * RET_V_W) // tn
    gb_blk0 = (2 * RET_V_W + D_MODEL) // tn
    return pl.pallas_call(
        _merge_kernel,
        grid=(D_MODEL // tn, m // tm),
        in_specs=[pl.BlockSpec((tm, ATTN_W), lambda j, i: (i, 0)),
                  pl.BlockSpec((tm, RET_V_W), lambda j, i: (i, 0)),
                  pl.BlockSpec((None, ATTN_W, tn), lambda j, i: (layer, 0, j)),
                  pl.BlockSpec((None, RET_V_W, tn), lambda j, i: (layer, 0, j)),
                  pl.BlockSpec((tm, tn), lambda j, i: (i, ga_blk0 + j)),
                  pl.BlockSpec((tm, tn), lambda j, i: (i, gb_blk0 + j))],
        out_specs=pl.BlockSpec((tm, tn), lambda j, i: (i, j)),
        out_shape=jax.ShapeDtypeStruct((m, D_MODEL), BF16),
        scratch_shapes=[pltpu.VMEM((ATTN_W, tn), BF16), pltpu.VMEM((RET_V_W, tn), BF16)],
        compiler_params=_params(2),
        name="gated_merge",
    )(o_attn, o_ret, w_ua3, w_ur3, plain, plain)


def _rotary_tables(s):
    pos = jnp.arange(s, dtype=F32)
    inv_freq = ROPE_BASE ** (-jnp.arange(0, RET_QK_DIM, 2, dtype=F32) / RET_QK_DIM)
    ang = pos[:, None] * inv_freq[None, :]
    cos, sin = jnp.cos(ang), jnp.sin(ang)
    return jnp.concatenate([cos, cos], axis=1), jnp.concatenate([-sin, sin], axis=1)


def kernel(x, ln1_g, w_in, q_norm_g, k_norm_g, ret_gn_g, ret_gn_b, w_up_attn, w_up_ret, w_out, ln2_g,
           w_ff1, w_ff2):
    b, s, d = x.shape
    m = b * s
    depth = w_in.shape[0]
    topk = min(TOPK_MAX, s // 4)
    assert s % Q_SUPER == 0 and topk <= Q_SUPER
    cos2, sin2 = _rotary_tables(s)
    tm = 1024
    s_tiles = s // tm

    w_in_t = jnp.swapaxes(w_in, 1, 2)
    c_kv = ATTN_W
    c_iq = c_kv + 2 * KV_W
    c_ikw = c_iq + IDX_Q_W
    c_rot = c_ikw + IDX_DIM + IDX_HEADS
    c_plain = c_rot + 2 * RET_QK_W
    in_proj = functools.partial(_matmul, w3=w_in_t, w_is_nk=True, tm=tm)

    xf = x.reshape(m, d)
    xg, ss = _norm_prep(xf, ln1_g[0])
    for l in range(depth):
        aq =in_proj(xg, layer=l, row_ss=ss, out_dtype=BF16, epilogue=_epi_headnorm, tn=1024, col0=0, n=ATTN_W,
                     name="proj_q",
                     extras=[(q_norm_g[l].reshape(1, HEAD_DIM), (1, HEAD_DIM), lambda i, j: (0, 0))])
        kv = in_proj(xg, layer=l, row_ss=ss, out_dtype=BF16, epilogue=_epi_kv, tn=KV_W, col0=c_kv, n=2 * KV_W,
                     name="proj_kv",
                     extras=[(k_norm_g[l].reshape(1, HEAD_DIM), (1, HEAD_DIM), lambda i, j: (0, 0))])
        iq = in_proj(xg, layer=l, row_ss=ss, out_dtype=F32, epilogue=_epi_plain, tn=IDX_Q_W, col0=c_iq, n=IDX_Q_W,
                     name="proj_iq")
        ikw = in_proj(xg, layer=l, row_ss=ss, out_dtype=F32, epilogue=_epi_plain, tn=LANES, col0=c_ikw, n=LANES,
                      name="proj_ikw")
        rot = in_proj(xg, layer=l, row_ss=ss, out_dtype=BF16, epilogue=_epi_rotary, tn=RET_QK_W, col0=c_rot,
                      n=2 * RET_QK_W, name="proj_rot",
                      extras=[(cos2, (tm, RET_QK_DIM), lambda i, j: (i % s_tiles, 0)),
                              (sin2, (tm, RET_QK_DIM), lambda i, j: (i % s_tiles, 0))])
        plain = in_proj(xg, layer=l, row_ss=ss, out_dtype=BF16, epilogue=_epi_plain, tn=1024, col0=c_plain,
                        n=PLAIN_W, name="proj_plain")

        aq3 = aq.reshape(b, s, ATTN_W)
        kv3 = kv.reshape(b, s, 2 * KV_W)
        iq3 = iq.reshape(b, s, IDX_Q_W)
        ikw3 = ikw.reshape(b, s, LANES)
        plain3 = plain.reshape(b, s, PLAIN_W)
        rot3 = rot.reshape(b, s, 2 * RET_QK_W)

        o_parts = []
        for j in range(s // Q_SUPER):
            bias = _topk_mask(iq3, ikw3, j=j, topk=topk)
            o_parts.append(_attention(aq3, kv3, bias, j=j))
        o_attn = jnp.concatenate(o_parts, axis=1).reshape(m, ATTN_W)

        o_ret = _retention(rot3, plain3, ret_gn_g[l], ret_gn_b[l]).reshape(m, RET_V_W)

        merged = _merge(o_attn, o_ret, w_up_attn, w_up_ret, l, plain)
        xf, xg2, ss2 = _matmul(merged, w_out, l, out_dtype=F32, epilogue=_epi_residual, tm=512, tn=1024,
                               name="proj_out", extras=[(xf, (512, 1024), lambda i, j: (i, j))],
                               norm_gain=ln2_g[l].reshape(1, d))

        f = _matmul(xg2, w_ff1, l, out_dtype=BF16, epilogue=_epi_relu2, tm=tm, tn=1024, row_ss=ss2, name="ff1")
        next_gain = ln1_g[l + 1].reshape(1, d) if l + 1 < depth else None
        res = _matmul(f, w_ff2, l, out_dtype=F32, epilogue=_epi_residual, tm=1024, tn=512, tk=2048,
                      name="ff2", extras=[(xf, (1024, 512), lambda i, j: (i, j))], norm_gain=next_gain)
        xf, xg, ss = res if next_gain is not None else (res, None, None)
    return xf.reshape(b, s, d)
```

```python
import functools
import math

import jax
import jax.numpy as jnp
import numpy as np
from jax import lax
from jax.experimental import pallas as pl
from jax.experimental.pallas import tpu as pltpu

D_MODEL = 2048
HEAD_DIM = 128
ATTN_Q_HEADS = 16
ATTN_KV_HEADS = 4
ATTN_GROUP = ATTN_Q_HEADS // ATTN_KV_HEADS
IDX_HEADS = 16
IDX_DIM = 64
TOPK_MAX = 256
RET_HEADS = 8
RET_QK_DIM = 128
RET_V_DIM = 256
RET_CHUNK = 128
ROPE_BASE = 10000.0
D_FF = 4 * D_MODEL
EPS = 1e-6

ATTN_W = ATTN_Q_HEADS * HEAD_DIM
KV_W = ATTN_KV_HEADS * HEAD_DIM
IDX_Q_W = IDX_HEADS * IDX_DIM
RET_QK_W = RET_HEADS * RET_QK_DIM
RET_V_W = RET_HEADS * RET_V_DIM

LANES = 128
PLAIN_W = 2 * RET_V_W + 2 * D_MODEL
IDX_PACK_W = IDX_Q_W + LANES
COUNT_ROWS = 64
BISECT_MAX_STEPS = 2200
BISECT_UNROLL = 4
Q_SUPER = 512
HALF_ROWS = Q_SUPER // 2
NEG_BIG = -1e30
VMEM_LIMIT = 56 * 1024 * 1024

F32 = jnp.float32
BF16 = jnp.bfloat16


def _params(n_axes):
    return pltpu.CompilerParams(dimension_semantics=("arbitrary",) * n_axes,
                                vmem_limit_bytes=VMEM_LIMIT)


def _row_sumsq(x):
    return jnp.broadcast_to(jnp.sum(x * x, axis=-1, keepdims=True), (x.shape[0], LANES))


def _norm_prep_kernel(x_ref, g_ref, xg_ref, ss_ref):
    x = x_ref[...]
    xg_ref[...] = (x * g_ref[...]).astype(xg_ref.dtype)
    ss_ref[...] = _row_sumsq(x)


def _norm_prep(x, g, tm=512):
    m, d = x.shape
    return pl.pallas_call(
        _norm_prep_kernel,
        grid=(m // tm,),
        in_specs=[pl.BlockSpec((tm, d), lambda i: (i, 0)),
                  pl.BlockSpec((1, d), lambda i: (0, 0))],
        out_specs=[pl.BlockSpec((tm, d), lambda i: (i, 0)),
                   pl.BlockSpec((None, tm, LANES), lambda i: (0, i, 0))],
        out_shape=[jax.ShapeDtypeStruct((m, d), BF16), jax.ShapeDtypeStruct((1, m, LANES), F32)],
        compiler_params=_params(1),
        name="norm_prep",
    )(x, g.reshape(1, d))


def _mm_kernel(*refs, nk, n_extra, epilogue, cache_w, w_is_nk, row_scaled, norm_out, d_norm):
    a_ref, w_ref = refs[0], refs[1]
    n_in = 2 + int(row_scaled) + n_extra + int(norm_out)
    ss_in_ref = refs[2] if row_scaled else None
    extra = refs[2 + int(row_scaled):2 + int(row_scaled) + n_extra]
    gain_ref = refs[n_in - 1] if norm_out else None
    o_ref = refs[n_in]
    scratch = refs[n_in + (3 if norm_out else 1):]
    i = pl.program_id(1)
    k = pl.program_id(2)
    if cache_w:
        wbf_ref = scratch[0]

        @pl.when(i == 0)
        def _():
            wbf_ref[k] = (w_ref[0].T if w_is_nk else w_ref[...]).astype(BF16)

        w = wbf_ref[k]
    else:
        w = w_ref[...]
    prod = jnp.dot(a_ref[...], w, preferred_element_type=F32)

    def finish(acc):
        if row_scaled:
            r = lax.rsqrt(jnp.sum(ss_in_ref[...], axis=0) * (1.0 / d_norm) + EPS)
            acc = acc * jnp.concatenate([r] * (acc.shape[1] // LANES), axis=1)
        out = epilogue(acc, *extra)
        o_ref[...] = out.astype(o_ref.dtype)
        if norm_out:
            xg_ref, ss_out_ref = refs[n_in + 1], refs[n_in + 2]
            xg_ref[...] = (out * gain_ref[...]).astype(xg_ref.dtype)
            ss_out_ref[...] = _row_sumsq(out)

    if nk == 1:
        finish(prod)
        return
    acc_ref = scratch[-1]

    @pl.when(k == 0)
    def _():
        acc_ref[...] = prod

    @pl.when(k > 0)
    def _():
        acc_ref[...] += prod

    @pl.when(k == nk - 1)
    def _():
        finish(acc_ref[...])


def _matmul(a, w3, layer, *, out_dtype, epilogue, extras=(), tm=1024, tn=1024, tk=None, col0=0, n=None,
            w_is_nk=False, row_ss=None, norm_gain=None, w_buffers=2, name="matmul"):
    m, kdim = a.shape
    n_total = w3.shape[1] if w_is_nk else w3.shape[2]
    n = n_total - col0 if n is None else n
    tk = kdim if tk is None else tk
    nk = kdim // tk
    tn = min(tn, n)
    assert m % tm == 0 and n % tn == 0 and kdim % tk == 0
    cache_w = w3.dtype != BF16
    if w_is_nk:
        assert cache_w and nk == 1 and col0 % 8 == 0
        w_spec = pl.BlockSpec((pl.Element(1), pl.Element(tn), pl.Element(tk)),
                              lambda j, i, k: (layer, pl.multiple_of(col0 + j * tn, 8), 0))
    else:
        assert col0 % tn == 0
        jb0 = col0 // tn
        if cache_w:
            w_map = lambda j, i, k: (layer, jnp.where(i == 0, k, nk - 1), jb0 + j)
        else:
            w_map = lambda j, i, k: (layer, k, jb0 + j)
        mode = {} if w_buffers == 2 else {"pipeline_mode": pl.Buffered(w_buffers)}
        w_spec = pl.BlockSpec((None, tk, tn), w_map, **mode)
    operands = [a, w3]
    in_specs = [pl.BlockSpec((tm, tk), lambda j, i, k: (i, k)), w_spec]
    if row_ss is not None:
        operands.append(row_ss)
        in_specs.append(pl.BlockSpec((row_ss.shape[0], tm, LANES), lambda j, i, k: (0, i, 0)))
    for arr, bshape, imap in extras:
        operands.append(arr)
        in_specs.append(pl.BlockSpec(bshape, functools.partial(lambda j, i, k, f: f(i, j), f=imap)))
    out_specs = [pl.BlockSpec((tm, tn), lambda j, i, k: (i, j))]
    out_shape = [jax.ShapeDtypeStruct((m, n), out_dtype)]
    if norm_gain is not None:
        operands.append(norm_gain)
        in_specs.append(pl.BlockSpec((1, tn), lambda j, i, k: (0, j)))
        out_specs += [pl.BlockSpec((tm, tn), lambda j, i, k: (i, j)),
                      pl.BlockSpec((None, tm, LANES), lambda j, i, k: (j, i, 0))]
        out_shape += [jax.ShapeDtypeStruct((m, n), BF16), jax.ShapeDtypeStruct((n // tn, m, LANES), F32)]
    scratch = []
    if cache_w:
        scratch.append(pltpu.VMEM((nk, tk, tn), BF16))
    if nk > 1:
        scratch.append(pltpu.VMEM((tm, tn), F32))
    res = pl.pallas_call(
        functools.partial(_mm_kernel, nk=nk, n_extra=len(extras), epilogue=epilogue, cache_w=cache_w,
                          w_is_nk=w_is_nk, row_scaled=row_ss is not None, norm_out=norm_gain is not None,
                          d_norm=kdim),
        grid=(n // tn, m // tm, nk),
        in_specs=in_specs,
        out_specs=out_specs,
        out_shape=out_shape,
        scratch_shapes=scratch,
        compiler_params=_params(3),
        name=name,
    )(*operands)
    return res if norm_gain is not None else res[0]


def _epi_plain(acc):
    return acc


def _epi_headnorm(acc, g_ref):
    g = g_ref[...]
    outs = []
    for h in range(acc.shape[1] // HEAD_DIM):
        a = acc[:, h * HEAD_DIM:(h + 1) * HEAD_DIM]
        r = lax.rsqrt(jnp.mean(a * a, axis=-1, keepdims=True) + EPS)
        outs.append((a * r) * g)
    return jnp.concatenate(outs, axis=1)


def _epi_kv(acc, g_ref):
    return jnp.where(pl.program_id(0) == 0, _epi_headnorm(acc, g_ref), acc)


def _epi_rotary(acc, cos_ref, sin_ref):
    c = cos_ref[...]
    s = sin_ref[...]
    scale = jnp.where(pl.program_id(0) == 0, 1.0, RET_QK_DIM ** -0.5).astype(F32)
    outs = []
    for h in range(acc.shape[1] // RET_QK_DIM):
        a = acc[:, h * RET_QK_DIM:(h + 1) * RET_QK_DIM]
        swapped = pltpu.roll(a, RET_QK_DIM // 2, axis=1)
        outs.append((a * c + swapped * s) * scale)
    return jnp.concatenate(outs, axis=1)


def _epi_residual(acc, x_ref):
    return x_ref[...] + acc


def _epi_relu2(acc):
    r = jnp.maximum(acc, 0.0)
    return r * r


def _topk_tile(iq_ref, qpack_ref, kpack_ref, bias_ref, score_ref, sel_ref, *, qbase, keys, tq, topk):
    rows = COUNT_ROWS
    neg_inf = float("-inf")

    ik = kpack_ref[:keys, :IDX_DIM].astype(BF16)
    w_t = qpack_ref[...].T * (IDX_HEADS ** -0.5)
    score = None
    for h in range(IDX_HEADS):
        iqh = iq_ref[:, h * IDX_DIM:(h + 1) * IDX_DIM].astype(BF16)
        logits = lax.dot_general(ik, iqh, (((1,), (1,)), ((), ())), preferred_element_type=F32)
        term = jnp.maximum(logits, 0.0) * w_t[IDX_DIM + h:IDX_DIM + h + 1, :]
        score = term if score is None else score + term

    def causal_mask():
        kpos = lax.broadcasted_iota(jnp.int32, (keys, tq), 0)
        qcol = lax.broadcasted_iota(jnp.int32, (keys, tq), 1) + qbase
        return kpos <= qcol

    causal = causal_mask()
    score = jnp.where(score == 0.0, 0.0, score)
    lo_fill = jnp.where(causal, score, neg_inf)
    score_ref[:keys, :] = lo_fill
    smax = jnp.max(jnp.max(lo_fill.reshape(keys // rows, rows, tq), axis=0), axis=0, keepdims=True)
    hi_fill = jnp.where(causal, score, float("inf"))
    smin = jnp.min(jnp.min(hi_fill.reshape(keys // rows, rows, tq), axis=0), axis=0, keepdims=True)
    n_causal = jnp.minimum(lax.broadcasted_iota(jnp.int32, (1, tq), 1) + (qbase + 1), keys)

    def count_part(thr, c0, width):
        thr_b = jnp.broadcast_to(thr, (rows, width))
        part = jnp.zeros((rows, width), F32)
        for i in range(keys // rows):
            part = part + jnp.where(score_ref[i * rows:(i + 1) * rows, c0:c0 + width] >= thr_b, 1.0, 0.0)
        return part

    kf = float(topk)
    hw = tq // 2

    def step(part, state, first):
        lo, hi, tau, act, mid, odd = state
        cnt = jnp.sum(part, axis=0, keepdims=True)
        ge = cnt >= kf
        hit = (cnt == kf) | (first & ge)
        lo = jnp.where(ge, mid, lo)
        hi = jnp.where(ge, hi, mid)
        nxt = 0.5 * lo + 0.5 * hi
        stuck = (nxt <= lo) | (nxt >= hi)
        live = act > 0
        tau = jnp.where(live & hit, mid, jnp.where(live & stuck, lo, tau))
        inexact = (hit & (cnt != kf)) | (jnp.logical_not(hit) & stuck)
        odd = jnp.where(live & inexact, 1, odd)
        act = jnp.where(hit | stuck, 0, act)
        return lo, hi, tau, act, nxt, odd

    def cond(c):
        return (c[1] > 0) & (c[0] < BISECT_MAX_STEPS)

    def body(c):
        it, _, part_a, st_a, st_b = c
        for u in range(BISECT_UNROLL):
            first = ((jnp.zeros((1, hw), jnp.int32) + it) == 0) if u == 0 else jnp.zeros((1, hw), jnp.bool_)
            st_a = step(part_a, st_a, first)
            part_b = count_part(st_b[4], hw, hw)
            st_b = step(part_b, st_b, first)
            part_a = count_part(st_a[4], 0, hw)
        return it + BISECT_UNROLL, jnp.sum(st_a[3]) + jnp.sum(st_b[3]), part_a, st_a, st_b

    lo0 = jnp.where(n_causal >= topk, smin, neg_inf)

    def init_state(c0):
        return (lo0[:, c0:c0 + hw], smax[:, c0:c0 + hw], jnp.full((1, hw), neg_inf, F32),
                jnp.ones((1, hw), jnp.int32), smax[:, c0:c0 + hw], jnp.zeros((1, hw), jnp.int32))

    st_a0, st_b0 = init_state(0), init_state(hw)
    out = lax.while_loop(cond, body, (jnp.int32(0), jnp.int32(tq), count_part(st_a0[4], 0, hw), st_a0, st_b0))
    tau = jnp.concatenate([out[3][2], out[4][2]], axis=1)

    odd = jnp.concatenate([out[3][5], out[4][5]], axis=1)

    picked = jnp.where(score_ref[:keys, :] >= tau, 0.0, NEG_BIG)
    bias_ref[:keys, :] = jnp.where(causal_mask(), picked, NEG_BIG).astype(bias_ref.dtype)

    @pl.when(jnp.max(jnp.where(tau > neg_inf, odd, 0)) > 0)
    def _():
        tc = 256
        row = lax.broadcasted_iota(jnp.int32, (tc, tc), 0)
        col = lax.broadcasted_iota(jnp.int32, (tc, tc), 1)
        lower = jnp.where(col <= row, 1.0, 0.0).astype(BF16)
        n_gt = jnp.sum(jnp.where(score_ref[:keys, :] > tau, 1.0, 0.0), axis=0, keepdims=True)
        need = kf - n_gt
        carry = jnp.zeros((1, tq), F32)
        for c in range(keys // tc):
            sc = score_ref[c * tc:(c + 1) * tc, :]
            eq = jnp.where(sc == tau, 1.0, 0.0)
            prefix = jnp.dot(lower, eq.astype(BF16), preferred_element_type=F32) + carry
            carry = carry + jnp.sum(eq, axis=0, keepdims=True)
            keep = (sc > tau) | ((sc == tau) & (prefix <= need))
            sel_ref[c * tc:(c + 1) * tc, :] = jnp.where(keep, 0.0, NEG_BIG)
        bias_ref[:keys, :] = jnp.where(causal_mask(), sel_ref[:keys, :], NEG_BIG).astype(bias_ref.dtype)


def _topk_mask_kernel(iq_ref, qpack_ref, kpack_ref, bias_ref, score_ref, sel_ref, *, q0, kend, tq, topk):
    half = pl.program_id(1)
    for which, keys in ((0, kend - tq), (1, kend)):
        @pl.when(half == which)
        def _():
            _topk_tile(iq_ref, qpack_ref, kpack_ref, bias_ref, score_ref, sel_ref,
                       qbase=q0 + which * tq, keys=keys, tq=tq, topk=topk)
            if keys < kend:
                bias_ref[keys:, :] = jnp.full((kend - keys, tq), NEG_BIG, bias_ref.dtype)


def _topk_mask(idx3, *, j, topk):
    b = idx3.shape[0]
    kend = Q_SUPER * (j + 1)
    q0 = Q_SUPER * j
    tq = HALF_ROWS
    nq = Q_SUPER // tq
    pack_blk = IDX_Q_W // LANES
    return pl.pallas_call(
        functools.partial(_topk_mask_kernel, q0=q0, kend=kend, tq=tq, topk=topk),
        grid=(b, nq),
        in_specs=[pl.BlockSpec((None, tq, IDX_Q_W), lambda bi, qi: (bi, q0 // tq + qi, 0)),
                  pl.BlockSpec((None, tq, LANES), lambda bi, qi: (bi, q0 // tq + qi, pack_blk)),
                  pl.BlockSpec((None, kend, LANES), lambda bi, qi: (bi, 0, pack_blk))],
        out_specs=pl.BlockSpec((None, kend, tq), lambda bi, qi: (bi, 0, qi)),
        out_shape=jax.ShapeDtypeStruct((b, kend, Q_SUPER), BF16),
        scratch_shapes=[pltpu.VMEM((kend, tq), F32), pltpu.VMEM((kend, tq), F32)],
        compiler_params=_params(2),
        name=f"topk_mask_{j}",
    )(idx3, idx3, idx3)


def _attn_tile(q_ref, k_ref, v_ref, bias_t_ref, o_ref, *, row0, keys):
    tq = HEAD_DIM
    rows = slice(row0, row0 + tq)
    exp2_scale = (HEAD_DIM ** -0.5) * math.log2(math.e)
    bias_t = bias_t_ref[:keys, rows]
    row = lax.broadcasted_iota(jnp.int32, (tq, tq), 0)
    col = lax.broadcasted_iota(jnp.int32, (tq, tq), 1)
    eye = jnp.where(row == col, 1.0, 0.0).astype(BF16)
    eye_rows = jnp.concatenate([eye] * ATTN_GROUP, axis=0)
    ones = jnp.ones((keys, HEAD_DIM), BF16)
    for g in range(ATTN_KV_HEADS):
        kg = k_ref[:keys, g * HEAD_DIM:(g + 1) * HEAD_DIM]
        vg = v_ref[:keys, g * HEAD_DIM:(g + 1) * HEAD_DIM]
        h0 = g * ATTN_GROUP
        qg = jnp.concatenate([q_ref[rows, (h0 + r) * HEAD_DIM:(h0 + r + 1) * HEAD_DIM]
                              for r in range(ATTN_GROUP)], axis=0)
        q_ext = jnp.concatenate([qg, eye_rows], axis=1)
        k_ext = jnp.concatenate([kg, bias_t], axis=1)
        t = lax.dot_general(q_ext, k_ext, (((1,), (1,)), ((), ())), preferred_element_type=F32)
        m = jnp.max(t, axis=-1, keepdims=True)
        p = jnp.exp2((t - m) * exp2_scale).astype(BF16)
        v_ext = jnp.concatenate([vg, ones], axis=1)
        o_ext = jnp.dot(p, v_ext, preferred_element_type=F32)
        o = o_ext[:, :HEAD_DIM] / o_ext[:, HEAD_DIM:]
        for r in range(ATTN_GROUP):
            o_ref[rows, (h0 + r) * HEAD_DIM:(h0 + r + 1) * HEAD_DIM] = o[r * tq:(r + 1) * tq].astype(o_ref.dtype)


def _attn_kernel(q_ref, k_ref, v_ref, bias_t_ref, o_ref, *, kend):
    half = pl.program_id(1)
    for which, keys in ((0, kend - HALF_ROWS), (1, kend)):
        @pl.when(half == which)
        def _():
            for row0 in range(0, HALF_ROWS, HEAD_DIM):
                _attn_tile(q_ref, k_ref, v_ref, bias_t_ref, o_ref, row0=row0, keys=keys)


def _attn_kernel_into(q_ref, k_ref, v_ref, bias_t_ref, o_prev_ref, o_ref, *, kend):
    del o_prev_ref
    _attn_kernel(q_ref, k_ref, v_ref, bias_t_ref, o_ref, kend=kend)


def _attention(q3, kv3, bias_t, o_prev, *, j):
    b, s, _ = q3.shape
    kend = Q_SUPER * (j + 1)
    q0 = Q_SUPER * j
    tq = HALF_ROWS
    nq = Q_SUPER // tq
    operands = [q3, kv3, kv3, bias_t]
    in_specs = [pl.BlockSpec((None, tq, ATTN_W), lambda bi, qi: (bi, q0 // tq + qi, 0)),
                pl.BlockSpec((None, kend, KV_W), lambda bi, qi: (bi, 0, 0)),
                pl.BlockSpec((None, kend, KV_W), lambda bi, qi: (bi, 0, 1)),
                pl.BlockSpec((None, kend, tq), lambda bi, qi: (bi, 0, qi))]
    body, aliases = _attn_kernel, {}
    if o_prev is not None:
        operands.append(o_prev)
        in_specs.append(pl.BlockSpec(memory_space=pl.ANY))
        body, aliases = _attn_kernel_into, {4: 0}
    return pl.pallas_call(
        functools.partial(body, kend=kend),
        grid=(b, nq),
        in_specs=in_specs,
        out_specs=pl.BlockSpec((None, tq, ATTN_W), lambda bi, qi: (bi, q0 // tq + qi, 0)),
        out_shape=jax.ShapeDtypeStruct((b, s, ATTN_W), BF16),
        input_output_aliases=aliases,
        compiler_params=_params(2),
        name=f"attention_{j}",
    )(*operands)


def _retention_kernel(q_ref, k_ref, v_ref, gate_ref, decay_ref, xi_ref, zeta_ref, cd_ref, g_ref, b_ref,
                      o_ref, *, n_chunks):
    c = RET_CHUNK
    decay = decay_ref[...]
    xi = xi_ref[...]
    zeta = zeta_ref[...]
    cd = cd_ref[...]
    gn_g = g_ref[...]
    gn_b = b_ref[...]
    state = jnp.zeros((RET_QK_DIM, RET_V_DIM), F32)
    for n in range(n_chunks):
        sl = slice(n * c, (n + 1) * c)
        q = q_ref[sl, :]
        k = k_ref[sl, :]
        v = v_ref[sl, :]
        qk = lax.dot_general(q, k, (((1,), (1,)), ((), ())), preferred_element_type=F32) * decay
        inner = jnp.dot(qk.astype(BF16), v, preferred_element_type=F32)
        cross = jnp.dot(q, state.astype(BF16), preferred_element_type=F32) * xi
        kz_t = (k.astype(F32) * zeta).T.astype(BF16)
        upd = jnp.dot(kz_t, v, preferred_element_type=F32)
        state = upd + cd * state
        y = inner + cross
        mu = jnp.mean(y, axis=-1, keepdims=True)
        d = y - mu
        var = jnp.mean(d * d, axis=-1, keepdims=True)
        yn = d * lax.rsqrt(var + EPS)
        z = yn * gn_g + gn_b
        gate = gate_ref[sl, :].astype(F32)
        silu = gate * (1.0 / (1.0 + jnp.exp(-gate)))
        o_ref[sl, :] = (silu * z).astype(o_ref.dtype)


def _retention_tables():
    c = RET_CHUNK
    lg = np.log(1.0 - np.exp2(-5.0 - np.arange(RET_HEADS, dtype=np.float32))).astype(np.float32)
    pos = np.arange(c, dtype=np.float32)
    diff = pos[:, None] - pos[None, :]
    decay = np.where(diff[None] >= 0, np.exp(lg[:, None, None] * np.maximum(diff, 0.0)[None]), 0.0)
    xi = np.exp(lg[:, None] * (pos[None, :] + 1.0))
    zeta = np.exp(lg[:, None] * (c - 1.0 - pos[None, :]))
    cd = np.exp(lg * c)
    f = lambda a: jnp.asarray(a.astype(np.float32))
    return (f(decay),
            f(np.broadcast_to(xi[:, :, None], (RET_HEADS, c, RET_V_DIM))),
            f(np.broadcast_to(zeta[:, :, None], (RET_HEADS, c, RET_QK_DIM))),
            f(np.broadcast_to(cd[:, None, None], (RET_HEADS, 1, RET_V_DIM))))


def _retention(rot3, plain3, gn_g, gn_b):
    b, s, _ = rot3.shape
    decay, xi, zeta, cd = _retention_tables()
    v_blk0 = 0
    gate_blk0 = RET_V_W // RET_V_DIM
    return pl.pallas_call(
        functools.partial(_retention_kernel, n_chunks=s // RET_CHUNK),
        grid=(b, RET_HEADS),
        in_specs=[pl.BlockSpec((None, s, RET_QK_DIM), lambda bi, h: (bi, 0, h)),
                  pl.BlockSpec((None, s, RET_QK_DIM), lambda bi, h: (bi, 0, RET_HEADS + h)),
                  pl.BlockSpec((None, s, RET_V_DIM), lambda bi, h: (bi, 0, v_blk0 + h)),
                  pl.BlockSpec((None, s, RET_V_DIM), lambda bi, h: (bi, 0, gate_blk0 + h)),
                  pl.BlockSpec((None, RET_CHUNK, RET_CHUNK), lambda bi, h: (h, 0, 0)),
                  pl.BlockSpec((None, RET_CHUNK, RET_V_DIM), lambda bi, h: (h, 0, 0)),
                  pl.BlockSpec((None, RET_CHUNK, RET_QK_DIM), lambda bi, h: (h, 0, 0)),
                  pl.BlockSpec((None, 1, RET_V_DIM), lambda bi, h: (h, 0, 0)),
                  pl.BlockSpec((1, RET_V_DIM), lambda bi, h: (0, h)),
                  pl.BlockSpec((1, RET_V_DIM), lambda bi, h: (0, h))],
        out_specs=pl.BlockSpec((None, s, RET_V_DIM), lambda bi, h: (bi, 0, h)),
        out_shape=jax.ShapeDtypeStruct((b, s, RET_V_W), BF16),
        compiler_params=_params(2),
        name="retention",
    )(rot3, rot3, plain3, plain3, decay, xi, zeta, cd, gn_g.reshape(1, -1), gn_b.reshape(1, -1))


def _merge_kernel(oa_ref, or_ref, wa_ref, wr_ref, ga_ref, gb_ref, o_ref, wa_bf, wr_bf):
    @pl.when(pl.program_id(1) == 0)
    def _():
        wa_bf[...] = wa_ref[...].astype(BF16)
        wr_bf[...] = wr_ref[...].astype(BF16)

    ya = jnp.dot(oa_ref[...], wa_bf[...], preferred_element_type=F32)
    yr = jnp.dot(or_ref[...], wr_bf[...], preferred_element_type=F32)
    ga = ga_ref[...].astype(F32)
    gb = gb_ref[...].astype(F32)
    sa = 1.0 / (1.0 + jnp.exp(-ga))
    sb = 1.0 / (1.0 + jnp.exp(-gb))
    o_ref[...] = (sa * ya + sb * yr).astype(o_ref.dtype)


def _merge(o_attn, o_ret, w_ua3, w_ur3, layer, plain, tm=1024, tn=512):
    m = o_attn.shape[0]
    ga_blk0 = (2 * RET_V_W) // tn
    gb_blk0 = (2 * RET_V_W + D_MODEL) // tn
    return pl.pallas_call(
        _merge_kernel,
        grid=(D_MODEL // tn, m // tm),
        in_specs=[pl.BlockSpec((tm, ATTN_W), lambda j, i: (i, 0)),
                  pl.BlockSpec((tm, RET_V_W), lambda j, i: (i, 0)),
                  pl.BlockSpec((None, ATTN_W, tn), lambda j, i: (layer, 0, j)),
                  pl.BlockSpec((None, RET_V_W, tn), lambda j, i: (layer, 0, j)),
                  pl.BlockSpec((tm, tn), lambda j, i: (i, ga_blk0 + j)),
                  pl.BlockSpec((tm, tn), lambda j, i: (i, gb_blk0 + j))],
        out_specs=pl.BlockSpec((tm, tn), lambda j, i: (i, j)),
        out_shape=jax.ShapeDtypeStruct((m, D_MODEL), BF16),
        scratch_shapes=[pltpu.VMEM((ATTN_W, tn), BF16), pltpu.VMEM((RET_V_W, tn), BF16)],
        compiler_params=_params(2),
        name="gated_merge",
    )(o_attn, o_ret, w_ua3, w_ur3, plain, plain)


def _rotary_tables(s):
    pos = jnp.arange(s, dtype=F32)
    inv_freq = ROPE_BASE ** (-jnp.arange(0, RET_QK_DIM, 2, dtype=F32) / RET_QK_DIM)
    ang = pos[:, None] * inv_freq[None, :]
    cos, sin = jnp.cos(ang), jnp.sin(ang)
    return jnp.concatenate([cos, cos], axis=1), jnp.concatenate([-sin, sin], axis=1)


def kernel(x, ln1_g, w_in, q_norm_g, k_norm_g, ret_gn_g, ret_gn_b, w_up_attn, w_up_ret, w_out, ln2_g,
           w_ff1, w_ff2):
    b, s, d = x.shape
    m = b * s
    depth = w_in.shape[0]
    topk = min(TOPK_MAX, s // 4)
    assert s % Q_SUPER == 0 and topk <= Q_SUPER
    cos2, sin2 = _rotary_tables(s)
    tm = 1024
    s_tiles = s // tm

    w_in_t = jnp.swapaxes(w_in, 1, 2)
    c_kv = ATTN_W
    c_iq = c_kv + 2 * KV_W
    c_rot = c_iq + IDX_Q_W + IDX_DIM + IDX_HEADS
    c_plain = c_rot + 2 * RET_QK_W
    in_proj = functools.partial(_matmul, w3=w_in_t, w_is_nk=True, tm=tm)

    xf = x.reshape(m, d)
    xg, ss = _norm_prep(xf, ln1_g[0])
    for l in range(depth):
        aq =in_proj(xg, layer=l, row_ss=ss, out_dtype=BF16, epilogue=_epi_headnorm, tn=1024, col0=0, n=ATTN_W,
                     name="proj_q",
                     extras=[(q_norm_g[l].reshape(1, HEAD_DIM), (1, HEAD_DIM), lambda i, j: (0, 0))])
        kv = in_proj(xg, layer=l, row_ss=ss, out_dtype=BF16, epilogue=_epi_kv, tn=KV_W, col0=c_kv, n=2 * KV_W,
                     name="proj_kv",
                     extras=[(k_norm_g[l].reshape(1, HEAD_DIM), (1, HEAD_DIM), lambda i, j: (0, 0))])
        idx = in_proj(xg, layer=l, row_ss=ss, out_dtype=F32, epilogue=_epi_plain, tn=IDX_PACK_W, col0=c_iq,
                      n=IDX_PACK_W, name="proj_idx")
        rot = in_proj(xg, layer=l, row_ss=ss, out_dtype=BF16, epilogue=_epi_rotary, tn=RET_QK_W, col0=c_rot,
                      n=2 * RET_QK_W, name="proj_rot",
                      extras=[(cos2, (tm, RET_QK_DIM), lambda i, j: (i % s_tiles, 0)),
                              (sin2, (tm, RET_QK_DIM), lambda i, j: (i % s_tiles, 0))])
        plain = in_proj(xg, layer=l, row_ss=ss, out_dtype=BF16, epilogue=_epi_plain, tn=1024, col0=c_plain,
                        n=PLAIN_W, name="proj_plain")

        aq3 = aq.reshape(b, s, ATTN_W)
        kv3 = kv.reshape(b, s, 2 * KV_W)
        idx3 = idx.reshape(b, s, IDX_PACK_W)
        plain3 = plain.reshape(b, s, PLAIN_W)
        rot3 = rot.reshape(b, s, 2 * RET_QK_W)

        o_attn3 = None
        for j in range(s // Q_SUPER):
            bias = _topk_mask(idx3, j=j, topk=topk)
            o_attn3 = _attention(aq3, kv3, bias, o_attn3, j=j)
        o_attn = o_attn3.reshape(m, ATTN_W)

        o_ret = _retention(rot3, plain3, ret_gn_g[l], ret_gn_b[l]).reshape(m, RET_V_W)

        merged = _merge(o_attn, o_ret, w_up_attn, w_up_ret, l, plain)
        xf, xg2, ss2 = _matmul(merged, w_out, l, out_dtype=F32, epilogue=_epi_residual, tm=tm, tn=1024,
                               name="proj_out", extras=[(xf, (tm, 1024), lambda i, j: (i, j))],
                               norm_gain=ln2_g[l].reshape(1, d), w_buffers=1)

        f = _matmul(xg2, w_ff1, l, out_dtype=BF16, epilogue=_epi_relu2, tm=tm, tn=1024, row_ss=ss2, name="ff1")
        next_gain = ln1_g[l + 1].reshape(1, d) if l + 1 < depth else None
        res = _matmul(f, w_ff2, l, out_dtype=F32, epilogue=_epi_residual, tm=512, tn=1024, tk=2048,
                      name="ff2", extras=[(xf, (512, 1024), lambda i, j: (i, j))], norm_gain=next_gain)
        xf, xg, ss = res if next_gain is not None else (res, None, None)
    return xf.reshape(b, s, d)
```

```python
import functools
import math

import jax
import jax.numpy as jnp
import numpy as np
from jax import lax
from jax.experimental import pallas as pl
from jax.experimental.pallas import tpu as pltpu

D_MODEL = 2048
HEAD_DIM = 128
ATTN_Q_HEADS = 16
ATTN_KV_HEADS = 4
ATTN_GROUP = ATTN_Q_HEADS // ATTN_KV_HEADS
IDX_HEADS = 16
IDX_DIM = 64
TOPK_MAX = 256
RET_HEADS = 8
RET_QK_DIM = 128
RET_V_DIM = 256
RET_CHUNK = 128
ROPE_BASE = 10000.0
D_FF = 4 * D_MODEL
EPS = 1e-6

ATTN_W = ATTN_Q_HEADS * HEAD_DIM
KV_W = ATTN_KV_HEADS * HEAD_DIM
IDX_Q_W = IDX_HEADS * IDX_DIM
RET_QK_W = RET_HEADS * RET_QK_DIM
RET_V_W = RET_HEADS * RET_V_DIM

LANES = 128
PLAIN_W = 2 * RET_V_W + 2 * D_MODEL
IDX_PACK_W = IDX_Q_W + LANES
COUNT_ROWS = 64
BISECT_MAX_STEPS = 2200
BISECT_UNROLL = 4
Q_SUPER = 512
HALF_ROWS = Q_SUPER // 2
NEG_BIG = -1e30
VMEM_LIMIT = 56 * 1024 * 1024

F32 = jnp.float32
BF16 = jnp.bfloat16


def _params(n_axes):
    return pltpu.CompilerParams(dimension_semantics=("arbitrary",) * n_axes,
                                vmem_limit_bytes=VMEM_LIMIT)


def _row_sumsq(x):
    return jnp.broadcast_to(jnp.sum(x * x, axis=-1, keepdims=True), (x.shape[0], LANES))


def _norm_prep_kernel(x_ref, g_ref, xg_ref, ss_ref):
    x = x_ref[...]
    xg_ref[...] = (x * g_ref[...]).astype(xg_ref.dtype)
    ss_ref[...] = _row_sumsq(x)


def _norm_prep(x, g, tm=512):
    m, d = x.shape
    return pl.pallas_call(
        _norm_prep_kernel,
        grid=(m // tm,),
        in_specs=[pl.BlockSpec((tm, d), lambda i: (i, 0)),
                  pl.BlockSpec((1, d), lambda i: (0, 0))],
        out_specs=[pl.BlockSpec((tm, d), lambda i: (i, 0)),
                   pl.BlockSpec((None, tm, LANES), lambda i: (0, i, 0))],
        out_shape=[jax.ShapeDtypeStruct((m, d), BF16), jax.ShapeDtypeStruct((1, m, LANES), F32)],
        compiler_params=_params(1),
        name="norm_prep",
    )(x, g.reshape(1, d))


def _mm_kernel(*refs, nk, n_extra, epilogue, cache_w, w_is_nk, row_scaled, norm_out, side_cast, d_norm):
    a_ref, w_ref = refs[0], refs[1]
    n_fixed = 2 + int(row_scaled) + n_extra
    n_in = n_fixed + int(norm_out) + int(side_cast)
    ss_in_ref = refs[2] if row_scaled else None
    extra = refs[2 + int(row_scaled):n_fixed]
    gain_ref = refs[n_fixed] if norm_out else None
    o_ref = refs[n_in]
    n_out = 1 + 2 * int(norm_out) + int(side_cast)
    scratch = refs[n_in + n_out:]
    i = pl.program_id(1)
    k = pl.program_id(2)
    if side_cast:
        refs[n_in + n_out - 1][...] = refs[n_in - 1][...].astype(BF16)
    if cache_w:
        wbf_ref = scratch[0]

        @pl.when(i == 0)
        def _():
            wbf_ref[k] = (w_ref[0].T if w_is_nk else w_ref[...]).astype(BF16)

        w = wbf_ref[k]
    else:
        w = w_ref[...]
    prod = jnp.dot(a_ref[...], w, preferred_element_type=F32)

    def finish(acc):
        if row_scaled:
            r = lax.rsqrt(jnp.sum(ss_in_ref[...], axis=0) * (1.0 / d_norm) + EPS)
            acc = acc * jnp.concatenate([r] * (acc.shape[1] // LANES), axis=1)
        out = epilogue(acc, *extra)
        o_ref[...] = out.astype(o_ref.dtype)
        if norm_out:
            xg_ref, ss_out_ref = refs[n_in + 1], refs[n_in + 2]
            xg_ref[...] = (out * gain_ref[...]).astype(xg_ref.dtype)
            ss_out_ref[...] = _row_sumsq(out)

    if nk == 1:
        finish(prod)
        return
    acc_ref = scratch[-1]

    @pl.when(k == 0)
    def _():
        acc_ref[...] = prod

    @pl.when(k > 0)
    def _():
        acc_ref[...] += prod

    @pl.when(k == nk - 1)
    def _():
        finish(acc_ref[...])


def _matmul(a, w3, layer, *, out_dtype, epilogue, extras=(), tm=1024, tn=1024, tk=None, col0=0, n=None,
            w_is_nk=False, row_ss=None, norm_gain=None, w_buffers=2, side_cast=None, name="matmul"):
    m, kdim = a.shape
    n_total = w3.shape[1] if w_is_nk else w3.shape[2]
    n = n_total - col0 if n is None else n
    tk = kdim if tk is None else tk
    nk = kdim // tk
    tn = min(tn, n)
    assert m % tm == 0 and n % tn == 0 and kdim % tk == 0
    cache_w = w3.dtype != BF16
    if w_is_nk:
        assert cache_w and nk == 1 and col0 % 8 == 0
        w_spec = pl.BlockSpec((pl.Element(1), pl.Element(tn), pl.Element(tk)),
                              lambda j, i, k: (layer, pl.multiple_of(col0 + j * tn, 8), 0))
    else:
        assert col0 % tn == 0
        jb0 = col0 // tn
        if cache_w:
            w_map = lambda j, i, k: (layer, jnp.where(i == 0, k, nk - 1), jb0 + j)
        else:
            w_map = lambda j, i, k: (layer, k, jb0 + j)
        mode = {} if w_buffers == 2 else {"pipeline_mode": pl.Buffered(w_buffers)}
        w_spec = pl.BlockSpec((None, tk, tn), w_map, **mode)
    operands = [a, w3]
    in_specs = [pl.BlockSpec((tm, tk), lambda j, i, k: (i, k)), w_spec]
    if row_ss is not None:
        operands.append(row_ss)
        in_specs.append(pl.BlockSpec((row_ss.shape[0], tm, LANES), lambda j, i, k: (0, i, 0)))
    for arr, bshape, imap in extras:
        operands.append(arr)
        in_specs.append(pl.BlockSpec(bshape, functools.partial(lambda j, i, k, f: f(i, j), f=imap)))
    out_specs = [pl.BlockSpec((tm, tn), lambda j, i, k: (i, j))]
    out_shape = [jax.ShapeDtypeStruct((m, n), out_dtype)]
    if norm_gain is not None:
        operands.append(norm_gain)
        in_specs.append(pl.BlockSpec((1, tn), lambda j, i, k: (0, j)))
        out_specs += [pl.BlockSpec((tm, tn), lambda j, i, k: (i, j)),
                      pl.BlockSpec((None, tm, LANES), lambda j, i, k: (j, i, 0))]
        out_shape += [jax.ShapeDtypeStruct((m, n), BF16), jax.ShapeDtypeStruct((n // tn, m, LANES), F32)]
    if side_cast is not None:
        other3, other_layer = side_cast
        steps, ni = (n // tn) * (m // tm), m // tm
        _, rows_o, cols_o = other3.shape
        assert nk == 1 and rows_o % steps == 0 and (rows_o // steps) % 16 == 0
        slab = rows_o // steps
        operands.append(other3)
        in_specs.append(pl.BlockSpec((None, slab, cols_o), lambda j, i, k: (other_layer, j * ni + i, 0)))
        out_specs.append(pl.BlockSpec((slab, cols_o), lambda j, i, k: (j * ni + i, 0)))
        out_shape.append(jax.ShapeDtypeStruct((rows_o, cols_o), BF16))
    scratch = []
    if cache_w:
        scratch.append(pltpu.VMEM((nk, tk, tn), BF16))
    if nk > 1:
        scratch.append(pltpu.VMEM((tm, tn), F32))
    res = pl.pallas_call(
        functools.partial(_mm_kernel, nk=nk, n_extra=len(extras), epilogue=epilogue, cache_w=cache_w,
                          w_is_nk=w_is_nk, row_scaled=row_ss is not None, norm_out=norm_gain is not None,
                          side_cast=side_cast is not None, d_norm=kdim),
        grid=(n // tn, m // tm, nk),
        in_specs=in_specs,
        out_specs=out_specs,
        out_shape=out_shape,
        scratch_shapes=scratch,
        compiler_params=_params(3),
        name=name,
    )(*operands)
    return res if len(res) > 1 else res[0]


def _epi_plain(acc):
    return acc


def _epi_headnorm(acc, g_ref):
    g = g_ref[...]
    outs = []
    for h in range(acc.shape[1] // HEAD_DIM):
        a = acc[:, h * HEAD_DIM:(h + 1) * HEAD_DIM]
        r = lax.rsqrt(jnp.mean(a * a, axis=-1, keepdims=True) + EPS)
        outs.append((a * r) * g)
    return jnp.concatenate(outs, axis=1)


def _epi_kv(acc, g_ref):
    return jnp.where(pl.program_id(0) == 0, _epi_headnorm(acc, g_ref), acc)


def _epi_rotary(acc, cos_ref, sin_ref):
    c = cos_ref[...]
    s = sin_ref[...]
    scale = jnp.where(pl.program_id(0) == 0, 1.0, RET_QK_DIM ** -0.5).astype(F32)
    outs = []
    for h in range(acc.shape[1] // RET_QK_DIM):
        a = acc[:, h * RET_QK_DIM:(h + 1) * RET_QK_DIM]
        swapped = pltpu.roll(a, RET_QK_DIM // 2, axis=1)
        outs.append((a * c + swapped * s) * scale)
    return jnp.concatenate(outs, axis=1)


def _epi_residual(acc, x_ref):
    return x_ref[...] + acc


def _epi_relu2(acc):
    r = jnp.maximum(acc, 0.0)
    return r * r


def _topk_tile(iq_ref, qpack_ref, kpack_ref, bias_ref, score_ref, sel_ref, *, qbase, keys, tq, topk):
    rows = COUNT_ROWS
    neg_inf = float("-inf")

    ik = kpack_ref[:keys, :IDX_DIM].astype(BF16)
    w_t = qpack_ref[...].T * (IDX_HEADS ** -0.5)
    score = None
    for h in range(IDX_HEADS):
        iqh = iq_ref[:, h * IDX_DIM:(h + 1) * IDX_DIM].astype(BF16)
        logits = lax.dot_general(ik, iqh, (((1,), (1,)), ((), ())), preferred_element_type=F32)
        term = jnp.maximum(logits, 0.0) * w_t[IDX_DIM + h:IDX_DIM + h + 1, :]
        score = term if score is None else score + term

    def causal_mask():
        kpos = lax.broadcasted_iota(jnp.int32, (keys, tq), 0)
        qcol = lax.broadcasted_iota(jnp.int32, (keys, tq), 1) + qbase
        return kpos <= qcol

    causal = causal_mask()
    score = jnp.where(score == 0.0, 0.0, score)
    lo_fill = jnp.where(causal, score, neg_inf)
    score_ref[:keys, :] = lo_fill
    smax = jnp.max(jnp.max(lo_fill.reshape(keys // rows, rows, tq), axis=0), axis=0, keepdims=True)
    hi_fill = jnp.where(causal, score, float("inf"))
    smin = jnp.min(jnp.min(hi_fill.reshape(keys // rows, rows, tq), axis=0), axis=0, keepdims=True)
    n_causal = jnp.minimum(lax.broadcasted_iota(jnp.int32, (1, tq), 1) + (qbase + 1), keys)

    def count_part(thr, c0, width):
        thr_b = jnp.broadcast_to(thr, (rows, width))
        part = jnp.zeros((rows, width), F32)
        for i in range(keys // rows):
            part = part + jnp.where(score_ref[i * rows:(i + 1) * rows, c0:c0 + width] >= thr_b, 1.0, 0.0)
        return part

    kf = float(topk)
    hw = tq // 2

    def step(part, state, first):
        lo, hi, tau, act, mid, odd = state
        cnt = jnp.sum(part, axis=0, keepdims=True)
        ge = cnt >= kf
        hit = (cnt == kf) | (first & ge)
        lo = jnp.where(ge, mid, lo)
        hi = jnp.where(ge, hi, mid)
        nxt = 0.5 * lo + 0.5 * hi
        stuck = (nxt <= lo) | (nxt >= hi)
        live = act > 0
        tau = jnp.where(live & hit, mid, jnp.where(live & stuck, lo, tau))
        inexact = (hit & (cnt != kf)) | (jnp.logical_not(hit) & stuck)
        odd = jnp.where(live & inexact, 1, odd)
        act = jnp.where(hit | stuck, 0, act)
        return lo, hi, tau, act, nxt, odd

    def cond(c):
        return (c[1] > 0) & (c[0] < BISECT_MAX_STEPS)

    def body(c):
        it, _, part_a, st_a, st_b = c
        for u in range(BISECT_UNROLL):
            first = ((jnp.zeros((1, hw), jnp.int32) + it) == 0) if u == 0 else jnp.zeros((1, hw), jnp.bool_)
            st_a = step(part_a, st_a, first)
            part_b = count_part(st_b[4], hw, hw)
            st_b = step(part_b, st_b, first)
            part_a = count_part(st_a[4], 0, hw)
        return it + BISECT_UNROLL, jnp.sum(st_a[3]) + jnp.sum(st_b[3]), part_a, st_a, st_b

    lo0 = jnp.where(n_causal >= topk, smin, neg_inf)

    def init_state(c0):
        return (lo0[:, c0:c0 + hw], smax[:, c0:c0 + hw], jnp.full((1, hw), neg_inf, F32),
                jnp.ones((1, hw), jnp.int32), smax[:, c0:c0 + hw], jnp.zeros((1, hw), jnp.int32))

    st_a0, st_b0 = init_state(0), init_state(hw)
    out = lax.while_loop(cond, body, (jnp.int32(0), jnp.int32(tq), count_part(st_a0[4], 0, hw), st_a0, st_b0))
    tau = jnp.concatenate([out[3][2], out[4][2]], axis=1)

    odd = jnp.concatenate([out[3][5], out[4][5]], axis=1)

    picked = jnp.where(score_ref[:keys, :] >= tau, 0.0, NEG_BIG)
    bias_ref[:keys, :] = jnp.where(causal_mask(), picked, NEG_BIG).astype(bias_ref.dtype)

    @pl.when(jnp.max(jnp.where(tau > neg_inf, odd, 0)) > 0)
    def _():
        tc = 256
        row = lax.broadcasted_iota(jnp.int32, (tc, tc), 0)
        col = lax.broadcasted_iota(jnp.int32, (tc, tc), 1)
        lower = jnp.where(col <= row, 1.0, 0.0).astype(BF16)
        n_gt = jnp.sum(jnp.where(score_ref[:keys, :] > tau, 1.0, 0.0), axis=0, keepdims=True)
        need = kf - n_gt
        carry = jnp.zeros((1, tq), F32)
        for c in range(keys // tc):
            sc = score_ref[c * tc:(c + 1) * tc, :]
            eq = jnp.where(sc == tau, 1.0, 0.0)
            prefix = jnp.dot(lower, eq.astype(BF16), preferred_element_type=F32) + carry
            carry = carry + jnp.sum(eq, axis=0, keepdims=True)
            keep = (sc > tau) | ((sc == tau) & (prefix <= need))
            sel_ref[c * tc:(c + 1) * tc, :] = jnp.where(keep, 0.0, NEG_BIG)
        bias_ref[:keys, :] = jnp.where(causal_mask(), sel_ref[:keys, :], NEG_BIG).astype(bias_ref.dtype)


def _topk_mask_kernel(iq_ref, qpack_ref, kpack_ref, bias_ref, score_ref, sel_ref, *, q0, kend, tq, topk):
    half = pl.program_id(1)
    for which, keys in ((0, kend - tq), (1, kend)):
        @pl.when(half == which)
        def _():
            _topk_tile(iq_ref, qpack_ref, kpack_ref, bias_ref, score_ref, sel_ref,
                       qbase=q0 + which * tq, keys=keys, tq=tq, topk=topk)
            if keys < kend:
                bias_ref[keys:, :] = jnp.full((kend - keys, tq), NEG_BIG, bias_ref.dtype)


def _topk_mask(idx3, *, j, topk):
    b = idx3.shape[0]
    kend = Q_SUPER * (j + 1)
    q0 = Q_SUPER * j
    tq = HALF_ROWS
    nq = Q_SUPER // tq
    pack_blk = IDX_Q_W // LANES
    return pl.pallas_call(
        functools.partial(_topk_mask_kernel, q0=q0, kend=kend, tq=tq, topk=topk),
        grid=(b, nq),
        in_specs=[pl.BlockSpec((None, tq, IDX_Q_W), lambda bi, qi: (bi, q0 // tq + qi, 0)),
                  pl.BlockSpec((None, tq, LANES), lambda bi, qi: (bi, q0 // tq + qi, pack_blk)),
                  pl.BlockSpec((None, kend, LANES), lambda bi, qi: (bi, 0, pack_blk))],
        out_specs=pl.BlockSpec((None, kend, tq), lambda bi, qi: (bi, 0, qi)),
        out_shape=jax.ShapeDtypeStruct((b, kend, Q_SUPER), BF16),
        scratch_shapes=[pltpu.VMEM((kend, tq), F32), pltpu.VMEM((kend, tq), F32)],
        compiler_params=_params(2),
        name=f"topk_mask_{j}",
    )(idx3, idx3, idx3)


def _attn_tile(q_ref, k_ref, v_ref, bias_t_ref, o_ref, *, row0, keys):
    tq = HEAD_DIM
    rows = slice(row0, row0 + tq)
    exp2_scale = (HEAD_DIM ** -0.5) * math.log2(math.e)
    bias_t = bias_t_ref[:keys, rows]
    row = lax.broadcasted_iota(jnp.int32, (tq, tq), 0)
    col = lax.broadcasted_iota(jnp.int32, (tq, tq), 1)
    eye = jnp.where(row == col, 1.0, 0.0).astype(BF16)
    eye_rows = jnp.concatenate([eye] * ATTN_GROUP, axis=0)
    ones = jnp.ones((keys, HEAD_DIM), BF16)
    for g in range(ATTN_KV_HEADS):
        kg = k_ref[:keys, g * HEAD_DIM:(g + 1) * HEAD_DIM]
        vg = v_ref[:keys, g * HEAD_DIM:(g + 1) * HEAD_DIM]
        h0 = g * ATTN_GROUP
        qg = jnp.concatenate([q_ref[rows, (h0 + r) * HEAD_DIM:(h0 + r + 1) * HEAD_DIM]
                              for r in range(ATTN_GROUP)], axis=0)
        q_ext = jnp.concatenate([qg, eye_rows], axis=1)
        k_ext = jnp.concatenate([kg, bias_t], axis=1)
        t = lax.dot_general(q_ext, k_ext, (((1,), (1,)), ((), ())), preferred_element_type=F32)
        m = jnp.max(t, axis=-1, keepdims=True)
        p = jnp.exp2((t - m) * exp2_scale).astype(BF16)
        v_ext = jnp.concatenate([vg, ones], axis=1)
        o_ext = jnp.dot(p, v_ext, preferred_element_type=F32)
        o = o_ext[:, :HEAD_DIM] / o_ext[:, HEAD_DIM:]
        for r in range(ATTN_GROUP):
            o_ref[rows, (h0 + r) * HEAD_DIM:(h0 + r + 1) * HEAD_DIM] = o[r * tq:(r + 1) * tq].astype(o_ref.dtype)


def _attn_kernel(q_ref, k_ref, v_ref, bias_t_ref, o_ref, *, kend):
    half = pl.program_id(1)
    for which, keys in ((0, kend - HALF_ROWS), (1, kend)):
        @pl.when(half == which)
        def _():
            for row0 in range(0, HALF_ROWS, HEAD_DIM):
                _attn_tile(q_ref, k_ref, v_ref, bias_t_ref, o_ref, row0=row0, keys=keys)


def _attn_kernel_into(q_ref, k_ref, v_ref, bias_t_ref, o_prev_ref, o_ref, *, kend):
    del o_prev_ref
    _attn_kernel(q_ref, k_ref, v_ref, bias_t_ref, o_ref, kend=kend)


def _attention(q3, kv3, bias_t, o_prev, *, j):
    b, s, _ = q3.shape
    kend = Q_SUPER * (j + 1)
    q0 = Q_SUPER * j
    tq = HALF_ROWS
    nq = Q_SUPER // tq
    return pl.pallas_call(
        functools.partial(_attn_kernel_into, kend=kend),
        grid=(b, nq),
        in_specs=[pl.BlockSpec((None, tq, ATTN_W), lambda bi, qi: (bi, q0 // tq + qi, 0)),
                  pl.BlockSpec((None, kend, KV_W), lambda bi, qi: (bi, 0, 0)),
                  pl.BlockSpec((None, kend, KV_W), lambda bi, qi: (bi, 0, 1)),
                  pl.BlockSpec((None, kend, tq), lambda bi, qi: (bi, 0, qi)),
                  pl.BlockSpec(memory_space=pl.ANY)],
        out_specs=pl.BlockSpec((None, tq, ATTN_W), lambda bi, qi: (bi, q0 // tq + qi, 0)),
        out_shape=jax.ShapeDtypeStruct((b, s, ATTN_W), BF16),
        input_output_aliases={4: 0},
        compiler_params=_params(2),
        name=f"attention_{j}",
    )(q3, kv3, kv3, bias_t, o_prev)


def _retention_kernel(q_ref, k_ref, v_ref, gate_ref, decay_ref, xi_ref, zeta_ref, cd_ref, g_ref, b_ref,
                      o_ref, *, n_chunks):
    c = RET_CHUNK
    decay = decay_ref[...]
    xi = xi_ref[...]
    zeta = zeta_ref[...]
    cd = cd_ref[...]
    gn_g = g_ref[...]
    gn_b = b_ref[...]
    state = jnp.zeros((RET_QK_DIM, RET_V_DIM), F32)
    for n in range(n_chunks):
        sl = slice(n * c, (n + 1) * c)
        q = q_ref[sl, :]
        k = k_ref[sl, :]
        v = v_ref[sl, :]
        qk = lax.dot_general(q, k, (((1,), (1,)), ((), ())), preferred_element_type=F32) * decay
        inner = jnp.dot(qk.astype(BF16), v, preferred_element_type=F32)
        cross = jnp.dot(q, state.astype(BF16), preferred_element_type=F32) * xi
        kz_t = (k.astype(F32) * zeta).T.astype(BF16)
        upd = jnp.dot(kz_t, v, preferred_element_type=F32)
        state = upd + cd * state
        y = inner + cross
        mu = jnp.mean(y, axis=-1, keepdims=True)
        d = y - mu
        var = jnp.mean(d * d, axis=-1, keepdims=True)
        yn = d * lax.rsqrt(var + EPS)
        z = yn * gn_g + gn_b
        gate = gate_ref[sl, :].astype(F32)
        silu = gate * (1.0 / (1.0 + jnp.exp(-gate)))
        o_ref[sl, :] = (silu * z).astype(o_ref.dtype)


def _retention_tables():
    c = RET_CHUNK
    lg = np.log(1.0 - np.exp2(-5.0 - np.arange(RET_HEADS, dtype=np.float32))).astype(np.float32)
    pos = np.arange(c, dtype=np.float32)
    diff = pos[:, None] - pos[None, :]
    decay = np.where(diff[None] >= 0, np.exp(lg[:, None, None] * np.maximum(diff, 0.0)[None]), 0.0)
    xi = np.exp(lg[:, None] * (pos[None, :] + 1.0))
    zeta = np.exp(lg[:, None] * (c - 1.0 - pos[None, :]))
    cd = np.exp(lg * c)
    f = lambda a: jnp.asarray(a.astype(np.float32))
    return (f(decay),
            f(np.broadcast_to(xi[:, :, None], (RET_HEADS, c, RET_V_DIM))),
            f(np.broadcast_to(zeta[:, :, None], (RET_HEADS, c, RET_QK_DIM))),
            f(np.broadcast_to(cd[:, None, None], (RET_HEADS, 1, RET_V_DIM))))


def _retention(rot3, plain3, gn_g, gn_b):
    b, s, _ = rot3.shape
    decay, xi, zeta, cd = _retention_tables()
    v_blk0 = 0
    gate_blk0 = RET_V_W // RET_V_DIM
    return pl.pallas_call(
        functools.partial(_retention_kernel, n_chunks=s // RET_CHUNK),
        grid=(b, RET_HEADS),
        in_specs=[pl.BlockSpec((None, s, RET_QK_DIM), lambda bi, h: (bi, 0, h)),
                  pl.BlockSpec((None, s, RET_QK_DIM), lambda bi, h: (bi, 0, RET_HEADS + h)),
                  pl.BlockSpec((None, s, RET_V_DIM), lambda bi, h: (bi, 0, v_blk0 + h)),
                  pl.BlockSpec((None, s, RET_V_DIM), lambda bi, h: (bi, 0, gate_blk0 + h)),
                  pl.BlockSpec((None, RET_CHUNK, RET_CHUNK), lambda bi, h: (h, 0, 0)),
                  pl.BlockSpec((None, RET_CHUNK, RET_V_DIM), lambda bi, h: (h, 0, 0)),
                  pl.BlockSpec((None, RET_CHUNK, RET_QK_DIM), lambda bi, h: (h, 0, 0)),
                  pl.BlockSpec((None, 1, RET_V_DIM), lambda bi, h: (h, 0, 0)),
                  pl.BlockSpec((1, RET_V_DIM), lambda bi, h: (0, h)),
                  pl.BlockSpec((1, RET_V_DIM), lambda bi, h: (0, h))],
        out_specs=pl.BlockSpec((None, s, RET_V_DIM), lambda bi, h: (bi, 0, h)),
        out_shape=jax.ShapeDtypeStruct((b, s, RET_V_W), BF16),
        compiler_params=_params(2),
        name="retention",
    )(rot3, rot3, plain3, plain3, decay, xi, zeta, cd, gn_g.reshape(1, -1), gn_b.reshape(1, -1))


def _merge_kernel(oa_ref, or_ref, wa_ref, wr_ref, ga_ref, gb_ref, o_ref, wa_bf, wr_bf):
    @pl.when(pl.program_id(1) == 0)
    def _():
        wa_bf[...] = wa_ref[...].astype(BF16)
        wr_bf[...] = wr_ref[...].astype(BF16)

    ya = jnp.dot(oa_ref[...], wa_bf[...], preferred_element_type=F32)
    yr = jnp.dot(or_ref[...], wr_bf[...], preferred_element_type=F32)
    ga = ga_ref[...].astype(F32)
    gb = gb_ref[...].astype(F32)
    sa = 1.0 / (1.0 + jnp.exp(-ga))
    sb = 1.0 / (1.0 + jnp.exp(-gb))
    o_ref[...] = (sa * ya + sb * yr).astype(o_ref.dtype)


def _merge(o_attn, o_ret, w_ua3, w_ur3, layer, plain, tm=1024, tn=512):
    m = o_attn.shape[0]
    ga_blk0 = (2 * RET_V_W) // tn
    gb_blk0 = (2 * RET_V_W + D_MODEL) // tn
    return pl.pallas_call(
        _merge_kernel,
        grid=(D_MODEL // tn, m // tm),
        in_specs=[pl.BlockSpec((tm, ATTN_W), lambda j, i: (i, 0)),
                  pl.BlockSpec((tm, RET_V_W), lambda j, i: (i, 0)),
                  pl.BlockSpec((None, ATTN_W, tn), lambda j, i: (layer, 0, j)),
                  pl.BlockSpec((None, RET_V_W, tn), lambda j, i: (layer, 0, j)),
                  pl.BlockSpec((tm, tn), lambda j, i: (i, ga_blk0 + j)),
                  pl.BlockSpec((tm, tn), lambda j, i: (i, gb_blk0 + j))],
        out_specs=pl.BlockSpec((tm, tn), lambda j, i: (i, j)),
        out_shape=jax.ShapeDtypeStruct((m, D_MODEL), BF16),
        scratch_shapes=[pltpu.VMEM((ATTN_W, tn), BF16), pltpu.VMEM((RET_V_W, tn), BF16)],
        compiler_params=_params(2),
        name="gated_merge",
    )(o_attn, o_ret, w_ua3, w_ur3, plain, plain)


def _rotary_tables(s):
    pos = jnp.arange(s, dtype=F32)
    inv_freq = ROPE_BASE ** (-jnp.arange(0, RET_QK_DIM, 2, dtype=F32) / RET_QK_DIM)
    ang = pos[:, None] * inv_freq[None, :]
    cos, sin = jnp.cos(ang), jnp.sin(ang)
    return jnp.concatenate([cos, cos], axis=1), jnp.concatenate([-sin, sin], axis=1)


def kernel(x, ln1_g, w_in, q_norm_g, k_norm_g, ret_gn_g, ret_gn_b, w_up_attn, w_up_ret, w_out, ln2_g,
           w_ff1, w_ff2):
    b, s, d = x.shape
    m = b * s
    depth = w_in.shape[0]
    topk = min(TOPK_MAX, s // 4)
    assert s % Q_SUPER == 0 and topk <= Q_SUPER
    cos2, sin2 = _rotary_tables(s)
    tm = 1024
    s_tiles = s // tm

    w_in_t = jnp.swapaxes(w_in, 1, 2)
    c_kv = ATTN_W
    c_iq = c_kv + 2 * KV_W
    c_rot = c_iq + IDX_Q_W + IDX_DIM + IDX_HEADS
    c_plain = c_rot + 2 * RET_QK_W
    in_proj = functools.partial(_matmul, w3=w_in_t, w_is_nk=True, tm=tm)

    xf = x.reshape(m, d)
    xg, ss = _norm_prep(xf, ln1_g[0])
    for l in range(depth):
        aq =in_proj(xg, layer=l, row_ss=ss, out_dtype=BF16, epilogue=_epi_headnorm, tn=1024, col0=0, n=ATTN_W,
                     name="proj_q",
                     extras=[(q_norm_g[l].reshape(1, HEAD_DIM), (1, HEAD_DIM), lambda i, j: (0, 0))])
        kv = in_proj(xg, layer=l, row_ss=ss, out_dtype=BF16, epilogue=_epi_kv, tn=KV_W, col0=c_kv, n=2 * KV_W,
                     name="proj_kv",
                     extras=[(k_norm_g[l].reshape(1, HEAD_DIM), (1, HEAD_DIM), lambda i, j: (0, 0))])
        idx = in_proj(xg, layer=l, row_ss=ss, out_dtype=F32, epilogue=_epi_plain, tn=IDX_PACK_W, col0=c_iq,
                      n=IDX_PACK_W, name="proj_idx")
        rot = in_proj(xg, layer=l, row_ss=ss, out_dtype=BF16, epilogue=_epi_rotary, tn=RET_QK_W, col0=c_rot,
                      n=2 * RET_QK_W, name="proj_rot",
                      extras=[(cos2, (tm, RET_QK_DIM), lambda i, j: (i % s_tiles, 0)),
                              (sin2, (tm, RET_QK_DIM), lambda i, j: (i % s_tiles, 0))])
        plain = in_proj(xg, layer=l, row_ss=ss, out_dtype=BF16, epilogue=_epi_plain, tn=1024, col0=c_plain,
                        n=PLAIN_W, name="proj_plain")

        aq3 = aq.reshape(b, s, ATTN_W)
        kv3 = kv.reshape(b, s, 2 * KV_W)
        idx3 = idx.reshape(b, s, IDX_PACK_W)
        plain3 = plain.reshape(b, s, PLAIN_W)
        rot3 = rot.reshape(b, s, 2 * RET_QK_W)

        o_attn3 = jnp.zeros((b, s, ATTN_W), BF16)
        for j in range(s // Q_SUPER):
            bias = _topk_mask(idx3, j=j, topk=topk)
            o_attn3 = _attention(aq3, kv3, bias, o_attn3, j=j)
        o_attn = o_attn3.reshape(m, ATTN_W)

        o_ret = _retention(rot3, plain3, ret_gn_g[l], ret_gn_b[l]).reshape(m, RET_V_W)

        merged = _merge(o_attn, o_ret, w_up_attn, w_up_ret, l, plain)
        xf, xg2, ss2 = _matmul(merged, w_out, l, out_dtype=F32, epilogue=_epi_residual, tm=tm, tn=1024,
                               name="proj_out", extras=[(xf, (tm, 1024), lambda i, j: (i, j))],
                               norm_gain=ln2_g[l].reshape(1, d), w_buffers=1)

        f, w_ff2_bf = _matmul(xg2, w_ff1, l, out_dtype=BF16, epilogue=_epi_relu2, tm=tm, tn=1024, row_ss=ss2,
                              side_cast=(w_ff2, l), name="ff1")
        next_gain = ln1_g[l + 1].reshape(1, d) if l + 1 < depth else None
        res = _matmul(f, w_ff2_bf[None], 0, out_dtype=F32, epilogue=_epi_residual, tm=tm, tn=1024, tk=2048,
                      name="ff2", extras=[(xf, (tm, 1024), lambda i, j: (i, j))], norm_gain=next_gain)
        xf, xg, ss = res if next_gain is not None else (res, None, None)
    return xf.reshape(b, s, d)
```

```python
import functools
import math

import jax
import jax.numpy as jnp
import numpy as np
from jax import lax
from jax.experimental import pallas as pl
from jax.experimental.pallas import tpu as pltpu

D_MODEL = 2048
HEAD_DIM = 128
ATTN_Q_HEADS = 16
ATTN_KV_HEADS = 4
ATTN_GROUP = ATTN_Q_HEADS // ATTN_KV_HEADS
IDX_HEADS = 16
IDX_DIM = 64
TOPK_MAX = 256
RET_HEADS = 8
RET_QK_DIM = 128
RET_V_DIM = 256
RET_CHUNK = 128
ROPE_BASE = 10000.0
D_FF = 4 * D_MODEL
EPS = 1e-6

ATTN_W = ATTN_Q_HEADS * HEAD_DIM
KV_W = ATTN_KV_HEADS * HEAD_DIM
IDX_Q_W = IDX_HEADS * IDX_DIM
RET_QK_W = RET_HEADS * RET_QK_DIM
RET_V_W = RET_HEADS * RET_V_DIM

LANES = 128
PLAIN_W = 2 * RET_V_W + 2 * D_MODEL
IDX_PACK_W = IDX_Q_W + LANES
COUNT_ROWS = 64
BISECT_MAX_STEPS = 2200
BISECT_UNROLL = 4
Q_SUPER = 512
HALF_ROWS = Q_SUPER // 2
NEG_BIG = -1e30
VMEM_LIMIT = 56 * 1024 * 1024

F32 = jnp.float32
BF16 = jnp.bfloat16


def _params(n_axes):
    return pltpu.CompilerParams(dimension_semantics=("arbitrary",) * n_axes,
                                vmem_limit_bytes=VMEM_LIMIT)


def _row_sumsq(x):
    return jnp.broadcast_to(jnp.sum(x * x, axis=-1, keepdims=True), (x.shape[0], LANES))


def _norm_prep_kernel(x_ref, g_ref, xg_ref, ss_ref):
    x = x_ref[...]
    xg_ref[...] = (x * g_ref[...]).astype(xg_ref.dtype)
    ss_ref[...] = _row_sumsq(x)


def _norm_prep(x, g, tm=512):
    m, d = x.shape
    return pl.pallas_call(
        _norm_prep_kernel,
        grid=(m // tm,),
        in_specs=[pl.BlockSpec((tm, d), lambda i: (i, 0)),
                  pl.BlockSpec((1, d), lambda i: (0, 0))],
        out_specs=[pl.BlockSpec((tm, d), lambda i: (i, 0)),
                   pl.BlockSpec((None, tm, LANES), lambda i: (0, i, 0))],
        out_shape=[jax.ShapeDtypeStruct((m, d), BF16), jax.ShapeDtypeStruct((1, m, LANES), F32)],
        compiler_params=_params(1),
        name="norm_prep",
    )(x, g.reshape(1, d))


def _mm_kernel(*refs, nk, n_extra, epilogue, cache_w, w_is_nk, row_scaled, norm_out, side_cast, d_norm):
    a_ref, w_ref = refs[0], refs[1]
    n_fixed = 2 + int(row_scaled) + n_extra
    n_in = n_fixed + int(norm_out) + int(side_cast)
    ss_in_ref = refs[2] if row_scaled else None
    extra = refs[2 + int(row_scaled):n_fixed]
    gain_ref = refs[n_fixed] if norm_out else None
    o_ref = refs[n_in]
    n_out = 1 + 2 * int(norm_out) + int(side_cast)
    scratch = refs[n_in + n_out:]
    i = pl.program_id(1)
    k = pl.program_id(2)
    if side_cast:
        refs[n_in + n_out - 1][...] = refs[n_in - 1][...].astype(BF16)
    if cache_w:
        wbf_ref = scratch[0]

        @pl.when(i == 0)
        def _():
            wbf_ref[k] = (w_ref[0].T if w_is_nk else w_ref[...]).astype(BF16)

        w = wbf_ref[k]
    else:
        w = w_ref[...]
    prod = jnp.dot(a_ref[...], w, preferred_element_type=F32)

    def finish(acc):
        if row_scaled:
            r = lax.rsqrt(jnp.sum(ss_in_ref[...], axis=0) * (1.0 / d_norm) + EPS)
            acc = acc * jnp.concatenate([r] * (acc.shape[1] // LANES), axis=1)
        out = epilogue(acc, *extra)
        o_ref[...] = out.astype(o_ref.dtype)
        if norm_out:
            xg_ref, ss_out_ref = refs[n_in + 1], refs[n_in + 2]
            xg_ref[...] = (out * gain_ref[...]).astype(xg_ref.dtype)
            ss_out_ref[...] = _row_sumsq(out)

    if nk == 1:
        finish(prod)
        return
    acc_ref = scratch[-1]

    @pl.when(k == 0)
    def _():
        acc_ref[...] = prod

    @pl.when(k > 0)
    def _():
        acc_ref[...] += prod

    @pl.when(k == nk - 1)
    def _():
        finish(acc_ref[...])


def _matmul(a, w3, layer, *, out_dtype, epilogue, extras=(), tm=1024, tn=1024, tk=None, col0=0, n=None,
            w_is_nk=False, row_ss=None, norm_gain=None, w_buffers=2, side_cast=None, name="matmul"):
    m, kdim = a.shape
    n_total = w3.shape[1] if w_is_nk else w3.shape[2]
    n = n_total - col0 if n is None else n
    tk = kdim if tk is None else tk
    nk = kdim // tk
    tn = min(tn, n)
    assert m % tm == 0 and n % tn == 0 and kdim % tk == 0
    cache_w = w3.dtype != BF16
    if w_is_nk:
        assert cache_w and nk == 1 and col0 % 8 == 0
        w_spec = pl.BlockSpec((pl.Element(1), pl.Element(tn), pl.Element(tk)),
                              lambda j, i, k: (layer, pl.multiple_of(col0 + j * tn, 8), 0))
    else:
        assert col0 % tn == 0
        jb0 = col0 // tn
        if cache_w:
            w_map = lambda j, i, k: (layer, jnp.where(i == 0, k, nk - 1), jb0 + j)
        else:
            w_map = lambda j, i, k: (layer, k, jb0 + j)
        mode = {} if w_buffers == 2 else {"pipeline_mode": pl.Buffered(w_buffers)}
        w_spec = pl.BlockSpec((None, tk, tn), w_map, **mode)
    operands = [a, w3]
    in_specs = [pl.BlockSpec((tm, tk), lambda j, i, k: (i, k)), w_spec]
    if row_ss is not None:
        operands.append(row_ss)
        in_specs.append(pl.BlockSpec((row_ss.shape[0], tm, LANES), lambda j, i, k: (0, i, 0)))
    for arr, bshape, imap in extras:
        operands.append(arr)
        in_specs.append(pl.BlockSpec(bshape, functools.partial(lambda j, i, k, f: f(i, j), f=imap)))
    out_specs = [pl.BlockSpec((tm, tn), lambda j, i, k: (i, j))]
    out_shape = [jax.ShapeDtypeStruct((m, n), out_dtype)]
    if norm_gain is not None:
        operands.append(norm_gain)
        in_specs.append(pl.BlockSpec((1, tn), lambda j, i, k: (0, j)))
        out_specs += [pl.BlockSpec((tm, tn), lambda j, i, k: (i, j)),
                      pl.BlockSpec((None, tm, LANES), lambda j, i, k: (j, i, 0))]
        out_shape += [jax.ShapeDtypeStruct((m, n), BF16), jax.ShapeDtypeStruct((n // tn, m, LANES), F32)]
    if side_cast is not None:
        other3, other_layer = side_cast
        steps, ni = (n // tn) * (m // tm), m // tm
        _, rows_o, cols_o = other3.shape
        assert nk == 1 and rows_o % steps == 0 and (rows_o // steps) % 16 == 0
        slab = rows_o // steps
        operands.append(other3)
        in_specs.append(pl.BlockSpec((None, slab, cols_o), lambda j, i, k: (other_layer, j * ni + i, 0)))
        out_specs.append(pl.BlockSpec((slab, cols_o), lambda j, i, k: (j * ni + i, 0)))
        out_shape.append(jax.ShapeDtypeStruct((rows_o, cols_o), BF16))
    scratch = []
    if cache_w:
        scratch.append(pltpu.VMEM((nk, tk, tn), BF16))
    if nk > 1:
        scratch.append(pltpu.VMEM((tm, tn), F32))
    res = pl.pallas_call(
        functools.partial(_mm_kernel, nk=nk, n_extra=len(extras), epilogue=epilogue, cache_w=cache_w,
                          w_is_nk=w_is_nk, row_scaled=row_ss is not None, norm_out=norm_gain is not None,
                          side_cast=side_cast is not None, d_norm=kdim),
        grid=(n // tn, m // tm, nk),
        in_specs=in_specs,
        out_specs=out_specs,
        out_shape=out_shape,
        scratch_shapes=scratch,
        compiler_params=_params(3),
        name=name,
    )(*operands)
    return res if len(res) > 1 else res[0]


def _epi_plain(acc):
    return acc


def _epi_headnorm(acc, g_ref):
    g = g_ref[...]
    outs = []
    for h in range(acc.shape[1] // HEAD_DIM):
        a = acc[:, h * HEAD_DIM:(h + 1) * HEAD_DIM]
        r = lax.rsqrt(jnp.mean(a * a, axis=-1, keepdims=True) + EPS)
        outs.append((a * r) * g)
    return jnp.concatenate(outs, axis=1)


def _epi_kv(acc, g_ref):
    return jnp.where(pl.program_id(0) == 0, _epi_headnorm(acc, g_ref), acc)


def _epi_rotary(acc, cos_ref, sin_ref):
    c = cos_ref[...]
    s = sin_ref[...]
    scale = jnp.where(pl.program_id(0) == 0, 1.0, RET_QK_DIM ** -0.5).astype(F32)
    outs = []
    for h in range(acc.shape[1] // RET_QK_DIM):
        a = acc[:, h * RET_QK_DIM:(h + 1) * RET_QK_DIM]
        swapped = pltpu.roll(a, RET_QK_DIM // 2, axis=1)
        outs.append((a * c + swapped * s) * scale)
    return jnp.concatenate(outs, axis=1)


def _epi_residual(acc, x_ref):
    return x_ref[...] + acc


def _epi_relu2(acc):
    r = jnp.maximum(acc, 0.0)
    return r * r


def _topk_tile(iq_ref, qpack_ref, kpack_ref, bias_ref, score_ref, sel_ref, *, qbase, keys, tq, topk):
    rows = COUNT_ROWS
    neg_inf = float("-inf")

    ik = kpack_ref[:keys, :IDX_DIM].astype(BF16)
    w_t = qpack_ref[...].T * (IDX_HEADS ** -0.5)
    score = None
    for h in range(IDX_HEADS):
        iqh = iq_ref[:, h * IDX_DIM:(h + 1) * IDX_DIM].astype(BF16)
        logits = lax.dot_general(ik, iqh, (((1,), (1,)), ((), ())), preferred_element_type=F32)
        term = jnp.maximum(logits, 0.0) * w_t[IDX_DIM + h:IDX_DIM + h + 1, :]
        score = term if score is None else score + term

    def causal_mask():
        kpos = lax.broadcasted_iota(jnp.int32, (keys, tq), 0)
        qcol = lax.broadcasted_iota(jnp.int32, (keys, tq), 1) + qbase
        return kpos <= qcol

    causal = causal_mask()
    score = jnp.where(score == 0.0, 0.0, score)
    lo_fill = jnp.where(causal, score, neg_inf)
    score_ref[:keys, :] = lo_fill
    smax = jnp.max(jnp.max(lo_fill.reshape(keys // rows, rows, tq), axis=0), axis=0, keepdims=True)
    hi_fill = jnp.where(causal, score, float("inf"))
    smin = jnp.min(jnp.min(hi_fill.reshape(keys // rows, rows, tq), axis=0), axis=0, keepdims=True)
    n_causal = jnp.minimum(lax.broadcasted_iota(jnp.int32, (1, tq), 1) + (qbase + 1), keys)

    def count_part(thr, c0, width):
        thr_b = jnp.broadcast_to(thr, (rows, width))
        part = jnp.zeros((rows, width), F32)
        for i in range(keys // rows):
            part = part + jnp.where(score_ref[i * rows:(i + 1) * rows, c0:c0 + width] >= thr_b, 1.0, 0.0)
        return part

    kf = float(topk)
    hw = tq // 2

    def step(part, state, first):
        lo, hi, tau, act, mid, odd = state
        cnt = jnp.sum(part, axis=0, keepdims=True)
        ge = cnt >= kf
        hit = (cnt == kf) | (first & ge)
        lo = jnp.where(ge, mid, lo)
        hi = jnp.where(ge, hi, mid)
        nxt = 0.5 * lo + 0.5 * hi
        stuck = (nxt <= lo) | (nxt >= hi)
        live = act > 0
        tau = jnp.where(live & hit, mid, jnp.where(live & stuck, lo, tau))
        inexact = (hit & (cnt != kf)) | (jnp.logical_not(hit) & stuck)
        odd = jnp.where(live & inexact, 1, odd)
        act = jnp.where(hit | stuck, 0, act)
        return lo, hi, tau, act, nxt, odd

    def cond(c):
        return (c[1] > 0) & (c[0] < BISECT_MAX_STEPS)

    def body(c):
        it, _, part_a, st_a, st_b = c
        for u in range(BISECT_UNROLL):
            first = ((jnp.zeros((1, hw), jnp.int32) + it) == 0) if u == 0 else jnp.zeros((1, hw), jnp.bool_)
            st_a = step(part_a, st_a, first)
            part_b = count_part(st_b[4], hw, hw)
            st_b = step(part_b, st_b, first)
            part_a = count_part(st_a[4], 0, hw)
        return it + BISECT_UNROLL, jnp.sum(st_a[3]) + jnp.sum(st_b[3]), part_a, st_a, st_b

    lo0 = jnp.where(n_causal >= topk, smin, neg_inf)

    def init_state(c0):
        return (lo0[:, c0:c0 + hw], smax[:, c0:c0 + hw], jnp.full((1, hw), neg_inf, F32),
                jnp.ones((1, hw), jnp.int32), smax[:, c0:c0 + hw], jnp.zeros((1, hw), jnp.int32))

    st_a0, st_b0 = init_state(0), init_state(hw)
    out = lax.while_loop(cond, body, (jnp.int32(0), jnp.int32(tq), count_part(st_a0[4], 0, hw), st_a0, st_b0))
    tau = jnp.concatenate([out[3][2], out[4][2]], axis=1)

    odd = jnp.concatenate([out[3][5], out[4][5]], axis=1)

    picked = jnp.where(score_ref[:keys, :] >= tau, 0.0, NEG_BIG)
    bias_ref[:keys, :] = jnp.where(causal_mask(), picked, NEG_BIG).astype(bias_ref.dtype)

    @pl.when(jnp.max(jnp.where(tau > neg_inf, odd, 0)) > 0)
    def _():
        tc = 256
        row = lax.broadcasted_iota(jnp.int32, (tc, tc), 0)
        col = lax.broadcasted_iota(jnp.int32, (tc, tc), 1)
        lower = jnp.where(col <= row, 1.0, 0.0).astype(BF16)
        n_gt = jnp.sum(jnp.where(score_ref[:keys, :] > tau, 1.0, 0.0), axis=0, keepdims=True)
        need = kf - n_gt
        carry = jnp.zeros((1, tq), F32)
        for c in range(keys // tc):
            sc = score_ref[c * tc:(c + 1) * tc, :]
            eq = jnp.where(sc == tau, 1.0, 0.0)
            prefix = jnp.dot(lower, eq.astype(BF16), preferred_element_type=F32) + carry
            carry = carry + jnp.sum(eq, axis=0, keepdims=True)
            keep = (sc > tau) | ((sc == tau) & (prefix <= need))
            sel_ref[c * tc:(c + 1) * tc, :] = jnp.where(keep, 0.0, NEG_BIG)
        bias_ref[:keys, :] = jnp.where(causal_mask(), sel_ref[:keys, :], NEG_BIG).astype(bias_ref.dtype)


def _topk_mask_kernel(iq_ref, qpack_ref, kpack_ref, bias_ref, score_ref, sel_ref, *, q0, kend, tq, topk):
    half = pl.program_id(1)
    for which, keys in ((0, kend - tq), (1, kend)):
        @pl.when(half == which)
        def _():
            _topk_tile(iq_ref, qpack_ref, kpack_ref, bias_ref, score_ref, sel_ref,
                       qbase=q0 + which * tq, keys=keys, tq=tq, topk=topk)
            if keys < kend:
                bias_ref[keys:, :] = jnp.full((kend - keys, tq), NEG_BIG, bias_ref.dtype)


def _topk_mask(idx3, *, j, topk):
    b = idx3.shape[0]
    kend = Q_SUPER * (j + 1)
    q0 = Q_SUPER * j
    tq = HALF_ROWS
    nq = Q_SUPER // tq
    pack_blk = IDX_Q_W // LANES
    return pl.pallas_call(
        functools.partial(_topk_mask_kernel, q0=q0, kend=kend, tq=tq, topk=topk),
        grid=(b, nq),
        in_specs=[pl.BlockSpec((None, tq, IDX_Q_W), lambda bi, qi: (bi, q0 // tq + qi, 0)),
                  pl.BlockSpec((None, tq, LANES), lambda bi, qi: (bi, q0 // tq + qi, pack_blk)),
                  pl.BlockSpec((None, kend, LANES), lambda bi, qi: (bi, 0, pack_blk))],
        out_specs=pl.BlockSpec((None, kend, tq), lambda bi, qi: (bi, 0, qi)),
        out_shape=jax.ShapeDtypeStruct((b, kend, Q_SUPER), BF16),
        scratch_shapes=[pltpu.VMEM((kend, tq), F32), pltpu.VMEM((kend, tq), F32)],
        compiler_params=_params(2),
        name=f"topk_mask_{j}",
    )(idx3, idx3, idx3)


def _attn_tile(q_ref, k_ref, v_ref, bias_t_ref, o_ref, *, row0, keys):
    tq = HEAD_DIM
    rows = slice(row0, row0 + tq)
    exp2_scale = (HEAD_DIM ** -0.5) * math.log2(math.e)
    bias_t = bias_t_ref[:keys, rows]
    row = lax.broadcasted_iota(jnp.int32, (tq, tq), 0)
    col = lax.broadcasted_iota(jnp.int32, (tq, tq), 1)
    eye = jnp.where(row == col, 1.0, 0.0).astype(BF16)
    eye_rows = jnp.concatenate([eye] * ATTN_GROUP, axis=0)
    ones = jnp.ones((keys, HEAD_DIM), BF16)
    for g in range(ATTN_KV_HEADS):
        kg = k_ref[:keys, g * HEAD_DIM:(g + 1) * HEAD_DIM]
        vg = v_ref[:keys, g * HEAD_DIM:(g + 1) * HEAD_DIM]
        h0 = g * ATTN_GROUP
        qg = jnp.concatenate([q_ref[rows, (h0 + r) * HEAD_DIM:(h0 + r + 1) * HEAD_DIM]
                              for r in range(ATTN_GROUP)], axis=0)
        q_ext = jnp.concatenate([qg, eye_rows], axis=1)
        k_ext = jnp.concatenate([kg, bias_t], axis=1)
        t = lax.dot_general(q_ext, k_ext, (((1,), (1,)), ((), ())), preferred_element_type=F32)
        m = jnp.max(t, axis=-1, keepdims=True)
        p = jnp.exp2((t - m) * exp2_scale).astype(BF16)
        v_ext = jnp.concatenate([vg, ones], axis=1)
        o_ext = jnp.dot(p, v_ext, preferred_element_type=F32)
        o = o_ext[:, :HEAD_DIM] / o_ext[:, HEAD_DIM:]
        for r in range(ATTN_GROUP):
            o_ref[rows, (h0 + r) * HEAD_DIM:(h0 + r + 1) * HEAD_DIM] = o[r * tq:(r + 1) * tq].astype(o_ref.dtype)


def _attn_kernel(q_ref, k_ref, v_ref, bias_t_ref, o_prev_ref, o_ref, *, kend):
    del o_prev_ref
    for row0 in range(0, Q_SUPER, HEAD_DIM):
        keys = kend - HALF_ROWS if row0 < HALF_ROWS else kend
        _attn_tile(q_ref, k_ref, v_ref, bias_t_ref, o_ref, row0=row0, keys=keys)


def _attention(q3, kv3, bias_t, o_prev, *, j):
    b, s, _ = q3.shape
    kend = Q_SUPER * (j + 1)
    return pl.pallas_call(
        functools.partial(_attn_kernel, kend=kend),
        grid=(b,),
        in_specs=[pl.BlockSpec((None, Q_SUPER, ATTN_W), lambda bi: (bi, j, 0)),
                  pl.BlockSpec((None, kend, KV_W), lambda bi: (bi, 0, 0)),
                  pl.BlockSpec((None, kend, KV_W), lambda bi: (bi, 0, 1)),
                  pl.BlockSpec((None, kend, Q_SUPER), lambda bi: (bi, 0, 0)),
                  pl.BlockSpec(memory_space=pl.ANY)],
        out_specs=pl.BlockSpec((None, Q_SUPER, ATTN_W), lambda bi: (bi, j, 0)),
        out_shape=jax.ShapeDtypeStruct((b, s, ATTN_W), BF16),
        input_output_aliases={4: 0},
        compiler_params=_params(1),
        name=f"attention_{j}",
    )(q3, kv3, kv3, bias_t, o_prev)


def _retention_kernel(q_ref, k_ref, v_ref, gate_ref, decay_ref, xi_ref, zeta_ref, cd_ref, g_ref, b_ref,
                      o_ref, *, n_chunks):
    c = RET_CHUNK
    decay = decay_ref[...]
    xi = xi_ref[...]
    zeta = zeta_ref[...]
    cd = cd_ref[...]
    gn_g = g_ref[...]
    gn_b = b_ref[...]
    state = jnp.zeros((RET_QK_DIM, RET_V_DIM), F32)
    for n in range(n_chunks):
        sl = slice(n * c, (n + 1) * c)
        q = q_ref[sl, :]
        k = k_ref[sl, :]
        v = v_ref[sl, :]
        qk = lax.dot_general(q, k, (((1,), (1,)), ((), ())), preferred_element_type=F32) * decay
        inner = jnp.dot(qk.astype(BF16), v, preferred_element_type=F32)
        cross = jnp.dot(q, state.astype(BF16), preferred_element_type=F32) * xi
        kz_t = (k.astype(F32) * zeta).T.astype(BF16)
        upd = jnp.dot(kz_t, v, preferred_element_type=F32)
        state = upd + cd * state
        y = inner + cross
        mu = jnp.mean(y, axis=-1, keepdims=True)
        d = y - mu
        var = jnp.mean(d * d, axis=-1, keepdims=True)
        yn = d * lax.rsqrt(var + EPS)
        z = yn * gn_g + gn_b
        gate = gate_ref[sl, :].astype(F32)
        silu = gate * (1.0 / (1.0 + jnp.exp(-gate)))
        o_ref[sl, :] = (silu * z).astype(o_ref.dtype)


def _retention_tables():
    c = RET_CHUNK
    lg = np.log(1.0 - np.exp2(-5.0 - np.arange(RET_HEADS, dtype=np.float32))).astype(np.float32)
    pos = np.arange(c, dtype=np.float32)
    diff = pos[:, None] - pos[None, :]
    decay = np.where(diff[None] >= 0, np.exp(lg[:, None, None] * np.maximum(diff, 0.0)[None]), 0.0)
    xi = np.exp(lg[:, None] * (pos[None, :] + 1.0))
    zeta = np.exp(lg[:, None] * (c - 1.0 - pos[None, :]))
    cd = np.exp(lg * c)
    f = lambda a: jnp.asarray(a.astype(np.float32))
    return (f(decay),
            f(np.broadcast_to(xi[:, :, None], (RET_HEADS, c, RET_V_DIM))),
            f(np.broadcast_to(zeta[:, :, None], (RET_HEADS, c, RET_QK_DIM))),
            f(np.broadcast_to(cd[:, None, None], (RET_HEADS, 1, RET_V_DIM))))


def _retention(rot3, plain3, gn_g, gn_b):
    b, s, _ = rot3.shape
    decay, xi, zeta, cd = _retention_tables()
    v_blk0 = 0
    gate_blk0 = RET_V_W // RET_V_DIM
    return pl.pallas_call(
        functools.partial(_retention_kernel, n_chunks=s // RET_CHUNK),
        grid=(b, RET_HEADS),
        in_specs=[pl.BlockSpec((None, s, RET_QK_DIM), lambda bi, h: (bi, 0, h)),
                  pl.BlockSpec((None, s, RET_QK_DIM), lambda bi, h: (bi, 0, RET_HEADS + h)),
                  pl.BlockSpec((None, s, RET_V_DIM), lambda bi, h: (bi, 0, v_blk0 + h)),
                  pl.BlockSpec((None, s, RET_V_DIM), lambda bi, h: (bi, 0, gate_blk0 + h)),
                  pl.BlockSpec((None, RET_CHUNK, RET_CHUNK), lambda bi, h: (h, 0, 0)),
                  pl.BlockSpec((None, RET_CHUNK, RET_V_DIM), lambda bi, h: (h, 0, 0)),
                  pl.BlockSpec((None, RET_CHUNK, RET_QK_DIM), lambda bi, h: (h, 0, 0)),
                  pl.BlockSpec((None, 1, RET_V_DIM), lambda bi, h: (h, 0, 0)),
                  pl.BlockSpec((1, RET_V_DIM), lambda bi, h: (0, h)),
                  pl.BlockSpec((1, RET_V_DIM), lambda bi, h: (0, h))],
        out_specs=pl.BlockSpec((None, s, RET_V_DIM), lambda bi, h: (bi, 0, h)),
        out_shape=jax.ShapeDtypeStruct((b, s, RET_V_W), BF16),
        compiler_params=_params(2),
        name="retention",
    )(rot3, rot3, plain3, plain3, decay, xi, zeta, cd, gn_g.reshape(1, -1), gn_b.reshape(1, -1))


def _merge_kernel(oa_ref, or_ref, wa_ref, wr_ref, ga_ref, gb_ref, o_ref, wa_bf, wr_bf):
    @pl.when(pl.program_id(1) == 0)
    def _():
        wa_bf[...] = wa_ref[...].astype(BF16)
        wr_bf[...] = wr_ref[...].astype(BF16)

    ya = jnp.dot(oa_ref[...], wa_bf[...], preferred_element_type=F32)
    yr = jnp.dot(or_ref[...], wr_bf[...], preferred_element_type=F32)
    ga = ga_ref[...].astype(F32)
    gb = gb_ref[...].astype(F32)
    sa = 1.0 / (1.0 + jnp.exp(-ga))
    sb = 1.0 / (1.0 + jnp.exp(-gb))
    o_ref[...] = (sa * ya + sb * yr).astype(o_ref.dtype)


def _merge(o_attn, o_ret, w_ua3, w_ur3, layer, plain, tm=1024, tn=512):
    m = o_attn.shape[0]
    ga_blk0 = (2 * RET_V_W) // tn
    gb_blk0 = (2 * RET_V_W + D_MODEL) // tn
    return pl.pallas_call(
        _merge_kernel,
        grid=(D_MODEL // tn, m // tm),
        in_specs=[pl.BlockSpec((tm, ATTN_W), lambda j, i: (i, 0)),
                  pl.BlockSpec((tm, RET_V_W), lambda j, i: (i, 0)),
                  pl.BlockSpec((None, ATTN_W, tn), lambda j, i: (layer, 0, j)),
                  pl.BlockSpec((None, RET_V_W, tn), lambda j, i: (layer, 0, j)),
                  pl.BlockSpec((tm, tn), lambda j, i: (i, ga_blk0 + j)),
                  pl.BlockSpec((tm, tn), lambda j, i: (i, gb_blk0 + j))],
        out_specs=pl.BlockSpec((tm, tn), lambda j, i: (i, j)),
        out_shape=jax.ShapeDtypeStruct((m, D_MODEL), BF16),
        scratch_shapes=[pltpu.VMEM((ATTN_W, tn), BF16), pltpu.VMEM((RET_V_W, tn), BF16)],
        compiler_params=_params(2),
        name="gated_merge",
    )(o_attn, o_ret, w_ua3, w_ur3, plain, plain)


def _rotary_tables(s):
    pos = jnp.arange(s, dtype=F32)
    inv_freq = ROPE_BASE ** (-jnp.arange(0, RET_QK_DIM, 2, dtype=F32) / RET_QK_DIM)
    ang = pos[:, None] * inv_freq[None, :]
    cos, sin = jnp.cos(ang), jnp.sin(ang)
    return jnp.concatenate([cos, cos], axis=1), jnp.concatenate([-sin, sin], axis=1)


def kernel(x, ln1_g, w_in, q_norm_g, k_norm_g, ret_gn_g, ret_gn_b, w_up_attn, w_up_ret, w_out, ln2_g,
           w_ff1, w_ff2):
    b, s, d = x.shape
    m = b * s
    depth = w_in.shape[0]
    topk = min(TOPK_MAX, s // 4)
    assert s % Q_SUPER == 0 and topk <= Q_SUPER
    cos2, sin2 = _rotary_tables(s)
    tm = 1024
    s_tiles = s // tm

    w_in_t = jnp.swapaxes(w_in, 1, 2)
    c_kv = ATTN_W
    c_iq = c_kv + 2 * KV_W
    c_rot = c_iq + IDX_Q_W + IDX_DIM + IDX_HEADS
    c_plain = c_rot + 2 * RET_QK_W
    in_proj = functools.partial(_matmul, w3=w_in_t, w_is_nk=True, tm=tm)

    xf = x.reshape(m, d)
    xg, ss = _norm_prep(xf, ln1_g[0])
    for l in range(depth):
        aq =in_proj(xg, layer=l, row_ss=ss, out_dtype=BF16, epilogue=_epi_headnorm, tn=1024, col0=0, n=ATTN_W,
                     name="proj_q",
                     extras=[(q_norm_g[l].reshape(1, HEAD_DIM), (1, HEAD_DIM), lambda i, j: (0, 0))])
        kv = in_proj(xg, layer=l, row_ss=ss, out_dtype=BF16, epilogue=_epi_kv, tn=KV_W, col0=c_kv, n=2 * KV_W,
                     name="proj_kv",
                     extras=[(k_norm_g[l].reshape(1, HEAD_DIM), (1, HEAD_DIM), lambda i, j: (0, 0))])
        idx = in_proj(xg, layer=l, row_ss=ss, out_dtype=F32, epilogue=_epi_plain, tn=IDX_PACK_W, col0=c_iq,
                      n=IDX_PACK_W, name="proj_idx")
        rot = in_proj(xg, layer=l, row_ss=ss, out_dtype=BF16, epilogue=_epi_rotary, tn=RET_QK_W, col0=c_rot,
                      n=2 * RET_QK_W, name="proj_rot",
                      extras=[(cos2, (tm, RET_QK_DIM), lambda i, j: (i % s_tiles, 0)),
                              (sin2, (tm, RET_QK_DIM), lambda i, j: (i % s_tiles, 0))])
        plain = in_proj(xg, layer=l, row_ss=ss, out_dtype=BF16, epilogue=_epi_plain, tn=1024, col0=c_plain,
                        n=PLAIN_W, name="proj_plain")

        aq3 = aq.reshape(b, s, ATTN_W)
        kv3 = kv.reshape(b, s, 2 * KV_W)
        idx3 = idx.reshape(b, s, IDX_PACK_W)
        plain3 = plain.reshape(b, s, PLAIN_W)
        rot3 = rot.reshape(b, s, 2 * RET_QK_W)

        o_attn3 = jnp.zeros((b, s, ATTN_W), BF16)
        for j in range(s // Q_SUPER):
            bias = _topk_mask(idx3, j=j, topk=topk)
            o_attn3 = _attention(aq3, kv3, bias, o_attn3, j=j)
        o_attn = o_attn3.reshape(m, ATTN_W)

        o_ret = _retention(rot3, plain3, ret_gn_g[l], ret_gn_b[l]).reshape(m, RET_V_W)

        merged = _merge(o_attn, o_ret, w_up_attn, w_up_ret, l, plain)
        xf, xg2, ss2 = _matmul(merged, w_out, l, out_dtype=F32, epilogue=_epi_residual, tm=tm, tn=1024,
                               name="proj_out", extras=[(xf, (tm, 1024), lambda i, j: (i, j))],
                               norm_gain=ln2_g[l].reshape(1, d), w_buffers=1)

        f, w_ff2_bf = _matmul(xg2, w_ff1, l, out_dtype=BF16, epilogue=_epi_relu2, tm=tm, tn=1024, row_ss=ss2,
                              side_cast=(w_ff2, l), name="ff1")
        next_gain = ln1_g[l + 1].reshape(1, d) if l + 1 < depth else None
        res = _matmul(f, w_ff2_bf[None], 0, out_dtype=F32, epilogue=_epi_residual, tm=tm, tn=1024, tk=2048,
                      name="ff2", extras=[(xf, (tm, 1024), lambda i, j: (i, j))], norm_gain=next_gain)
        xf, xg, ss = res if next_gain is not None else (res, None, None)
    return xf.reshape(b, s, d)
```

```python
import functools
import math

import jax
import jax.numpy as jnp
import numpy as np
from jax import lax
from jax.experimental import pallas as pl
from jax.experimental.pallas import tpu as pltpu

D_MODEL = 2048
HEAD_DIM = 128
ATTN_Q_HEADS = 16
ATTN_KV_HEADS = 4
ATTN_GROUP = ATTN_Q_HEADS // ATTN_KV_HEADS
IDX_HEADS = 16
IDX_DIM = 64
TOPK_MAX = 256
RET_HEADS = 8
RET_QK_DIM = 128
RET_V_DIM = 256
RET_CHUNK = 128
ROPE_BASE = 10000.0
D_FF = 4 * D_MODEL
EPS = 1e-6

ATTN_W = ATTN_Q_HEADS * HEAD_DIM
KV_W = ATTN_KV_HEADS * HEAD_DIM
IDX_Q_W = IDX_HEADS * IDX_DIM
RET_QK_W = RET_HEADS * RET_QK_DIM
RET_V_W = RET_HEADS * RET_V_DIM

LANES = 128
PLAIN_W = 2 * RET_V_W + 2 * D_MODEL
IDX_PACK_W = IDX_Q_W + LANES
COUNT_ROWS = 64
BISECT_MAX_STEPS = 2200
BISECT_UNROLL = 4
Q_SUPER = 512
HALF_ROWS = Q_SUPER // 2
NEG_BIG = -1e30
VMEM_LIMIT = 56 * 1024 * 1024

F32 = jnp.float32
BF16 = jnp.bfloat16


def _params(n_axes):
    return pltpu.CompilerParams(dimension_semantics=("arbitrary",) * n_axes,
                                vmem_limit_bytes=VMEM_LIMIT)


def _row_sumsq(x):
    return jnp.broadcast_to(jnp.sum(x * x, axis=-1, keepdims=True), (x.shape[0], LANES))


def _norm_prep_kernel(x_ref, g_ref, xg_ref, ss_ref):
    x = x_ref[...]
    xg_ref[...] = (x * g_ref[...]).astype(xg_ref.dtype)
    ss_ref[...] = _row_sumsq(x)


def _norm_prep(x, g, tm=512):
    m, d = x.shape
    return pl.pallas_call(
        _norm_prep_kernel,
        grid=(m // tm,),
        in_specs=[pl.BlockSpec((tm, d), lambda i: (i, 0)),
                  pl.BlockSpec((1, d), lambda i: (0, 0))],
        out_specs=[pl.BlockSpec((tm, d), lambda i: (i, 0)),
                   pl.BlockSpec((None, tm, LANES), lambda i: (0, i, 0))],
        out_shape=[jax.ShapeDtypeStruct((m, d), BF16), jax.ShapeDtypeStruct((1, m, LANES), F32)],
        compiler_params=_params(1),
        name="norm_prep",
    )(x, g.reshape(1, d))


def _mm_kernel(*refs, nk, n_extra, epilogue, cache_w, w_is_nk, row_scaled, norm_out, side_cast, d_norm):
    a_ref, w_ref = refs[0], refs[1]
    n_fixed = 2 + int(row_scaled) + n_extra
    n_in = n_fixed + int(norm_out) + int(side_cast)
    ss_in_ref = refs[2] if row_scaled else None
    extra = refs[2 + int(row_scaled):n_fixed]
    gain_ref = refs[n_fixed] if norm_out else None
    o_ref = refs[n_in]
    n_out = 1 + 2 * int(norm_out) + int(side_cast)
    scratch = refs[n_in + n_out:]
    i = pl.program_id(1)
    k = pl.program_id(2)
    if side_cast:
        refs[n_in + n_out - 1][...] = refs[n_in - 1][...].astype(BF16)
    if cache_w:
        wbf_ref = scratch[0]

        @pl.when(i == 0)
        def _():
            wbf_ref[k] = (w_ref[0].T if w_is_nk else w_ref[...]).astype(BF16)

        w = wbf_ref[k]
    else:
        w = w_ref[...]
    prod = jnp.dot(a_ref[...], w, preferred_element_type=F32)

    def finish(acc):
        if row_scaled:
            r = lax.rsqrt(jnp.sum(ss_in_ref[...], axis=0) * (1.0 / d_norm) + EPS)
            acc = acc * jnp.concatenate([r] * (acc.shape[1] // LANES), axis=1)
        out = epilogue(acc, *extra)
        o_ref[...] = out.astype(o_ref.dtype)
        if norm_out:
            xg_ref, ss_out_ref = refs[n_in + 1], refs[n_in + 2]
            xg_ref[...] = (out * gain_ref[...]).astype(xg_ref.dtype)
            ss_out_ref[...] = _row_sumsq(out)

    if nk == 1:
        finish(prod)
        return
    acc_ref = scratch[-1]

    @pl.when(k == 0)
    def _():
        acc_ref[...] = prod

    @pl.when(k > 0)
    def _():
        acc_ref[...] += prod

    @pl.when(k == nk - 1)
    def _():
        finish(acc_ref[...])


def _matmul(a, w3, layer, *, out_dtype, epilogue, extras=(), tm=1024, tn=1024, tk=None, col0=0, n=None,
            w_is_nk=False, row_ss=None, norm_gain=None, w_buffers=2, side_cast=None, name="matmul"):
    m, kdim = a.shape
    n_total = w3.shape[1] if w_is_nk else w3.shape[2]
    n = n_total - col0 if n is None else n
    tk = kdim if tk is None else tk
    nk = kdim // tk
    tn = min(tn, n)
    assert m % tm == 0 and n % tn == 0 and kdim % tk == 0
    cache_w = w3.dtype != BF16
    if w_is_nk:
        assert cache_w and nk == 1 and col0 % 8 == 0
        w_spec = pl.BlockSpec((pl.Element(1), pl.Element(tn), pl.Element(tk)),
                              lambda j, i, k: (layer, pl.multiple_of(col0 + j * tn, 8), 0))
    else:
        assert col0 % tn == 0
        jb0 = col0 // tn
        if cache_w:
            w_map = lambda j, i, k: (layer, jnp.where(i == 0, k, nk - 1), jb0 + j)
        else:
            w_map = lambda j, i, k: (layer, k, jb0 + j)
        mode = {} if w_buffers == 2 else {"pipeline_mode": pl.Buffered(w_buffers)}
        w_spec = pl.BlockSpec((None, tk, tn), w_map, **mode)
    operands = [a, w3]
    in_specs = [pl.BlockSpec((tm, tk), lambda j, i, k: (i, k)), w_spec]
    if row_ss is not None:
        operands.append(row_ss)
        in_specs.append(pl.BlockSpec((row_ss.shape[0], tm, LANES), lambda j, i, k: (0, i, 0)))
    for arr, bshape, imap in extras:
        operands.append(arr)
        in_specs.append(pl.BlockSpec(bshape, functools.partial(lambda j, i, k, f: f(i, j), f=imap)))
    out_specs = [pl.BlockSpec((tm, tn), lambda j, i, k: (i, j))]
    out_shape = [jax.ShapeDtypeStruct((m, n), out_dtype)]
    if norm_gain is not None:
        operands.append(norm_gain)
        in_specs.append(pl.BlockSpec((1, tn), lambda j, i, k: (0, j)))
        out_specs += [pl.BlockSpec((tm, tn), lambda j, i, k: (i, j)),
                      pl.BlockSpec((None, tm, LANES), lambda j, i, k: (j, i, 0))]
        out_shape += [jax.ShapeDtypeStruct((m, n), BF16), jax.ShapeDtypeStruct((n // tn, m, LANES), F32)]
    if side_cast is not None:
        other3, other_layer = side_cast
        steps, ni = (n // tn) * (m // tm), m // tm
        _, rows_o, cols_o = other3.shape
        assert nk == 1 and rows_o % steps == 0 and (rows_o // steps) % 16 == 0
        slab = rows_o // steps
        operands.append(other3)
        in_specs.append(pl.BlockSpec((None, slab, cols_o), lambda j, i, k: (other_layer, j * ni + i, 0)))
        out_specs.append(pl.BlockSpec((slab, cols_o), lambda j, i, k: (j * ni + i, 0)))
        out_shape.append(jax.ShapeDtypeStruct((rows_o, cols_o), BF16))
    scratch = []
    if cache_w:
        scratch.append(pltpu.VMEM((nk, tk, tn), BF16))
    if nk > 1:
        scratch.append(pltpu.VMEM((tm, tn), F32))
    res = pl.pallas_call(
        functools.partial(_mm_kernel, nk=nk, n_extra=len(extras), epilogue=epilogue, cache_w=cache_w,
                          w_is_nk=w_is_nk, row_scaled=row_ss is not None, norm_out=norm_gain is not None,
                          side_cast=side_cast is not None, d_norm=kdim),
        grid=(n // tn, m // tm, nk),
        in_specs=in_specs,
        out_specs=out_specs,
        out_shape=out_shape,
        scratch_shapes=scratch,
        compiler_params=_params(3),
        name=name,
    )(*operands)
    return res if len(res) > 1 else res[0]


def _epi_plain(acc):
    return acc


def _epi_headnorm(acc, g_ref):
    g = g_ref[...]
    outs = []
    for h in range(acc.shape[1] // HEAD_DIM):
        a = acc[:, h * HEAD_DIM:(h + 1) * HEAD_DIM]
        r = lax.rsqrt(jnp.mean(a * a, axis=-1, keepdims=True) + EPS)
        outs.append((a * r) * g)
    return jnp.concatenate(outs, axis=1)


def _epi_kv(acc, g_ref):
    return jnp.where(pl.program_id(0) == 0, _epi_headnorm(acc, g_ref), acc)


def _epi_rotary(acc, cos_ref, sin_ref):
    c = cos_ref[...]
    s = sin_ref[...]
    scale = jnp.where(pl.program_id(0) == 0, 1.0, RET_QK_DIM ** -0.5).astype(F32)
    outs = []
    for h in range(acc.shape[1] // RET_QK_DIM):
        a = acc[:, h * RET_QK_DIM:(h + 1) * RET_QK_DIM]
        swapped = pltpu.roll(a, RET_QK_DIM // 2, axis=1)
        outs.append((a * c + swapped * s) * scale)
    return jnp.concatenate(outs, axis=1)


def _epi_residual(acc, x_ref):
    return x_ref[...] + acc


def _epi_relu2(acc):
    r = jnp.maximum(acc, 0.0)
    return r * r


def _topk_tile(iq_ref, qpack_ref, kpack_ref, bias_ref, score_ref, sel_ref, *, qbase, keys, tq, topk):
    rows = COUNT_ROWS
    neg_inf = float("-inf")

    ik = kpack_ref[:keys, :IDX_DIM].astype(BF16)
    w_t = qpack_ref[...].T * (IDX_HEADS ** -0.5)
    score = None
    for h in range(IDX_HEADS):
        iqh = iq_ref[:, h * IDX_DIM:(h + 1) * IDX_DIM].astype(BF16)
        logits = lax.dot_general(ik, iqh, (((1,), (1,)), ((), ())), preferred_element_type=F32)
        term = jnp.maximum(logits, 0.0) * w_t[IDX_DIM + h:IDX_DIM + h + 1, :]
        score = term if score is None else score + term

    def causal_mask():
        kpos = lax.broadcasted_iota(jnp.int32, (keys, tq), 0)
        qcol = lax.broadcasted_iota(jnp.int32, (keys, tq), 1) + qbase
        return kpos <= qcol

    causal = causal_mask()
    score = jnp.where(score == 0.0, 0.0, score)
    lo_fill = jnp.where(causal, score, neg_inf)
    score_ref[:keys, :] = lo_fill
    smax = jnp.max(jnp.max(lo_fill.reshape(keys // rows, rows, tq), axis=0), axis=0, keepdims=True)
    hi_fill = jnp.where(causal, score, float("inf"))
    smin = jnp.min(jnp.min(hi_fill.reshape(keys // rows, rows, tq), axis=0), axis=0, keepdims=True)
    n_causal = jnp.minimum(lax.broadcasted_iota(jnp.int32, (1, tq), 1) + (qbase + 1), keys)

    def count_part(thr, c0, width):
        thr_b = jnp.broadcast_to(thr, (rows, width))
        part = jnp.zeros((rows, width), F32)
        for i in range(keys // rows):
            part = part + jnp.where(score_ref[i * rows:(i + 1) * rows, c0:c0 + width] >= thr_b, 1.0, 0.0)
        return part

    kf = float(topk)
    hw = tq // 2

    def step(part, state, first):
        lo, hi, tau, act, mid, odd = state
        cnt = jnp.sum(part, axis=0, keepdims=True)
        ge = cnt >= kf
        hit = (cnt == kf) | (first & ge)
        lo = jnp.where(ge, mid, lo)
        hi = jnp.where(ge, hi, mid)
        nxt = 0.5 * lo + 0.5 * hi
        stuck = (nxt <= lo) | (nxt >= hi)
        live = act > 0
        tau = jnp.where(live & hit, mid, jnp.where(live & stuck, lo, tau))
        inexact = (hit & (cnt != kf)) | (jnp.logical_not(hit) & stuck)
        odd = jnp.where(live & inexact, 1, odd)
        act = jnp.where(hit | stuck, 0, act)
        return lo, hi, tau, act, nxt, odd

    def cond(c):
        return (c[1] > 0) & (c[0] < BISECT_MAX_STEPS)

    def body(c):
        it, _, part_a, st_a, st_b = c
        for u in range(BISECT_UNROLL):
            first = ((jnp.zeros((1, hw), jnp.int32) + it) == 0) if u == 0 else jnp.zeros((1, hw), jnp.bool_)
            st_a = step(part_a, st_a, first)
            part_b = count_part(st_b[4], hw, hw)
            st_b = step(part_b, st_b, first)
            part_a = count_part(st_a[4], 0, hw)
        return it + BISECT_UNROLL, jnp.sum(st_a[3]) + jnp.sum(st_b[3]), part_a, st_a, st_b

    lo0 = jnp.where(n_causal >= topk, smin, neg_inf)

    def init_state(c0):
        return (lo0[:, c0:c0 + hw], smax[:, c0:c0 + hw], jnp.full((1, hw), neg_inf, F32),
                jnp.ones((1, hw), jnp.int32), smax[:, c0:c0 + hw], jnp.zeros((1, hw), jnp.int32))

    st_a0, st_b0 = init_state(0), init_state(hw)
    out = lax.while_loop(cond, body, (jnp.int32(0), jnp.int32(tq), count_part(st_a0[4], 0, hw), st_a0, st_b0))
    tau = jnp.concatenate([out[3][2], out[4][2]], axis=1)

    odd = jnp.concatenate([out[3][5], out[4][5]], axis=1)

    picked = jnp.where(score_ref[:keys, :] >= tau, 0.0, NEG_BIG)
    bias_ref[:keys, :] = jnp.where(causal_mask(), picked, NEG_BIG).astype(bias_ref.dtype)

    @pl.when(jnp.max(jnp.where(tau > neg_inf, odd, 0)) > 0)
    def _():
        tc = 256
        row = lax.broadcasted_iota(jnp.int32, (tc, tc), 0)
        col = lax.broadcasted_iota(jnp.int32, (tc, tc), 1)
        lower = jnp.where(col <= row, 1.0, 0.0).astype(BF16)
        n_gt = jnp.sum(jnp.where(score_ref[:keys, :] > tau, 1.0, 0.0), axis=0, keepdims=True)
        need = kf - n_gt
        carry = jnp.zeros((1, tq), F32)
        for c in range(keys // tc):
            sc = score_ref[c * tc:(c + 1) * tc, :]
            eq = jnp.where(sc == tau, 1.0, 0.0)
            prefix = jnp.dot(lower, eq.astype(BF16), preferred_element_type=F32) + carry
            carry = carry + jnp.sum(eq, axis=0, keepdims=True)
            keep = (sc > tau) | ((sc == tau) & (prefix <= need))
            sel_ref[c * tc:(c + 1) * tc, :] = jnp.where(keep, 0.0, NEG_BIG)
        bias_ref[:keys, :] = jnp.where(causal_mask(), sel_ref[:keys, :], NEG_BIG).astype(bias_ref.dtype)


def _topk_mask_kernel(iq_ref, qpack_ref, kpack_ref, bias_ref, score_ref, sel_ref, *, q0, kend, tq, topk):
    half = pl.program_id(1)
    for which, keys in ((0, kend - tq), (1, kend)):
        @pl.when(half == which)
        def _():
            _topk_tile(iq_ref, qpack_ref, kpack_ref, bias_ref, score_ref, sel_ref,
                       qbase=q0 + which * tq, keys=keys, tq=tq, topk=topk)
            if keys < kend:
                bias_ref[keys:, :] = jnp.full((kend - keys, tq), NEG_BIG, bias_ref.dtype)


def _topk_mask(idx3, *, j, topk):
    b = idx3.shape[0]
    kend = Q_SUPER * (j + 1)
    q0 = Q_SUPER * j
    tq = HALF_ROWS
    nq = Q_SUPER // tq
    pack_blk = IDX_Q_W // LANES
    return pl.pallas_call(
        functools.partial(_topk_mask_kernel, q0=q0, kend=kend, tq=tq, topk=topk),
        grid=(b, nq),
        in_specs=[pl.BlockSpec((None, tq, IDX_Q_W), lambda bi, qi: (bi, q0 // tq + qi, 0)),
                  pl.BlockSpec((None, tq, LANES), lambda bi, qi: (bi, q0 // tq + qi, pack_blk)),
                  pl.BlockSpec((None, kend, LANES), lambda bi, qi: (bi, 0, pack_blk))],
        out_specs=pl.BlockSpec((None, kend, tq), lambda bi, qi: (bi, 0, qi)),
        out_shape=jax.ShapeDtypeStruct((b, kend, Q_SUPER), BF16),
        scratch_shapes=[pltpu.VMEM((kend, tq), F32), pltpu.VMEM((kend, tq), F32)],
        compiler_params=_params(2),
        name=f"topk_mask_{j}",
    )(idx3, idx3, idx3)


def _attn_tile(q_ref, k_ref, v_ref, bias_t_ref, o_ref, *, row0, keys):
    tq = HEAD_DIM
    rows = slice(row0, row0 + tq)
    exp2_scale = (HEAD_DIM ** -0.5) * math.log2(math.e)
    bias_t = bias_t_ref[:keys, rows]
    row = lax.broadcasted_iota(jnp.int32, (tq, tq), 0)
    col = lax.broadcasted_iota(jnp.int32, (tq, tq), 1)
    eye = jnp.where(row == col, 1.0, 0.0).astype(BF16)
    eye_rows = jnp.concatenate([eye] * ATTN_GROUP, axis=0)
    ones = jnp.ones((keys, HEAD_DIM), BF16)
    for g in range(ATTN_KV_HEADS):
        kg = k_ref[:keys, g * HEAD_DIM:(g + 1) * HEAD_DIM]
        vg = v_ref[:keys, g * HEAD_DIM:(g + 1) * HEAD_DIM]
        h0 = g * ATTN_GROUP
        qg = jnp.concatenate([q_ref[rows, (h0 + r) * HEAD_DIM:(h0 + r + 1) * HEAD_DIM]
                              for r in range(ATTN_GROUP)], axis=0)
        q_ext = jnp.concatenate([qg, eye_rows], axis=1)
        k_ext = jnp.concatenate([kg, bias_t], axis=1)
        t = lax.dot_general(q_ext, k_ext, (((1,), (1,)), ((), ())), preferred_element_type=F32)
        m = jnp.max(t, axis=-1, keepdims=True)
        p = jnp.exp2((t - m) * exp2_scale).astype(BF16)
        v_ext = jnp.concatenate([vg, ones], axis=1)
        o_ext = jnp.dot(p, v_ext, preferred_element_type=F32)
        o = o_ext[:, :HEAD_DIM] / o_ext[:, HEAD_DIM:]
        for r in range(ATTN_GROUP):
            o_ref[rows, (h0 + r) * HEAD_DIM:(h0 + r + 1) * HEAD_DIM] = o[r * tq:(r + 1) * tq].astype(o_ref.dtype)


def _attn_kernel(q_ref, k_ref, v_ref, bias_t_ref, o_ref, *, kend):
    for row0 in range(0, Q_SUPER, HEAD_DIM):
        keys = kend - HALF_ROWS if row0 < HALF_ROWS else kend
        _attn_tile(q_ref, k_ref, v_ref, bias_t_ref, o_ref, row0=row0, keys=keys)


def _attention(qo3, kv3, bias_t, *, j):
    b, s, _ = qo3.shape
    kend = Q_SUPER * (j + 1)
    return pl.pallas_call(
        functools.partial(_attn_kernel, kend=kend),
        grid=(b,),
        in_specs=[pl.BlockSpec((None, Q_SUPER, ATTN_W), lambda bi: (bi, j, 0)),
                  pl.BlockSpec((None, kend, KV_W), lambda bi: (bi, 0, 0)),
                  pl.BlockSpec((None, kend, KV_W), lambda bi: (bi, 0, 1)),
                  pl.BlockSpec((None, kend, Q_SUPER), lambda bi: (bi, 0, 0))],
        out_specs=pl.BlockSpec((None, Q_SUPER, ATTN_W), lambda bi: (bi, j, 0)),
        out_shape=jax.ShapeDtypeStruct((b, s, ATTN_W), BF16),
        input_output_aliases={0: 0},
        compiler_params=_params(1),
        name=f"attention_{j}",
    )(qo3, kv3, kv3, bias_t)


def _retention_kernel(q_ref, k_ref, v_ref, gate_ref, decay_ref, xi_ref, zeta_ref, cd_ref, g_ref, b_ref,
                      o_ref, *, n_chunks):
    c = RET_CHUNK
    decay = decay_ref[...]
    xi = xi_ref[...]
    zeta = zeta_ref[...]
    cd = cd_ref[...]
    gn_g = g_ref[...]
    gn_b = b_ref[...]
    state = jnp.zeros((RET_QK_DIM, RET_V_DIM), F32)
    for n in range(n_chunks):
        sl = slice(n * c, (n + 1) * c)
        q = q_ref[sl, :]
        k = k_ref[sl, :]
        v = v_ref[sl, :]
        qk = lax.dot_general(q, k, (((1,), (1,)), ((), ())), preferred_element_type=F32) * decay
        inner = jnp.dot(qk.astype(BF16), v, preferred_element_type=F32)
        cross = jnp.dot(q, state.astype(BF16), preferred_element_type=F32) * xi
        kz_t = (k.astype(F32) * zeta).T.astype(BF16)
        upd = jnp.dot(kz_t, v, preferred_element_type=F32)
        state = upd + cd * state
        y = inner + cross
        mu = jnp.mean(y, axis=-1, keepdims=True)
        d = y - mu
        var = jnp.mean(d * d, axis=-1, keepdims=True)
        yn = d * lax.rsqrt(var + EPS)
        z = yn * gn_g + gn_b
        gate = gate_ref[sl, :].astype(F32)
        silu = gate * (1.0 / (1.0 + jnp.exp(-gate)))
        o_ref[sl, :] = (silu * z).astype(o_ref.dtype)


def _retention_tables():
    c = RET_CHUNK
    lg = np.log(1.0 - np.exp2(-5.0 - np.arange(RET_HEADS, dtype=np.float32))).astype(np.float32)
    pos = np.arange(c, dtype=np.float32)
    diff = pos[:, None] - pos[None, :]
    decay = np.where(diff[None] >= 0, np.exp(lg[:, None, None] * np.maximum(diff, 0.0)[None]), 0.0)
    xi = np.exp(lg[:, None] * (pos[None, :] + 1.0))
    zeta = np.exp(lg[:, None] * (c - 1.0 - pos[None, :]))
    cd = np.exp(lg * c)
    f = lambda a: jnp.asarray(a.astype(np.float32))
    return (f(decay),
            f(np.broadcast_to(xi[:, :, None], (RET_HEADS, c, RET_V_DIM))),
            f(np.broadcast_to(zeta[:, :, None], (RET_HEADS, c, RET_QK_DIM))),
            f(np.broadcast_to(cd[:, None, None], (RET_HEADS, 1, RET_V_DIM))))


def _retention(rot3, plain3, gn_g, gn_b):
    b, s, _ = rot3.shape
    decay, xi, zeta, cd = _retention_tables()
    v_blk0 = 0
    gate_blk0 = RET_V_W // RET_V_DIM
    return pl.pallas_call(
        functools.partial(_retention_kernel, n_chunks=s // RET_CHUNK),
        grid=(b, RET_HEADS),
        in_specs=[pl.BlockSpec((None, s, RET_QK_DIM), lambda bi, h: (bi, 0, h)),
                  pl.BlockSpec((None, s, RET_QK_DIM), lambda bi, h: (bi, 0, RET_HEADS + h)),
                  pl.BlockSpec((None, s, RET_V_DIM), lambda bi, h: (bi, 0, v_blk0 + h)),
                  pl.BlockSpec((None, s, RET_V_DIM), lambda bi, h: (bi, 0, gate_blk0 + h)),
                  pl.BlockSpec((None, RET_CHUNK, RET_CHUNK), lambda bi, h: (h, 0, 0)),
                  pl.BlockSpec((None, RET_CHUNK, RET_V_DIM), lambda bi, h: (h, 0, 0)),
                  pl.BlockSpec((None, RET_CHUNK, RET_QK_DIM), lambda bi, h: (h, 0, 0)),
                  pl.BlockSpec((None, 1, RET_V_DIM), lambda bi, h: (h, 0, 0)),
                  pl.BlockSpec((1, RET_V_DIM), lambda bi, h: (0, h)),
                  pl.BlockSpec((1, RET_V_DIM), lambda bi, h: (0, h))],
        out_specs=pl.BlockSpec((None, s, RET_V_DIM), lambda bi, h: (bi, 0, h)),
        out_shape=jax.ShapeDtypeStruct((b, s, RET_V_W), BF16),
        compiler_params=_params(2),
        name="retention",
    )(rot3, rot3, plain3, plain3, decay, xi, zeta, cd, gn_g.reshape(1, -1), gn_b.reshape(1, -1))


def _merge_kernel(oa_ref, or_ref, wa_ref, wr_ref, ga_ref, gb_ref, o_ref, wa_bf, wr_bf):
    @pl.when(pl.program_id(1) == 0)
    def _():
        wa_bf[...] = wa_ref[...].astype(BF16)
        wr_bf[...] = wr_ref[...].astype(BF16)

    ya = jnp.dot(oa_ref[...], wa_bf[...], preferred_element_type=F32)
    yr = jnp.dot(or_ref[...], wr_bf[...], preferred_element_type=F32)
    ga = ga_ref[...].astype(F32)
    gb = gb_ref[...].astype(F32)
    sa = 1.0 / (1.0 + jnp.exp(-ga))
    sb = 1.0 / (1.0 + jnp.exp(-gb))
    o_ref[...] = (sa * ya + sb * yr).astype(o_ref.dtype)


def _merge(o_attn, o_ret, w_ua3, w_ur3, layer, plain, tm=1024, tn=512):
    m = o_attn.shape[0]
    ga_blk0 = (2 * RET_V_W) // tn
    gb_blk0 = (2 * RET_V_W + D_MODEL) // tn
    return pl.pallas_call(
        _merge_kernel,
        grid=(D_MODEL // tn, m // tm),
        in_specs=[pl.BlockSpec((tm, ATTN_W), lambda j, i: (i, 0)),
                  pl.BlockSpec((tm, RET_V_W), lambda j, i: (i, 0)),
                  pl.BlockSpec((None, ATTN_W, tn), lambda j, i: (layer, 0, j)),
                  pl.BlockSpec((None, RET_V_W, tn), lambda j, i: (layer, 0, j)),
                  pl.BlockSpec((tm, tn), lambda j, i: (i, ga_blk0 + j)),
                  pl.BlockSpec((tm, tn), lambda j, i: (i, gb_blk0 + j))],
        out_specs=pl.BlockSpec((tm, tn), lambda j, i: (i, j)),
        out_shape=jax.ShapeDtypeStruct((m, D_MODEL), BF16),
        scratch_shapes=[pltpu.VMEM((ATTN_W, tn), BF16), pltpu.VMEM((RET_V_W, tn), BF16)],
        compiler_params=_params(2),
        name="gated_merge",
    )(o_attn, o_ret, w_ua3, w_ur3, plain, plain)


def _rotary_tables(s):
    pos = jnp.arange(s, dtype=F32)
    inv_freq = ROPE_BASE ** (-jnp.arange(0, RET_QK_DIM, 2, dtype=F32) / RET_QK_DIM)
    ang = pos[:, None] * inv_freq[None, :]
    cos, sin = jnp.cos(ang), jnp.sin(ang)
    return jnp.concatenate([cos, cos], axis=1), jnp.concatenate([-sin, sin], axis=1)


def kernel(x, ln1_g, w_in, q_norm_g, k_norm_g, ret_gn_g, ret_gn_b, w_up_attn, w_up_ret, w_out, ln2_g,
           w_ff1, w_ff2):
    b, s, d = x.shape
    m = b * s
    depth = w_in.shape[0]
    topk = min(TOPK_MAX, s // 4)
    assert s % Q_SUPER == 0 and topk <= Q_SUPER
    cos2, sin2 = _rotary_tables(s)
    tm = 1024
    s_tiles = s // tm

    w_in_t = jnp.swapaxes(w_in, 1, 2)
    c_kv = ATTN_W
    c_iq = c_kv + 2 * KV_W
    c_rot = c_iq + IDX_Q_W + IDX_DIM + IDX_HEADS
    c_plain = c_rot + 2 * RET_QK_W
    in_proj = functools.partial(_matmul, w3=w_in_t, w_is_nk=True, tm=tm)

    xf = x.reshape(m, d)
    xg, ss = _norm_prep(xf, ln1_g[0])
    for l in range(depth):
        aq =in_proj(xg, layer=l, row_ss=ss, out_dtype=BF16, epilogue=_epi_headnorm, tn=1024, col0=0, n=ATTN_W,
                     name="proj_q",
                     extras=[(q_norm_g[l].reshape(1, HEAD_DIM), (1, HEAD_DIM), lambda i, j: (0, 0))])
        kv = in_proj(xg, layer=l, row_ss=ss, out_dtype=BF16, epilogue=_epi_kv, tn=KV_W, col0=c_kv, n=2 * KV_W,
                     name="proj_kv",
                     extras=[(k_norm_g[l].reshape(1, HEAD_DIM), (1, HEAD_DIM), lambda i, j: (0, 0))])
        idx = in_proj(xg, layer=l, row_ss=ss, out_dtype=F32, epilogue=_epi_plain, tn=IDX_PACK_W, col0=c_iq,
                      n=IDX_PACK_W, name="proj_idx")
        rot = in_proj(xg, layer=l, row_ss=ss, out_dtype=BF16, epilogue=_epi_rotary, tn=RET_QK_W, col0=c_rot,
                      n=2 * RET_QK_W, name="proj_rot",
                      extras=[(cos2, (tm, RET_QK_DIM), lambda i, j: (i % s_tiles, 0)),
                              (sin2, (tm, RET_QK_DIM), lambda i, j: (i % s_tiles, 0))])
        plain = in_proj(xg, layer=l, row_ss=ss, out_dtype=BF16, epilogue=_epi_plain, tn=1024, col0=c_plain,
                        n=PLAIN_W, name="proj_plain")

        aq3 = aq.reshape(b, s, ATTN_W)
        kv3 = kv.reshape(b, s, 2 * KV_W)
        idx3 = idx.reshape(b, s, IDX_PACK_W)
        plain3 = plain.reshape(b, s, PLAIN_W)
        rot3 = rot.reshape(b, s, 2 * RET_QK_W)

        qo3 = aq3
        for j in range(s // Q_SUPER):
            bias = _topk_mask(idx3, j=j, topk=topk)
            qo3 = _attention(qo3, kv3, bias, j=j)
        o_attn = qo3.reshape(m, ATTN_W)

        o_ret = _retention(rot3, plain3, ret_gn_g[l], ret_gn_b[l]).reshape(m, RET_V_W)

        merged = _merge(o_attn, o_ret, w_up_attn, w_up_ret, l, plain)
        xf, xg2, ss2 = _matmul(merged, w_out, l, out_dtype=F32, epilogue=_epi_residual, tm=tm, tn=1024,
                               name="proj_out", extras=[(xf, (tm, 1024), lambda i, j: (i, j))],
                               norm_gain=ln2_g[l].reshape(1, d), w_buffers=1)

        f, w_ff2_bf = _matmul(xg2, w_ff1, l, out_dtype=BF16, epilogue=_epi_relu2, tm=tm, tn=1024, row_ss=ss2,
                              side_cast=(w_ff2, l), name="ff1")
        next_gain = ln1_g[l + 1].reshape(1, d) if l + 1 < depth else None
        res = _matmul(f, w_ff2_bf[None], 0, out_dtype=F32, epilogue=_epi_residual, tm=tm, tn=1024, tk=2048,
                      name="ff2", extras=[(xf, (tm, 1024), lambda i, j: (i, j))], norm_gain=next_gain)
        xf, xg, ss = res if next_gain is not None else (res, None, None)
    return xf.reshape(b, s, d)
```

```python
import functools
import math

import jax
import jax.numpy as jnp
import numpy as np
from jax import lax
from jax.experimental import pallas as pl
from jax.experimental.pallas import tpu as pltpu

D_MODEL = 2048
HEAD_DIM = 128
ATTN_Q_HEADS = 16
ATTN_KV_HEADS = 4
ATTN_GROUP = ATTN_Q_HEADS // ATTN_KV_HEADS
IDX_HEADS = 16
IDX_DIM = 64
TOPK_MAX = 256
RET_HEADS = 8
RET_QK_DIM = 128
RET_V_DIM = 256
RET_CHUNK = 128
ROPE_BASE = 10000.0
D_FF = 4 * D_MODEL
EPS = 1e-6

ATTN_W = ATTN_Q_HEADS * HEAD_DIM
KV_W = ATTN_KV_HEADS * HEAD_DIM
IDX_Q_W = IDX_HEADS * IDX_DIM
RET_QK_W = RET_HEADS * RET_QK_DIM
RET_V_W = RET_HEADS * RET_V_DIM

LANES = 128
PLAIN_W = 2 * RET_V_W + 2 * D_MODEL
IDX_PACK_W = IDX_Q_W + LANES
COUNT_ROWS = 64
BISECT_MAX_STEPS = 2200
BISECT_UNROLL = 4
Q_SUPER = 512
HALF_ROWS = Q_SUPER // 2
NEG_BIG = -1e30
VMEM_LIMIT = 56 * 1024 * 1024

F32 = jnp.float32
BF16 = jnp.bfloat16


def _params(n_axes):
    return pltpu.CompilerParams(dimension_semantics=("arbitrary",) * n_axes,
                                vmem_limit_bytes=VMEM_LIMIT)


def _row_sumsq(x):
    return jnp.broadcast_to(jnp.sum(x * x, axis=-1, keepdims=True), (x.shape[0], LANES))


def _norm_prep_kernel(x_ref, g_ref, xg_ref, ss_ref):
    x = x_ref[...]
    xg_ref[...] = (x * g_ref[...]).astype(xg_ref.dtype)
    ss_ref[...] = _row_sumsq(x)


def _norm_prep(x, g, tm=512):
    m, d = x.shape
    return pl.pallas_call(
        _norm_prep_kernel,
        grid=(m // tm,),
        in_specs=[pl.BlockSpec((tm, d), lambda i: (i, 0)),
                  pl.BlockSpec((1, d), lambda i: (0, 0))],
        out_specs=[pl.BlockSpec((tm, d), lambda i: (i, 0)),
                   pl.BlockSpec((None, tm, LANES), lambda i: (0, i, 0))],
        out_shape=[jax.ShapeDtypeStruct((m, d), BF16), jax.ShapeDtypeStruct((1, m, LANES), F32)],
        compiler_params=_params(1),
        name="norm_prep",
    )(x, g.reshape(1, d))


def _mm_kernel(*refs, nk, n_extra, epilogue, cache_w, w_is_nk, row_scaled, norm_out, side_cast, d_norm):
    a_ref, w_ref = refs[0], refs[1]
    n_fixed = 2 + int(row_scaled) + n_extra
    n_in = n_fixed + int(norm_out) + int(side_cast)
    ss_in_ref = refs[2] if row_scaled else None
    extra = refs[2 + int(row_scaled):n_fixed]
    gain_ref = refs[n_fixed] if norm_out else None
    o_ref = refs[n_in]
    n_out = 1 + 2 * int(norm_out) + int(side_cast)
    scratch = refs[n_in + n_out:]
    i = pl.program_id(1)
    k = pl.program_id(2)
    if side_cast:
        refs[n_in + n_out - 1][...] = refs[n_in - 1][...].astype(BF16)
    if cache_w:
        wbf_ref = scratch[0]

        @pl.when(i == 0)
        def _():
            wbf_ref[k] = (w_ref[0].T if w_is_nk else w_ref[...]).astype(BF16)

        w = wbf_ref[k]
    else:
        w = w_ref[...]
    prod = jnp.dot(a_ref[...], w, preferred_element_type=F32)

    def finish(acc):
        if row_scaled:
            r = lax.rsqrt(jnp.sum(ss_in_ref[...], axis=0) * (1.0 / d_norm) + EPS)
            acc = acc * jnp.concatenate([r] * (acc.shape[1] // LANES), axis=1)
        out = epilogue(acc, *extra)
        o_ref[...] = out.astype(o_ref.dtype)
        if norm_out:
            xg_ref, ss_out_ref = refs[n_in + 1], refs[n_in + 2]
            xg_ref[...] = (out * gain_ref[...]).astype(xg_ref.dtype)
            ss_out_ref[...] = _row_sumsq(out)

    if nk == 1:
        finish(prod)
        return
    acc_ref = scratch[-1]

    @pl.when(k == 0)
    def _():
        acc_ref[...] = prod

    @pl.when((k > 0) & (k < nk - 1))
    def _():
        acc_ref[...] += prod

    @pl.when(k == nk - 1)
    def _():
        finish(acc_ref[...] + prod)


def _matmul(a, w3, layer, *, out_dtype, epilogue, extras=(), tm=1024, tn=1024, tk=None, col0=0, n=None,
            w_is_nk=False, row_ss=None, norm_gain=None, w_buffers=2, side_cast=None, name="matmul"):
    m, kdim = a.shape
    n_total = w3.shape[1] if w_is_nk else w3.shape[2]
    n = n_total - col0 if n is None else n
    tk = kdim if tk is None else tk
    nk = kdim // tk
    tn = min(tn, n)
    assert m % tm == 0 and n % tn == 0 and kdim % tk == 0
    cache_w = w3.dtype != BF16
    if w_is_nk:
        assert cache_w and nk == 1 and col0 % 8 == 0
        w_spec = pl.BlockSpec((pl.Element(1), pl.Element(tn), pl.Element(tk)),
                              lambda j, i, k: (layer, pl.multiple_of(col0 + j * tn, 8), 0))
    else:
        assert col0 % tn == 0
        jb0 = col0 // tn
        if cache_w:
            w_map = lambda j, i, k: (layer, jnp.where(i == 0, k, nk - 1), jb0 + j)
        else:
            w_map = lambda j, i, k: (layer, k, jb0 + j)
        mode = {} if w_buffers == 2 else {"pipeline_mode": pl.Buffered(w_buffers)}
        w_spec = pl.BlockSpec((None, tk, tn), w_map, **mode)
    operands = [a, w3]
    in_specs = [pl.BlockSpec((tm, tk), lambda j, i, k: (i, k)), w_spec]
    if row_ss is not None:
        operands.append(row_ss)
        in_specs.append(pl.BlockSpec((row_ss.shape[0], tm, LANES), lambda j, i, k: (0, i, 0)))
    for arr, bshape, imap in extras:
        operands.append(arr)
        in_specs.append(pl.BlockSpec(bshape, functools.partial(lambda j, i, k, f: f(i, j), f=imap)))
    out_specs = [pl.BlockSpec((tm, tn), lambda j, i, k: (i, j))]
    out_shape = [jax.ShapeDtypeStruct((m, n), out_dtype)]
    if norm_gain is not None:
        operands.append(norm_gain)
        in_specs.append(pl.BlockSpec((1, tn), lambda j, i, k: (0, j)))
        out_specs += [pl.BlockSpec((tm, tn), lambda j, i, k: (i, j)),
                      pl.BlockSpec((None, tm, LANES), lambda j, i, k: (j, i, 0))]
        out_shape += [jax.ShapeDtypeStruct((m, n), BF16), jax.ShapeDtypeStruct((n // tn, m, LANES), F32)]
    if side_cast is not None:
        other3, other_layer = side_cast
        steps, ni = (n // tn) * (m // tm), m // tm
        _, rows_o, cols_o = other3.shape
        assert nk == 1 and rows_o % steps == 0 and (rows_o // steps) % 16 == 0
        slab = rows_o // steps
        operands.append(other3)
        in_specs.append(pl.BlockSpec((None, slab, cols_o), lambda j, i, k: (other_layer, j * ni + i, 0)))
        out_specs.append(pl.BlockSpec((slab, cols_o), lambda j, i, k: (j * ni + i, 0)))
        out_shape.append(jax.ShapeDtypeStruct((rows_o, cols_o), BF16))
    scratch = []
    if cache_w:
        scratch.append(pltpu.VMEM((nk, tk, tn), BF16))
    if nk > 1:
        scratch.append(pltpu.VMEM((tm, tn), F32))
    res = pl.pallas_call(
        functools.partial(_mm_kernel, nk=nk, n_extra=len(extras), epilogue=epilogue, cache_w=cache_w,
                          w_is_nk=w_is_nk, row_scaled=row_ss is not None, norm_out=norm_gain is not None,
                          side_cast=side_cast is not None, d_norm=kdim),
        grid=(n // tn, m // tm, nk),
        in_specs=in_specs,
        out_specs=out_specs,
        out_shape=out_shape,
        scratch_shapes=scratch,
        compiler_params=_params(3),
        name=name,
    )(*operands)
    return res if len(res) > 1 else res[0]


def _epi_plain(acc):
    return acc


def _epi_headnorm(acc, g_ref):
    g = g_ref[...]
    outs = []
    for h in range(acc.shape[1] // HEAD_DIM):
        a = acc[:, h * HEAD_DIM:(h + 1) * HEAD_DIM]
        r = lax.rsqrt(jnp.mean(a * a, axis=-1, keepdims=True) + EPS)
        outs.append((a * r) * g)
    return jnp.concatenate(outs, axis=1)


def _epi_kv(acc, g_ref):
    return jnp.concatenate([_epi_headnorm(acc[:, :KV_W], g_ref), acc[:, KV_W:]], axis=1)


def _epi_rotary(acc, cos_ref, sin_ref):
    c = cos_ref[...]
    s = sin_ref[...]
    scale = jnp.where(pl.program_id(0) == 0, 1.0, RET_QK_DIM ** -0.5).astype(F32)
    outs = []
    for h in range(acc.shape[1] // RET_QK_DIM):
        a = acc[:, h * RET_QK_DIM:(h + 1) * RET_QK_DIM]
        swapped = pltpu.roll(a, RET_QK_DIM // 2, axis=1)
        outs.append((a * c + swapped * s) * scale)
    return jnp.concatenate(outs, axis=1)


def _epi_residual(acc, x_ref):
    return x_ref[...] + acc


def _epi_relu2(acc):
    r = jnp.maximum(acc, 0.0)
    return r * r


def _topk_tile(iq_ref, qpack_ref, kpack_ref, bias_ref, score_ref, sel_ref, *, qbase, keys, tq, topk):
    rows = COUNT_ROWS
    neg_inf = float("-inf")

    ik = kpack_ref[:keys, :IDX_DIM].astype(BF16)
    w_t = qpack_ref[...].T * (IDX_HEADS ** -0.5)
    score = None
    for h in range(IDX_HEADS):
        iqh = iq_ref[:, h * IDX_DIM:(h + 1) * IDX_DIM].astype(BF16)
        logits = lax.dot_general(ik, iqh, (((1,), (1,)), ((), ())), preferred_element_type=F32)
        term = jnp.maximum(logits, 0.0) * w_t[IDX_DIM + h:IDX_DIM + h + 1, :]
        score = term if score is None else score + term

    def causal_mask():
        kpos = lax.broadcasted_iota(jnp.int32, (keys, tq), 0)
        qcol = lax.broadcasted_iota(jnp.int32, (keys, tq), 1) + qbase
        return kpos <= qcol

    causal = causal_mask()
    score = jnp.where(score == 0.0, 0.0, score)
    lo_fill = jnp.where(causal, score, neg_inf)
    score_ref[:keys, :] = lo_fill
    smax = jnp.max(jnp.max(lo_fill.reshape(keys // rows, rows, tq), axis=0), axis=0, keepdims=True)
    hi_fill = jnp.where(causal, score, float("inf"))
    smin = jnp.min(jnp.min(hi_fill.reshape(keys // rows, rows, tq), axis=0), axis=0, keepdims=True)
    n_causal = jnp.minimum(lax.broadcasted_iota(jnp.int32, (1, tq), 1) + (qbase + 1), keys)

    def count_part(thr, c0, width):
        thr_b = jnp.broadcast_to(thr, (rows, width))
        part = jnp.zeros((rows, width), F32)
        for i in range(keys // rows):
            part = part + jnp.where(score_ref[i * rows:(i + 1) * rows, c0:c0 + width] >= thr_b, 1.0, 0.0)
        return part

    kf = float(topk)
    hw = tq // 2

    def step(part, state, first):
        lo, hi, tau, act, mid, odd = state
        cnt = jnp.sum(part, axis=0, keepdims=True)
        ge = cnt >= kf
        hit = (cnt == kf) | (first & ge)
        lo = jnp.where(ge, mid, lo)
        hi = jnp.where(ge, hi, mid)
        nxt = 0.5 * lo + 0.5 * hi
        stuck = (nxt <= lo) | (nxt >= hi)
        live = act > 0
        tau = jnp.where(live & hit, mid, jnp.where(live & stuck, lo, tau))
        inexact = (hit & (cnt != kf)) | (jnp.logical_not(hit) & stuck)
        odd = jnp.where(live & inexact, 1, odd)
        act = jnp.where(hit | stuck, 0, act)
        return lo, hi, tau, act, nxt, odd

    def cond(c):
        return (c[1] > 0) & (c[0] < BISECT_MAX_STEPS)

    def body(c):
        it, _, part_a, st_a, st_b = c
        for u in range(BISECT_UNROLL):
            first = ((jnp.zeros((1, hw), jnp.int32) + it) == 0) if u == 0 else jnp.zeros((1, hw), jnp.bool_)
            st_a = step(part_a, st_a, first)
            part_b = count_part(st_b[4], hw, hw)
            st_b = step(part_b, st_b, first)
            part_a = count_part(st_a[4], 0, hw)
        return it + BISECT_UNROLL, jnp.sum(st_a[3]) + jnp.sum(st_b[3]), part_a, st_a, st_b

    lo0 = jnp.where(n_causal >= topk, smin, neg_inf)

    def init_state(c0):
        return (lo0[:, c0:c0 + hw], smax[:, c0:c0 + hw], jnp.full((1, hw), neg_inf, F32),
                jnp.ones((1, hw), jnp.int32), smax[:, c0:c0 + hw], jnp.zeros((1, hw), jnp.int32))

    st_a0, st_b0 = init_state(0), init_state(hw)
    out = lax.while_loop(cond, body, (jnp.int32(0), jnp.int32(tq), count_part(st_a0[4], 0, hw), st_a0, st_b0))
    tau = jnp.concatenate([out[3][2], out[4][2]], axis=1)

    odd = jnp.concatenate([out[3][5], out[4][5]], axis=1)

    picked = jnp.where(score_ref[:keys, :] >= tau, 0.0, NEG_BIG)
    bias_ref[:keys, :] = jnp.where(causal_mask(), picked, NEG_BIG).astype(bias_ref.dtype)

    @pl.when(jnp.max(jnp.where(tau > neg_inf, odd, 0)) > 0)
    def _():
        tc = 256
        row = lax.broadcasted_iota(jnp.int32, (tc, tc), 0)
        col = lax.broadcasted_iota(jnp.int32, (tc, tc), 1)
        lower = jnp.where(col <= row, 1.0, 0.0).astype(BF16)
        n_gt = jnp.sum(jnp.where(score_ref[:keys, :] > tau, 1.0, 0.0), axis=0, keepdims=True)
        need = kf - n_gt
        carry = jnp.zeros((1, tq), F32)
        for c in range(keys // tc):
            sc = score_ref[c * tc:(c + 1) * tc, :]
            eq = jnp.where(sc == tau, 1.0, 0.0)
            prefix = jnp.dot(lower, eq.astype(BF16), preferred_element_type=F32) + carry
            carry = carry + jnp.sum(eq, axis=0, keepdims=True)
            keep = (sc > tau) | ((sc == tau) & (prefix <= need))
            sel_ref[c * tc:(c + 1) * tc, :] = jnp.where(keep, 0.0, NEG_BIG)
        bias_ref[:keys, :] = jnp.where(causal_mask(), sel_ref[:keys, :], NEG_BIG).astype(bias_ref.dtype)


def _topk_mask_kernel(iq_ref, qpack_ref, kpack_ref, bias_ref, score_ref, sel_ref, *, q0, kend, tq, topk):
    half = pl.program_id(1)
    for which, keys in ((0, kend - tq), (1, kend)):
        @pl.when(half == which)
        def _():
            _topk_tile(iq_ref, qpack_ref, kpack_ref, bias_ref, score_ref, sel_ref,
                       qbase=q0 + which * tq, keys=keys, tq=tq, topk=topk)
            if keys < kend:
                bias_ref[keys:, :] = jnp.full((kend - keys, tq), NEG_BIG, bias_ref.dtype)


def _topk_mask(idx3, *, j, topk):
    b = idx3.shape[0]
    kend = Q_SUPER * (j + 1)
    q0 = Q_SUPER * j
    tq = HALF_ROWS
    nq = Q_SUPER // tq
    pack_blk = IDX_Q_W // LANES
    return pl.pallas_call(
        functools.partial(_topk_mask_kernel, q0=q0, kend=kend, tq=tq, topk=topk),
        grid=(b, nq),
        in_specs=[pl.BlockSpec((None, tq, IDX_Q_W), lambda bi, qi: (bi, q0 // tq + qi, 0)),
                  pl.BlockSpec((None, tq, LANES), lambda bi, qi: (bi, q0 // tq + qi, pack_blk)),
                  pl.BlockSpec((None, kend, LANES), lambda bi, qi: (bi, 0, pack_blk))],
        out_specs=pl.BlockSpec((None, kend, tq), lambda bi, qi: (bi, 0, qi)),
        out_shape=jax.ShapeDtypeStruct((b, kend, Q_SUPER), BF16),
        scratch_shapes=[pltpu.VMEM((kend, tq), F32), pltpu.VMEM((kend, tq), F32)],
        compiler_params=_params(2),
        name=f"topk_mask_{j}",
    )(idx3, idx3, idx3)


def _attn_tile(q_ref, k_ref, v_ref, bias_t_ref, o_ref, *, row0, keys):
    tq = HEAD_DIM
    rows = slice(row0, row0 + tq)
    exp2_scale = (HEAD_DIM ** -0.5) * math.log2(math.e)
    bias_t = bias_t_ref[:keys, rows]
    row = lax.broadcasted_iota(jnp.int32, (tq, tq), 0)
    col = lax.broadcasted_iota(jnp.int32, (tq, tq), 1)
    eye = jnp.where(row == col, 1.0, 0.0).astype(BF16)
    eye_rows = jnp.concatenate([eye] * ATTN_GROUP, axis=0)
    ones = jnp.ones((keys, HEAD_DIM), BF16)
    for g in range(ATTN_KV_HEADS):
        kg = k_ref[:keys, g * HEAD_DIM:(g + 1) * HEAD_DIM]
        vg = v_ref[:keys, g * HEAD_DIM:(g + 1) * HEAD_DIM]
        h0 = g * ATTN_GROUP
        qg = jnp.concatenate([q_ref[rows, (h0 + r) * HEAD_DIM:(h0 + r + 1) * HEAD_DIM]
                              for r in range(ATTN_GROUP)], axis=0)
        q_ext = jnp.concatenate([qg, eye_rows], axis=1)
        k_ext = jnp.concatenate([kg, bias_t], axis=1)
        t = lax.dot_general(q_ext, k_ext, (((1,), (1,)), ((), ())), preferred_element_type=F32)
        m = jnp.max(t, axis=-1, keepdims=True)
        p = jnp.exp2((t - m) * exp2_scale).astype(BF16)
        v_ext = jnp.concatenate([vg, ones], axis=1)
        o_ext = jnp.dot(p, v_ext, preferred_element_type=F32)
        o = o_ext[:, :HEAD_DIM] / o_ext[:, HEAD_DIM:]
        for r in range(ATTN_GROUP):
            o_ref[rows, (h0 + r) * HEAD_DIM:(h0 + r + 1) * HEAD_DIM] = o[r * tq:(r + 1) * tq].astype(o_ref.dtype)


def _attn_kernel(q_ref, k_ref, v_ref, bias_t_ref, o_ref, *, kend):
    for row0 in range(0, Q_SUPER, HEAD_DIM):
        keys = kend - HALF_ROWS if row0 < HALF_ROWS else kend
        _attn_tile(q_ref, k_ref, v_ref, bias_t_ref, o_ref, row0=row0, keys=keys)


def _attention(qo3, kv3, bias_t, *, j):
    b, s, _ = qo3.shape
    kend = Q_SUPER * (j + 1)
    return pl.pallas_call(
        functools.partial(_attn_kernel, kend=kend),
        grid=(b,),
        in_specs=[pl.BlockSpec((None, Q_SUPER, ATTN_W), lambda bi: (bi, j, 0)),
                  pl.BlockSpec((None, kend, KV_W), lambda bi: (bi, 0, 0)),
                  pl.BlockSpec((None, kend, KV_W), lambda bi: (bi, 0, 1)),
                  pl.BlockSpec((None, kend, Q_SUPER), lambda bi: (bi, 0, 0))],
        out_specs=pl.BlockSpec((None, Q_SUPER, ATTN_W), lambda bi: (bi, j, 0)),
        out_shape=jax.ShapeDtypeStruct((b, s, ATTN_W), BF16),
        input_output_aliases={0: 0},
        compiler_params=_params(1),
        name=f"attention_{j}",
    )(qo3, kv3, kv3, bias_t)


def _retention_kernel(q_ref, k_ref, v_ref, gate_ref, decay_ref, xi_ref, zeta_ref, cd_ref, g_ref, b_ref,
                      o_ref, *, n_chunks):
    c = RET_CHUNK
    decay = decay_ref[...]
    xi = xi_ref[...]
    zeta = zeta_ref[...]
    cd = cd_ref[...]
    gn_g = g_ref[...]
    gn_b = b_ref[...]
    state = jnp.zeros((RET_QK_DIM, RET_V_DIM), F32)
    for n in range(n_chunks):
        sl = slice(n * c, (n + 1) * c)
        q = q_ref[sl, :]
        k = k_ref[sl, :]
        v = v_ref[sl, :]
        qk = lax.dot_general(q, k, (((1,), (1,)), ((), ())), preferred_element_type=F32) * decay
        inner = jnp.dot(qk.astype(BF16), v, preferred_element_type=F32)
        cross = jnp.dot(q, state.astype(BF16), preferred_element_type=F32) * xi
        kz_t = (k.astype(F32) * zeta).T.astype(BF16)
        upd = jnp.dot(kz_t, v, preferred_element_type=F32)
        state = upd + cd * state
        y = inner + cross
        mu = jnp.mean(y, axis=-1, keepdims=True)
        d = y - mu
        var = jnp.mean(d * d, axis=-1, keepdims=True)
        yn = d * lax.rsqrt(var + EPS)
        z = yn * gn_g + gn_b
        gate = gate_ref[sl, :].astype(F32)
        silu = gate * (1.0 / (1.0 + jnp.exp(-gate)))
        o_ref[sl, :] = (silu * z).astype(o_ref.dtype)


def _retention_tables():
    c = RET_CHUNK
    lg = np.log(1.0 - np.exp2(-5.0 - np.arange(RET_HEADS, dtype=np.float32))).astype(np.float32)
    pos = np.arange(c, dtype=np.float32)
    diff = pos[:, None] - pos[None, :]
    decay = np.where(diff[None] >= 0, np.exp(lg[:, None, None] * np.maximum(diff, 0.0)[None]), 0.0)
    xi = np.exp(lg[:, None] * (pos[None, :] + 1.0))
    zeta = np.exp(lg[:, None] * (c - 1.0 - pos[None, :]))
    cd = np.exp(lg * c)
    f = lambda a: jnp.asarray(a.astype(np.float32))
    return (f(decay),
            f(np.broadcast_to(xi[:, :, None], (RET_HEADS, c, RET_V_DIM))),
            f(np.broadcast_to(zeta[:, :, None], (RET_HEADS, c, RET_QK_DIM))),
            f(np.broadcast_to(cd[:, None, None], (RET_HEADS, 1, RET_V_DIM))))


def _retention(rot3, plain3, gn_g, gn_b):
    b, s, _ = rot3.shape
    decay, xi, zeta, cd = _retention_tables()
    v_blk0 = 0
    gate_blk0 = RET_V_W // RET_V_DIM
    return pl.pallas_call(
        functools.partial(_retention_kernel, n_chunks=s // RET_CHUNK),
        grid=(b, RET_HEADS),
        in_specs=[pl.BlockSpec((None, s, RET_QK_DIM), lambda bi, h: (bi, 0, h)),
                  pl.BlockSpec((None, s, RET_QK_DIM), lambda bi, h: (bi, 0, RET_HEADS + h)),
                  pl.BlockSpec((None, s, RET_V_DIM), lambda bi, h: (bi, 0, v_blk0 + h)),
                  pl.BlockSpec((None, s, RET_V_DIM), lambda bi, h: (bi, 0, gate_blk0 + h)),
                  pl.BlockSpec((None, RET_CHUNK, RET_CHUNK), lambda bi, h: (h, 0, 0)),
                  pl.BlockSpec((None, RET_CHUNK, RET_V_DIM), lambda bi, h: (h, 0, 0)),
                  pl.BlockSpec((None, RET_CHUNK, RET_QK_DIM), lambda bi, h: (h, 0, 0)),
                  pl.BlockSpec((None, 1, RET_V_DIM), lambda bi, h: (h, 0, 0)),
                  pl.BlockSpec((1, RET_V_DIM), lambda bi, h: (0, h)),
                  pl.BlockSpec((1, RET_V_DIM), lambda bi, h: (0, h))],
        out_specs=pl.BlockSpec((None, s, RET_V_DIM), lambda bi, h: (bi, 0, h)),
        out_shape=jax.ShapeDtypeStruct((b, s, RET_V_W), BF16),
        compiler_params=_params(2),
        name="retention",
    )(rot3, rot3, plain3, plain3, decay, xi, zeta, cd, gn_g.reshape(1, -1), gn_b.reshape(1, -1))


def _merge_kernel(oa_ref, or_ref, wa_ref, wr_ref, ga_ref, gb_ref, o_ref, wa_bf, wr_bf):
    @pl.when(pl.program_id(1) == 0)
    def _():
        wa_bf[...] = wa_ref[...].astype(BF16)
        wr_bf[...] = wr_ref[...].astype(BF16)

    ya = jnp.dot(oa_ref[...], wa_bf[...], preferred_element_type=F32)
    yr = jnp.dot(or_ref[...], wr_bf[...], preferred_element_type=F32)
    ga = ga_ref[...].astype(F32)
    gb = gb_ref[...].astype(F32)
    sa = 1.0 / (1.0 + jnp.exp(-ga))
    sb = 1.0 / (1.0 + jnp.exp(-gb))
    o_ref[...] = (sa * ya + sb * yr).astype(o_ref.dtype)


def _merge(o_attn, o_ret, w_ua3, w_ur3, layer, plain, tm=1024, tn=512):
    m = o_attn.shape[0]
    ga_blk0 = (2 * RET_V_W) // tn
    gb_blk0 = (2 * RET_V_W + D_MODEL) // tn
    return pl.pallas_call(
        _merge_kernel,
        grid=(D_MODEL // tn, m // tm),
        in_specs=[pl.BlockSpec((tm, ATTN_W), lambda j, i: (i, 0)),
                  pl.BlockSpec((tm, RET_V_W), lambda j, i: (i, 0)),
                  pl.BlockSpec((None, ATTN_W, tn), lambda j, i: (layer, 0, j)),
                  pl.BlockSpec((None, RET_V_W, tn), lambda j, i: (layer, 0, j)),
                  pl.BlockSpec((tm, tn), lambda j, i: (i, ga_blk0 + j)),
                  pl.BlockSpec((tm, tn), lambda j, i: (i, gb_blk0 + j))],
        out_specs=pl.BlockSpec((tm, tn), lambda j, i: (i, j)),
        out_shape=jax.ShapeDtypeStruct((m, D_MODEL), BF16),
        scratch_shapes=[pltpu.VMEM((ATTN_W, tn), BF16), pltpu.VMEM((RET_V_W, tn), BF16)],
        compiler_params=_params(2),
        name="gated_merge",
    )(o_attn, o_ret, w_ua3, w_ur3, plain, plain)


def _rotary_tables(s):
    pos = jnp.arange(s, dtype=F32)
    inv_freq = ROPE_BASE ** (-jnp.arange(0, RET_QK_DIM, 2, dtype=F32) / RET_QK_DIM)
    ang = pos[:, None] * inv_freq[None, :]
    cos, sin = jnp.cos(ang), jnp.sin(ang)
    return jnp.concatenate([cos, cos], axis=1), jnp.concatenate([-sin, sin], axis=1)


def kernel(x, ln1_g, w_in, q_norm_g, k_norm_g, ret_gn_g, ret_gn_b, w_up_attn, w_up_ret, w_out, ln2_g,
           w_ff1, w_ff2):
    b, s, d = x.shape
    m = b * s
    depth = w_in.shape[0]
    topk = min(TOPK_MAX, s // 4)
    assert s % Q_SUPER == 0 and topk <= Q_SUPER
    cos2, sin2 = _rotary_tables(s)
    tm = 1024
    s_tiles = s // tm

    w_in_t = jnp.swapaxes(w_in, 1, 2)
    c_kv = ATTN_W
    c_iq = c_kv + 2 * KV_W
    c_rot = c_iq + IDX_Q_W + IDX_DIM + IDX_HEADS
    c_plain = c_rot + 2 * RET_QK_W
    in_proj = functools.partial(_matmul, w3=w_in_t, w_is_nk=True, tm=tm)

    xf = x.reshape(m, d)
    xg, ss = _norm_prep(xf, ln1_g[0])
    for l in range(depth):
        aq =in_proj(xg, layer=l, row_ss=ss, out_dtype=BF16, epilogue=_epi_headnorm, tn=1024, col0=0, n=ATTN_W,
                     name="proj_q",
                     extras=[(q_norm_g[l].reshape(1, HEAD_DIM), (1, HEAD_DIM), lambda i, j: (0, 0))])
        kv = in_proj(xg, layer=l, row_ss=ss, out_dtype=BF16, epilogue=_epi_kv, tn=2 * KV_W, col0=c_kv, n=2 * KV_W,
                     name="proj_kv",
                     extras=[(k_norm_g[l].reshape(1, HEAD_DIM), (1, HEAD_DIM), lambda i, j: (0, 0))])
        idx = in_proj(xg, layer=l, row_ss=ss, out_dtype=F32, epilogue=_epi_plain, tn=IDX_PACK_W, col0=c_iq,
                      n=IDX_PACK_W, name="proj_idx")
        rot = in_proj(xg, layer=l, row_ss=ss, out_dtype=BF16, epilogue=_epi_rotary, tn=RET_QK_W, col0=c_rot,
                      n=2 * RET_QK_W, name="proj_rot",
                      extras=[(cos2, (tm, RET_QK_DIM), lambda i, j: (i % s_tiles, 0)),
                              (sin2, (tm, RET_QK_DIM), lambda i, j: (i % s_tiles, 0))])
        plain = in_proj(xg, layer=l, row_ss=ss, out_dtype=BF16, epilogue=_epi_plain, tn=1024, col0=c_plain,
                        n=PLAIN_W, name="proj_plain")

        aq3 = aq.reshape(b, s, ATTN_W)
        kv3 = kv.reshape(b, s, 2 * KV_W)
        idx3 = idx.reshape(b, s, IDX_PACK_W)
        plain3 = plain.reshape(b, s, PLAIN_W)
        rot3 = rot.reshape(b, s, 2 * RET_QK_W)

        qo3 = aq3
        for j in range(s // Q_SUPER):
            bias = _topk_mask(idx3, j=j, topk=topk)
            qo3 = _attention(qo3, kv3, bias, j=j)
        o_attn = qo3.reshape(m, ATTN_W)

        o_ret = _retention(rot3, plain3, ret_gn_g[l], ret_gn_b[l]).reshape(m, RET_V_W)

        merged = _merge(o_attn, o_ret, w_up_attn, w_up_ret, l, plain)
        xf, xg2, ss2 = _matmul(merged, w_out, l, out_dtype=F32, epilogue=_epi_residual, tm=tm, tn=1024,
                               name="proj_out", extras=[(xf, (tm, 1024), lambda i, j: (i, j))],
                               norm_gain=ln2_g[l].reshape(1, d), w_buffers=1)

        f, w_ff2_bf = _matmul(xg2, w_ff1, l, out_dtype=BF16, epilogue=_epi_relu2, tm=tm, tn=1024, row_ss=ss2,
                              side_cast=(w_ff2, l), name="ff1")
        next_gain = ln1_g[l + 1].reshape(1, d) if l + 1 < depth else None
        res = _matmul(f, w_ff2_bf[None], 0, out_dtype=F32, epilogue=_epi_residual, tm=tm, tn=1024, tk=2048,
                      name="ff2", extras=[(xf, (tm, 1024), lambda i, j: (i, j))], norm_gain=next_gain)
        xf, xg, ss = res if next_gain is not None else (res, None, None)
    return xf.reshape(b, s, d)
```

```python
import functools
import math

import jax
import jax.numpy as jnp
import numpy as np
from jax import lax
from jax.experimental import pallas as pl
from jax.experimental.pallas import tpu as pltpu

D_MODEL = 2048
HEAD_DIM = 128
ATTN_Q_HEADS = 16
ATTN_KV_HEADS = 4
ATTN_GROUP = ATTN_Q_HEADS // ATTN_KV_HEADS
IDX_HEADS = 16
IDX_DIM = 64
TOPK_MAX = 256
RET_HEADS = 8
RET_QK_DIM = 128
RET_V_DIM = 256
RET_CHUNK = 128
ROPE_BASE = 10000.0
D_FF = 4 * D_MODEL
EPS = 1e-6

ATTN_W = ATTN_Q_HEADS * HEAD_DIM
KV_W = ATTN_KV_HEADS * HEAD_DIM
IDX_Q_W = IDX_HEADS * IDX_DIM
RET_QK_W = RET_HEADS * RET_QK_DIM
RET_V_W = RET_HEADS * RET_V_DIM

LANES = 128
PLAIN_W = 2 * RET_V_W + 2 * D_MODEL
IDX_PACK_W = IDX_Q_W + LANES
COUNT_ROWS = 64
BISECT_MAX_STEPS = 2200
BISECT_UNROLL = 4
Q_SUPER = 512
HALF_ROWS = Q_SUPER // 2
NEG_BIG = -1e30
VMEM_LIMIT = 56 * 1024 * 1024

F32 = jnp.float32
BF16 = jnp.bfloat16


def _params(n_axes):
    return pltpu.CompilerParams(dimension_semantics=("arbitrary",) * n_axes,
                                vmem_limit_bytes=VMEM_LIMIT)


def _row_sumsq(x):
    return jnp.broadcast_to(jnp.sum(x * x, axis=-1, keepdims=True), (x.shape[0], LANES))


def _norm_prep_kernel(x_ref, g_ref, xg_ref, ss_ref):
    x = x_ref[...]
    xg_ref[...] = (x * g_ref[...]).astype(xg_ref.dtype)
    ss_ref[...] = _row_sumsq(x)


def _norm_prep(x, g, tm=1024):
    m, d = x.shape
    return pl.pallas_call(
        _norm_prep_kernel,
        grid=(m // tm,),
        in_specs=[pl.BlockSpec((tm, d), lambda i: (i, 0)),
                  pl.BlockSpec((1, d), lambda i: (0, 0))],
        out_specs=[pl.BlockSpec((tm, d), lambda i: (i, 0)),
                   pl.BlockSpec((None, tm, LANES), lambda i: (0, i, 0))],
        out_shape=[jax.ShapeDtypeStruct((m, d), BF16), jax.ShapeDtypeStruct((1, m, LANES), F32)],
        compiler_params=_params(1),
        name="norm_prep",
    )(x, g.reshape(1, d))


def _mm_kernel(*refs, nk, n_extra, epilogue, cache_w, w_is_nk, row_scaled, norm_out, side_cast, d_norm):
    a_ref, w_ref = refs[0], refs[1]
    n_fixed = 2 + int(row_scaled) + n_extra
    n_in = n_fixed + int(norm_out) + int(side_cast)
    ss_in_ref = refs[2] if row_scaled else None
    extra = refs[2 + int(row_scaled):n_fixed]
    gain_ref = refs[n_fixed] if norm_out else None
    o_ref = refs[n_in]
    n_out = 1 + 2 * int(norm_out) + int(side_cast)
    scratch = refs[n_in + n_out:]
    i = pl.program_id(1)
    k = pl.program_id(2)
    if side_cast:
        refs[n_in + n_out - 1][...] = refs[n_in - 1][...].astype(BF16)
    if cache_w:
        wbf_ref = scratch[0]

        @pl.when(i == 0)
        def _():
            wbf_ref[k] = (w_ref[0].T if w_is_nk else w_ref[...]).astype(BF16)

        w = wbf_ref[k]
    else:
        w = w_ref[...]
    prod = jnp.dot(a_ref[...], w, preferred_element_type=F32)

    def finish(acc):
        if row_scaled:
            r = lax.rsqrt(jnp.sum(ss_in_ref[...], axis=0) * (1.0 / d_norm) + EPS)
            acc = acc * jnp.concatenate([r] * (acc.shape[1] // LANES), axis=1)
        out = epilogue(acc, *extra)
        o_ref[...] = out.astype(o_ref.dtype)
        if norm_out:
            xg_ref, ss_out_ref = refs[n_in + 1], refs[n_in + 2]
            xg_ref[...] = (out * gain_ref[...]).astype(xg_ref.dtype)
            ss_out_ref[...] = _row_sumsq(out)

    if nk == 1:
        finish(prod)
        return
    acc_ref = scratch[-1]

    @pl.when(k == 0)
    def _():
        acc_ref[...] = prod

    @pl.when((k > 0) & (k < nk - 1))
    def _():
        acc_ref[...] += prod

    @pl.when(k == nk - 1)
    def _():
        finish(acc_ref[...] + prod)


def _matmul(a, w3, layer, *, out_dtype, epilogue, extras=(), tm=1024, tn=1024, tk=None, col0=0, n=None,
            w_is_nk=False, row_ss=None, norm_gain=None, w_buffers=2, side_cast=None, name="matmul"):
    m, kdim = a.shape
    n_total = w3.shape[1] if w_is_nk else w3.shape[2]
    n = n_total - col0 if n is None else n
    tk = kdim if tk is None else tk
    nk = kdim // tk
    tn = min(tn, n)
    assert m % tm == 0 and n % tn == 0 and kdim % tk == 0
    cache_w = w3.dtype != BF16
    if w_is_nk:
        assert cache_w and nk == 1 and col0 % 8 == 0
        w_spec = pl.BlockSpec((pl.Element(1), pl.Element(tn), pl.Element(tk)),
                              lambda j, i, k: (layer, pl.multiple_of(col0 + j * tn, 8), 0))
    else:
        assert col0 % tn == 0
        jb0 = col0 // tn
        if cache_w:
            w_map = lambda j, i, k: (layer, jnp.where(i == 0, k, nk - 1), jb0 + j)
        else:
            w_map = lambda j, i, k: (layer, k, jb0 + j)
        mode = {} if w_buffers == 2 else {"pipeline_mode": pl.Buffered(w_buffers)}
        w_spec = pl.BlockSpec((None, tk, tn), w_map, **mode)
    operands = [a, w3]
    in_specs = [pl.BlockSpec((tm, tk), lambda j, i, k: (i, k)), w_spec]
    if row_ss is not None:
        operands.append(row_ss)
        in_specs.append(pl.BlockSpec((row_ss.shape[0], tm, LANES), lambda j, i, k: (0, i, 0)))
    for arr, bshape, imap in extras:
        operands.append(arr)
        in_specs.append(pl.BlockSpec(bshape, functools.partial(lambda j, i, k, f: f(i, j), f=imap)))
    out_specs = [pl.BlockSpec((tm, tn), lambda j, i, k: (i, j))]
    out_shape = [jax.ShapeDtypeStruct((m, n), out_dtype)]
    if norm_gain is not None:
        operands.append(norm_gain)
        in_specs.append(pl.BlockSpec((1, tn), lambda j, i, k: (0, j)))
        out_specs += [pl.BlockSpec((tm, tn), lambda j, i, k: (i, j)),
                      pl.BlockSpec((None, tm, LANES), lambda j, i, k: (j, i, 0))]
        out_shape += [jax.ShapeDtypeStruct((m, n), BF16), jax.ShapeDtypeStruct((n // tn, m, LANES), F32)]
    if side_cast is not None:
        other3, other_layer = side_cast
        steps, ni = (n // tn) * (m // tm), m // tm
        _, rows_o, cols_o = other3.shape
        assert nk == 1 and rows_o % steps == 0 and (rows_o // steps) % 16 == 0
        slab = rows_o // steps
        operands.append(other3)
        in_specs.append(pl.BlockSpec((None, slab, cols_o), lambda j, i, k: (other_layer, j * ni + i, 0)))
        out_specs.append(pl.BlockSpec((slab, cols_o), lambda j, i, k: (j * ni + i, 0)))
        out_shape.append(jax.ShapeDtypeStruct((rows_o, cols_o), BF16))
    scratch = []
    if cache_w:
        scratch.append(pltpu.VMEM((nk, tk, tn), BF16))
    if nk > 1:
        scratch.append(pltpu.VMEM((tm, tn), F32))
    res = pl.pallas_call(
        functools.partial(_mm_kernel, nk=nk, n_extra=len(extras), epilogue=epilogue, cache_w=cache_w,
                          w_is_nk=w_is_nk, row_scaled=row_ss is not None, norm_out=norm_gain is not None,
                          side_cast=side_cast is not None, d_norm=kdim),
        grid=(n // tn, m // tm, nk),
        in_specs=in_specs,
        out_specs=out_specs,
        out_shape=out_shape,
        scratch_shapes=scratch,
        compiler_params=_params(3),
        name=name,
    )(*operands)
    return res if len(res) > 1 else res[0]


def _epi_plain(acc):
    return acc


def _epi_headnorm(acc, g_ref):
    g = g_ref[...]
    outs = []
    for h in range(acc.shape[1] // HEAD_DIM):
        a = acc[:, h * HEAD_DIM:(h + 1) * HEAD_DIM]
        r = lax.rsqrt(jnp.mean(a * a, axis=-1, keepdims=True) + EPS)
        outs.append((a * r) * g)
    return jnp.concatenate(outs, axis=1)


def _epi_kv(acc, g_ref):
    return jnp.concatenate([_epi_headnorm(acc[:, :KV_W], g_ref), acc[:, KV_W:]], axis=1)


def _epi_rotary(acc, cos_ref, sin_ref):
    c = cos_ref[...]
    s = sin_ref[...]
    scale = jnp.where(pl.program_id(0) == 0, 1.0, RET_QK_DIM ** -0.5).astype(F32)
    outs = []
    for h in range(acc.shape[1] // RET_QK_DIM):
        a = acc[:, h * RET_QK_DIM:(h + 1) * RET_QK_DIM]
        swapped = pltpu.roll(a, RET_QK_DIM // 2, axis=1)
        outs.append((a * c + swapped * s) * scale)
    return jnp.concatenate(outs, axis=1)


def _epi_residual(acc, x_ref):
    return x_ref[...] + acc


def _epi_relu2(acc):
    r = jnp.maximum(acc, 0.0)
    return r * r


def _topk_tile(iq_ref, qpack_ref, kpack_ref, bias_ref, score_ref, sel_ref, *, qbase, keys, tq, topk):
    rows = COUNT_ROWS
    neg_inf = float("-inf")

    ik = kpack_ref[:keys, :IDX_DIM].astype(BF16)
    w_t = qpack_ref[...].T * (IDX_HEADS ** -0.5)
    score = None
    for h in range(IDX_HEADS):
        iqh = iq_ref[:, h * IDX_DIM:(h + 1) * IDX_DIM].astype(BF16)
        logits = lax.dot_general(ik, iqh, (((1,), (1,)), ((), ())), preferred_element_type=F32)
        term = jnp.maximum(logits, 0.0) * w_t[IDX_DIM + h:IDX_DIM + h + 1, :]
        score = term if score is None else score + term

    def causal_mask():
        kpos = lax.broadcasted_iota(jnp.int32, (keys, tq), 0)
        qcol = lax.broadcasted_iota(jnp.int32, (keys, tq), 1) + qbase
        return kpos <= qcol

    causal = causal_mask()
    score = jnp.where(score == 0.0, 0.0, score)
    lo_fill = jnp.where(causal, score, neg_inf)
    score_ref[:keys, :] = lo_fill
    smax = jnp.max(jnp.max(lo_fill.reshape(keys // rows, rows, tq), axis=0), axis=0, keepdims=True)
    hi_fill = jnp.where(causal, score, float("inf"))
    smin = jnp.min(jnp.min(hi_fill.reshape(keys // rows, rows, tq), axis=0), axis=0, keepdims=True)
    n_causal = jnp.minimum(lax.broadcasted_iota(jnp.int32, (1, tq), 1) + (qbase + 1), keys)

    def count_part(thr, c0, width):
        thr_b = jnp.broadcast_to(thr, (rows, width))
        part = jnp.zeros((rows, width), F32)
        for i in range(keys // rows):
            part = part + jnp.where(score_ref[i * rows:(i + 1) * rows, c0:c0 + width] >= thr_b, 1.0, 0.0)
        return part

    kf = float(topk)
    hw = tq // 2

    def step(part, state, first):
        lo, hi, tau, act, mid, odd = state
        cnt = jnp.sum(part, axis=0, keepdims=True)
        ge = cnt >= kf
        hit = (cnt == kf) | (first & ge)
        lo = jnp.where(ge, mid, lo)
        hi = jnp.where(ge, hi, mid)
        nxt = 0.5 * lo + 0.5 * hi
        stuck = (nxt <= lo) | (nxt >= hi)
        live = act > 0
        tau = jnp.where(live & hit, mid, jnp.where(live & stuck, lo, tau))
        inexact = (hit & (cnt != kf)) | (jnp.logical_not(hit) & stuck)
        odd = jnp.where(live & inexact, 1, odd)
        act = jnp.where(hit | stuck, 0, act)
        return lo, hi, tau, act, nxt, odd

    def cond(c):
        return (c[1] > 0) & (c[0] < BISECT_MAX_STEPS)

    def body(c):
        it, _, part_a, st_a, st_b = c
        for u in range(BISECT_UNROLL):
            first = ((jnp.zeros((1, hw), jnp.int32) + it) == 0) if u == 0 else jnp.zeros((1, hw), jnp.bool_)
            st_a = step(part_a, st_a, first)
            part_b = count_part(st_b[4], hw, hw)
            st_b = step(part_b, st_b, first)
            part_a = count_part(st_a[4], 0, hw)
        return it + BISECT_UNROLL, jnp.sum(st_a[3]) + jnp.sum(st_b[3]), part_a, st_a, st_b

    lo0 = jnp.where(n_causal >= topk, smin, neg_inf)

    def init_state(c0):
        return (lo0[:, c0:c0 + hw], smax[:, c0:c0 + hw], jnp.full((1, hw), neg_inf, F32),
                jnp.ones((1, hw), jnp.int32), smax[:, c0:c0 + hw], jnp.zeros((1, hw), jnp.int32))

    st_a0, st_b0 = init_state(0), init_state(hw)
    out = lax.while_loop(cond, body, (jnp.int32(0), jnp.int32(tq), count_part(st_a0[4], 0, hw), st_a0, st_b0))
    tau = jnp.concatenate([out[3][2], out[4][2]], axis=1)

    odd = jnp.concatenate([out[3][5], out[4][5]], axis=1)

    score = score_ref[:keys, :]
    picked = jnp.where(score >= tau, 0.0, NEG_BIG)
    bias_ref[:keys, :] = jnp.where(score > neg_inf, picked, NEG_BIG).astype(bias_ref.dtype)

    @pl.when(jnp.max(jnp.where(tau > neg_inf, odd, 0)) > 0)
    def _():
        tc = 256
        row = lax.broadcasted_iota(jnp.int32, (tc, tc), 0)
        col = lax.broadcasted_iota(jnp.int32, (tc, tc), 1)
        lower = jnp.where(col <= row, 1.0, 0.0).astype(BF16)
        n_gt = jnp.sum(jnp.where(score_ref[:keys, :] > tau, 1.0, 0.0), axis=0, keepdims=True)
        need = kf - n_gt
        carry = jnp.zeros((1, tq), F32)
        for c in range(keys // tc):
            sc = score_ref[c * tc:(c + 1) * tc, :]
            eq = jnp.where(sc == tau, 1.0, 0.0)
            prefix = jnp.dot(lower, eq.astype(BF16), preferred_element_type=F32) + carry
            carry = carry + jnp.sum(eq, axis=0, keepdims=True)
            keep = (sc > tau) | ((sc == tau) & (prefix <= need))
            sel_ref[c * tc:(c + 1) * tc, :] = jnp.where(keep, 0.0, NEG_BIG)
        bias_ref[:keys, :] = jnp.where(causal_mask(), sel_ref[:keys, :], NEG_BIG).astype(bias_ref.dtype)


def _topk_mask_kernel(iq_ref, qpack_ref, kpack_ref, bias_ref, score_ref, sel_ref, *, q0, kend, tq, topk):
    half = pl.program_id(1)
    for which, keys in ((0, kend - tq), (1, kend)):
        @pl.when(half == which)
        def _():
            _topk_tile(iq_ref, qpack_ref, kpack_ref, bias_ref, score_ref, sel_ref,
                       qbase=q0 + which * tq, keys=keys, tq=tq, topk=topk)
            if keys < kend:
                bias_ref[keys:, :] = jnp.full((kend - keys, tq), NEG_BIG, bias_ref.dtype)


def _topk_mask(idx3, *, j, topk):
    b = idx3.shape[0]
    kend = Q_SUPER * (j + 1)
    q0 = Q_SUPER * j
    tq = HALF_ROWS
    nq = Q_SUPER // tq
    pack_blk = IDX_Q_W // LANES
    return pl.pallas_call(
        functools.partial(_topk_mask_kernel, q0=q0, kend=kend, tq=tq, topk=topk),
        grid=(b, nq),
        in_specs=[pl.BlockSpec((None, tq, IDX_Q_W), lambda bi, qi: (bi, q0 // tq + qi, 0)),
                  pl.BlockSpec((None, tq, LANES), lambda bi, qi: (bi, q0 // tq + qi, pack_blk)),
                  pl.BlockSpec((None, kend, LANES), lambda bi, qi: (bi, 0, pack_blk))],
        out_specs=pl.BlockSpec((None, kend, tq), lambda bi, qi: (bi, 0, qi)),
        out_shape=jax.ShapeDtypeStruct((b, kend, Q_SUPER), BF16),
        scratch_shapes=[pltpu.VMEM((kend, tq), F32), pltpu.VMEM((kend, tq), F32)],
        compiler_params=_params(2),
        name=f"topk_mask_{j}",
    )(idx3, idx3, idx3)


def _attn_tile(q_ref, k_ref, v_ref, bias_t_ref, o_ref, *, row0, keys):
    tq = HEAD_DIM
    rows = slice(row0, row0 + tq)
    exp2_scale = (HEAD_DIM ** -0.5) * math.log2(math.e)
    bias_t = bias_t_ref[:keys, rows]
    row = lax.broadcasted_iota(jnp.int32, (tq, tq), 0)
    col = lax.broadcasted_iota(jnp.int32, (tq, tq), 1)
    eye = jnp.where(row == col, 1.0, 0.0).astype(BF16)
    eye_rows = jnp.concatenate([eye] * ATTN_GROUP, axis=0)
    ones = jnp.ones((keys, HEAD_DIM), BF16)
    for g in range(ATTN_KV_HEADS):
        kg = k_ref[:keys, g * HEAD_DIM:(g + 1) * HEAD_DIM]
        vg = v_ref[:keys, g * HEAD_DIM:(g + 1) * HEAD_DIM]
        h0 = g * ATTN_GROUP
        qg = jnp.concatenate([q_ref[rows, (h0 + r) * HEAD_DIM:(h0 + r + 1) * HEAD_DIM]
                              for r in range(ATTN_GROUP)], axis=0)
        q_ext = jnp.concatenate([qg, eye_rows], axis=1)
        k_ext = jnp.concatenate([kg, bias_t], axis=1)
        t = lax.dot_general(q_ext, k_ext, (((1,), (1,)), ((), ())), preferred_element_type=F32)
        m = jnp.max(t, axis=-1, keepdims=True)
        p = jnp.exp2((t - m) * exp2_scale).astype(BF16)
        v_ext = jnp.concatenate([vg, ones], axis=1)
        o_ext = jnp.dot(p, v_ext, preferred_element_type=F32)
        o = o_ext[:, :HEAD_DIM] / o_ext[:, HEAD_DIM:]
        for r in range(ATTN_GROUP):
            o_ref[rows, (h0 + r) * HEAD_DIM:(h0 + r + 1) * HEAD_DIM] = o[r * tq:(r + 1) * tq].astype(o_ref.dtype)


def _attn_kernel(q_ref, k_ref, v_ref, bias_t_ref, o_ref, *, kend):
    for row0 in range(0, Q_SUPER, HEAD_DIM):
        keys = kend - HALF_ROWS if row0 < HALF_ROWS else kend
        _attn_tile(q_ref, k_ref, v_ref, bias_t_ref, o_ref, row0=row0, keys=keys)


def _attention(qo3, kv3, bias_t, *, j):
    b, s, _ = qo3.shape
    kend = Q_SUPER * (j + 1)
    return pl.pallas_call(
        functools.partial(_attn_kernel, kend=kend),
        grid=(b,),
        in_specs=[pl.BlockSpec((None, Q_SUPER, ATTN_W), lambda bi: (bi, j, 0)),
                  pl.BlockSpec((None, kend, KV_W), lambda bi: (bi, 0, 0)),
                  pl.BlockSpec((None, kend, KV_W), lambda bi: (bi, 0, 1)),
                  pl.BlockSpec((None, kend, Q_SUPER), lambda bi: (bi, 0, 0))],
        out_specs=pl.BlockSpec((None, Q_SUPER, ATTN_W), lambda bi: (bi, j, 0)),
        out_shape=jax.ShapeDtypeStruct((b, s, ATTN_W), BF16),
        input_output_aliases={0: 0},
        compiler_params=_params(1),
        name=f"attention_{j}",
    )(qo3, kv3, kv3, bias_t)


def _retention_kernel(q_ref, k_ref, v_ref, gate_ref, decay_ref, xi_ref, zeta_ref, cd_ref, g_ref, b_ref,
                      o_ref, *, n_chunks):
    c = RET_CHUNK
    decay = decay_ref[...]
    xi = xi_ref[...]
    zeta = zeta_ref[...]
    cd = cd_ref[...]
    gn_g = g_ref[...]
    gn_b = b_ref[...]
    state = jnp.zeros((RET_QK_DIM, RET_V_DIM), F32)
    for n in range(n_chunks):
        sl = slice(n * c, (n + 1) * c)
        q = q_ref[sl, :]
        k = k_ref[sl, :]
        v = v_ref[sl, :]
        qk = lax.dot_general(q, k, (((1,), (1,)), ((), ())), preferred_element_type=F32) * decay
        inner = jnp.dot(qk.astype(BF16), v, preferred_element_type=F32)
        cross = jnp.dot(q, state.astype(BF16), preferred_element_type=F32) * xi
        kz_t = (k.astype(F32) * zeta).T.astype(BF16)
        upd = jnp.dot(kz_t, v, preferred_element_type=F32)
        state = upd + cd * state
        y = inner + cross
        mu = jnp.mean(y, axis=-1, keepdims=True)
        d = y - mu
        var = jnp.mean(d * d, axis=-1, keepdims=True)
        yn = d * lax.rsqrt(var + EPS)
        z = yn * gn_g + gn_b
        gate = gate_ref[sl, :].astype(F32)
        silu = gate * (1.0 / (1.0 + jnp.exp(-gate)))
        o_ref[sl, :] = (silu * z).astype(o_ref.dtype)


def _retention_tables():
    c = RET_CHUNK
    lg = np.log(1.0 - np.exp2(-5.0 - np.arange(RET_HEADS, dtype=np.float32))).astype(np.float32)
    pos = np.arange(c, dtype=np.float32)
    diff = pos[:, None] - pos[None, :]
    decay = np.where(diff[None] >= 0, np.exp(lg[:, None, None] * np.maximum(diff, 0.0)[None]), 0.0)
    xi = np.exp(lg[:, None] * (pos[None, :] + 1.0))
    zeta = np.exp(lg[:, None] * (c - 1.0 - pos[None, :]))
    cd = np.exp(lg * c)
    f = lambda a: jnp.asarray(a.astype(np.float32))
    return (f(decay),
            f(np.broadcast_to(xi[:, :, None], (RET_HEADS, c, RET_V_DIM))),
            f(np.broadcast_to(zeta[:, :, None], (RET_HEADS, c, RET_QK_DIM))),
            f(np.broadcast_to(cd[:, None, None], (RET_HEADS, 1, RET_V_DIM))))


def _retention(rot3, plain3, gn_g, gn_b):
    b, s, _ = rot3.shape
    decay, xi, zeta, cd = _retention_tables()
    v_blk0 = 0
    gate_blk0 = RET_V_W // RET_V_DIM
    return pl.pallas_call(
        functools.partial(_retention_kernel, n_chunks=s // RET_CHUNK),
        grid=(b, RET_HEADS),
        in_specs=[pl.BlockSpec((None, s, RET_QK_DIM), lambda bi, h: (bi, 0, h)),
                  pl.BlockSpec((None, s, RET_QK_DIM), lambda bi, h: (bi, 0, RET_HEADS + h)),
                  pl.BlockSpec((None, s, RET_V_DIM), lambda bi, h: (bi, 0, v_blk0 + h)),
                  pl.BlockSpec((None, s, RET_V_DIM), lambda bi, h: (bi, 0, gate_blk0 + h)),
                  pl.BlockSpec((None, RET_CHUNK, RET_CHUNK), lambda bi, h: (h, 0, 0)),
                  pl.BlockSpec((None, RET_CHUNK, RET_V_DIM), lambda bi, h: (h, 0, 0)),
                  pl.BlockSpec((None, RET_CHUNK, RET_QK_DIM), lambda bi, h: (h, 0, 0)),
                  pl.BlockSpec((None, 1, RET_V_DIM), lambda bi, h: (h, 0, 0)),
                  pl.BlockSpec((1, RET_V_DIM), lambda bi, h: (0, h)),
                  pl.BlockSpec((1, RET_V_DIM), lambda bi, h: (0, h))],
        out_specs=pl.BlockSpec((None, s, RET_V_DIM), lambda bi, h: (bi, 0, h)),
        out_shape=jax.ShapeDtypeStruct((b, s, RET_V_W), BF16),
        compiler_params=_params(2),
        name="retention",
    )(rot3, rot3, plain3, plain3, decay, xi, zeta, cd, gn_g.reshape(1, -1), gn_b.reshape(1, -1))


def _merge_kernel(oa_ref, or_ref, wa_ref, wr_ref, ga_ref, gb_ref, o_ref, wa_bf, wr_bf):
    @pl.when(pl.program_id(1) == 0)
    def _():
        wa_bf[...] = wa_ref[...].astype(BF16)
        wr_bf[...] = wr_ref[...].astype(BF16)

    ya = jnp.dot(oa_ref[...], wa_bf[...], preferred_element_type=F32)
    yr = jnp.dot(or_ref[...], wr_bf[...], preferred_element_type=F32)
    ga = ga_ref[...].astype(F32)
    gb = gb_ref[...].astype(F32)
    sa = 1.0 / (1.0 + jnp.exp(-ga))
    sb = 1.0 / (1.0 + jnp.exp(-gb))
    o_ref[...] = (sa * ya + sb * yr).astype(o_ref.dtype)


def _merge(o_attn, o_ret, w_ua3, w_ur3, layer, plain, tm=1024, tn=512):
    m = o_attn.shape[0]
    ga_blk0 = (2 * RET_V_W) // tn
    gb_blk0 = (2 * RET_V_W + D_MODEL) // tn
    return pl.pallas_call(
        _merge_kernel,
        grid=(D_MODEL // tn, m // tm),
        in_specs=[pl.BlockSpec((tm, ATTN_W), lambda j, i: (i, 0)),
                  pl.BlockSpec((tm, RET_V_W), lambda j, i: (i, 0)),
                  pl.BlockSpec((None, ATTN_W, tn), lambda j, i: (layer, 0, j)),
                  pl.BlockSpec((None, RET_V_W, tn), lambda j, i: (layer, 0, j)),
                  pl.BlockSpec((tm, tn), lambda j, i: (i, ga_blk0 + j)),
                  pl.BlockSpec((tm, tn), lambda j, i: (i, gb_blk0 + j))],
        out_specs=pl.BlockSpec((tm, tn), lambda j, i: (i, j)),
        out_shape=jax.ShapeDtypeStruct((m, D_MODEL), BF16),
        scratch_shapes=[pltpu.VMEM((ATTN_W, tn), BF16), pltpu.VMEM((RET_V_W, tn), BF16)],
        compiler_params=_params(2),
        name="gated_merge",
    )(o_attn, o_ret, w_ua3, w_ur3, plain, plain)


def _rotary_tables(s):
    pos = jnp.arange(s, dtype=F32)
    inv_freq = ROPE_BASE ** (-jnp.arange(0, RET_QK_DIM, 2, dtype=F32) / RET_QK_DIM)
    ang = pos[:, None] * inv_freq[None, :]
    cos, sin = jnp.cos(ang), jnp.sin(ang)
    return jnp.concatenate([cos, cos], axis=1), jnp.concatenate([-sin, sin], axis=1)


def kernel(x, ln1_g, w_in, q_norm_g, k_norm_g, ret_gn_g, ret_gn_b, w_up_attn, w_up_ret, w_out, ln2_g,
           w_ff1, w_ff2):
    b, s, d = x.shape
    m = b * s
    depth = w_in.shape[0]
    topk = min(TOPK_MAX, s // 4)
    assert s % Q_SUPER == 0 and topk <= Q_SUPER
    cos2, sin2 = _rotary_tables(s)
    tm = 1024
    s_tiles = s // tm

    w_in_t = jnp.swapaxes(w_in, 1, 2)
    c_kv = ATTN_W
    c_iq = c_kv + 2 * KV_W
    c_rot = c_iq + IDX_Q_W + IDX_DIM + IDX_HEADS
    c_plain = c_rot + 2 * RET_QK_W
    in_proj = functools.partial(_matmul, w3=w_in_t, w_is_nk=True, tm=tm)

    xf = x.reshape(m, d)
    xg, ss = _norm_prep(xf, ln1_g[0])
    for l in range(depth):
        aq =in_proj(xg, layer=l, row_ss=ss, out_dtype=BF16, epilogue=_epi_headnorm, tn=1024, col0=0, n=ATTN_W,
                     name="proj_q",
                     extras=[(q_norm_g[l].reshape(1, HEAD_DIM), (1, HEAD_DIM), lambda i, j: (0, 0))])
        kv = in_proj(xg, layer=l, row_ss=ss, out_dtype=BF16, epilogue=_epi_kv, tn=2 * KV_W, col0=c_kv, n=2 * KV_W,
                     name="proj_kv",
                     extras=[(k_norm_g[l].reshape(1, HEAD_DIM), (1, HEAD_DIM), lambda i, j: (0, 0))])
        idx = in_proj(xg, layer=l, row_ss=ss, out_dtype=F32, epilogue=_epi_plain, tn=IDX_PACK_W, col0=c_iq,
                      n=IDX_PACK_W, name="proj_idx")
        rot = in_proj(xg, layer=l, row_ss=ss, out_dtype=BF16, epilogue=_epi_rotary, tn=RET_QK_W, col0=c_rot,
                      n=2 * RET_QK_W, name="proj_rot",
                      extras=[(cos2, (tm, RET_QK_DIM), lambda i, j: (i % s_tiles, 0)),
                              (sin2, (tm, RET_QK_DIM), lambda i, j: (i % s_tiles, 0))])
        plain = in_proj(xg, layer=l, row_ss=ss, out_dtype=BF16, epilogue=_epi_plain, tn=1024, col0=c_plain,
                        n=PLAIN_W, name="proj_plain")

        aq3 = aq.reshape(b, s, ATTN_W)
        kv3 = kv.reshape(b, s, 2 * KV_W)
        idx3 = idx.reshape(b, s, IDX_PACK_W)
        plain3 = plain.reshape(b, s, PLAIN_W)
        rot3 = rot.reshape(b, s, 2 * RET_QK_W)

        qo3 = aq3
        for j in range(s // Q_SUPER):
            bias = _topk_mask(idx3, j=j, topk=topk)
            qo3 = _attention(qo3, kv3, bias, j=j)
        o_attn = qo3.reshape(m, ATTN_W)

        o_ret = _retention(rot3, plain3, ret_gn_g[l], ret_gn_b[l]).reshape(m, RET_V_W)

        merged = _merge(o_attn, o_ret, w_up_attn, w_up_ret, l, plain)
        xf, xg2, ss2 = _matmul(merged, w_out, l, out_dtype=F32, epilogue=_epi_residual, tm=tm, tn=1024,
                               name="proj_out", extras=[(xf, (tm, 1024), lambda i, j: (i, j))],
                               norm_gain=ln2_g[l].reshape(1, d), w_buffers=1)

        f, w_ff2_bf = _matmul(xg2, w_ff1, l, out_dtype=BF16, epilogue=_epi_relu2, tm=tm, tn=1024, row_ss=ss2,
                              side_cast=(w_ff2, l), name="ff1")
        next_gain = ln1_g[l + 1].reshape(1, d) if l + 1 < depth else None
        res = _matmul(f, w_ff2_bf[None], 0, out_dtype=F32, epilogue=_epi_residual, tm=tm, tn=1024, tk=2048,
                      name="ff2", extras=[(xf, (tm, 1024), lambda i, j: (i, j))], norm_gain=next_gain)
        xf, xg, ss = res if next_gain is not None else (res, None, None)
    return xf.reshape(b, s, d)
```

```python
import functools
import math

import jax
import jax.numpy as jnp
import numpy as np
from jax import lax
from jax.experimental import pallas as pl
from jax.experimental.pallas import tpu as pltpu

D_MODEL = 2048
HEAD_DIM = 128
ATTN_Q_HEADS = 16
ATTN_KV_HEADS = 4
ATTN_GROUP = ATTN_Q_HEADS // ATTN_KV_HEADS
IDX_HEADS = 16
IDX_DIM = 64
TOPK_MAX = 256
RET_HEADS = 8
RET_QK_DIM = 128
RET_V_DIM = 256
RET_CHUNK = 128
ROPE_BASE = 10000.0
D_FF = 4 * D_MODEL
EPS = 1e-6

ATTN_W = ATTN_Q_HEADS * HEAD_DIM
KV_W = ATTN_KV_HEADS * HEAD_DIM
IDX_Q_W = IDX_HEADS * IDX_DIM
RET_QK_W = RET_HEADS * RET_QK_DIM
RET_V_W = RET_HEADS * RET_V_DIM

LANES = 128
PLAIN_W = 2 * RET_V_W + 2 * D_MODEL
IDX_PACK_W = IDX_Q_W + LANES
COUNT_ROWS = 64
BISECT_MAX_STEPS = 2200
BISECT_UNROLL = 4
Q_SUPER = 512
HALF_ROWS = Q_SUPER // 2
NEG_BIG = -1e30
VMEM_LIMIT = 56 * 1024 * 1024

F32 = jnp.float32
BF16 = jnp.bfloat16


def _params(n_axes):
    return pltpu.CompilerParams(dimension_semantics=("arbitrary",) * n_axes,
                                vmem_limit_bytes=VMEM_LIMIT)


def _row_sumsq(x):
    return jnp.broadcast_to(jnp.sum(x * x, axis=-1, keepdims=True), (x.shape[0], LANES))


def _norm_prep_kernel(x_ref, g_ref, xg_ref, ss_ref):
    x = x_ref[...]
    xg_ref[...] = (x * g_ref[...]).astype(xg_ref.dtype)
    ss_ref[...] = _row_sumsq(x)


def _norm_prep(x, g, tm=512):
    m, d = x.shape
    return pl.pallas_call(
        _norm_prep_kernel,
        grid=(m // tm,),
        in_specs=[pl.BlockSpec((tm, d), lambda i: (i, 0)),
                  pl.BlockSpec((1, d), lambda i: (0, 0))],
        out_specs=[pl.BlockSpec((tm, d), lambda i: (i, 0)),
                   pl.BlockSpec((None, tm, LANES), lambda i: (0, i, 0))],
        out_shape=[jax.ShapeDtypeStruct((m, d), BF16), jax.ShapeDtypeStruct((1, m, LANES), F32)],
        compiler_params=_params(1),
        name="norm_prep",
    )(x, g.reshape(1, d))


def _mm_kernel(*refs, nk, n_extra, epilogue, cache_w, w_is_nk, row_scaled, norm_out, side_cast, d_norm):
    a_ref, w_ref = refs[0], refs[1]
    n_fixed = 2 + int(row_scaled) + n_extra
    n_in = n_fixed + int(norm_out) + int(side_cast)
    ss_in_ref = refs[2] if row_scaled else None
    extra = refs[2 + int(row_scaled):n_fixed]
    gain_ref = refs[n_fixed] if norm_out else None
    o_ref = refs[n_in]
    n_out = 1 + 2 * int(norm_out) + int(side_cast)
    scratch = refs[n_in + n_out:]
    i = pl.program_id(1)
    k = pl.program_id(2)
    if side_cast:
        refs[n_in + n_out - 1][...] = refs[n_in - 1][...].astype(BF16)
    if cache_w:
        wbf_ref = scratch[0]

        @pl.when(i == 0)
        def _():
            wbf_ref[k] = (w_ref[0].T if w_is_nk else w_ref[...]).astype(BF16)

        w = wbf_ref[k]
    else:
        w = w_ref[...]
    prod = jnp.dot(a_ref[...], w, preferred_element_type=F32)

    def finish(acc):
        if row_scaled:
            r = lax.rsqrt(jnp.sum(ss_in_ref[...], axis=0) * (1.0 / d_norm) + EPS)
            acc = acc * jnp.concatenate([r] * (acc.shape[1] // LANES), axis=1)
        out = epilogue(acc, *extra)
        o_ref[...] = out.astype(o_ref.dtype)
        if norm_out:
            xg_ref, ss_out_ref = refs[n_in + 1], refs[n_in + 2]
            xg_ref[...] = (out * gain_ref[...]).astype(xg_ref.dtype)
            ss_out_ref[...] = _row_sumsq(out)

    if nk == 1:
        finish(prod)
        return
    acc_ref = scratch[-1]

    @pl.when(k == 0)
    def _():
        acc_ref[...] = prod

    @pl.when((k > 0) & (k < nk - 1))
    def _():
        acc_ref[...] += prod

    @pl.when(k == nk - 1)
    def _():
        finish(acc_ref[...] + prod)


def _matmul(a, w3, layer, *, out_dtype, epilogue, extras=(), tm=1024, tn=1024, tk=None, col0=0, n=None,
            w_is_nk=False, row_ss=None, norm_gain=None, w_buffers=2, side_cast=None, name="matmul"):
    m, kdim = a.shape
    n_total = w3.shape[1] if w_is_nk else w3.shape[2]
    n = n_total - col0 if n is None else n
    tk = kdim if tk is None else tk
    nk = kdim // tk
    tn = min(tn, n)
    assert m % tm == 0 and n % tn == 0 and kdim % tk == 0
    cache_w = w3.dtype != BF16
    if w_is_nk:
        assert cache_w and nk == 1 and col0 % 8 == 0
        w_spec = pl.BlockSpec((pl.Element(1), pl.Element(tn), pl.Element(tk)),
                              lambda j, i, k: (layer, pl.multiple_of(col0 + j * tn, 8), 0))
    else:
        assert col0 % tn == 0
        jb0 = col0 // tn
        if cache_w:
            w_map = lambda j, i, k: (layer, jnp.where(i == 0, k, nk - 1), jb0 + j)
        else:
            w_map = lambda j, i, k: (layer, k, jb0 + j)
        mode = {} if w_buffers == 2 else {"pipeline_mode": pl.Buffered(w_buffers)}
        w_spec = pl.BlockSpec((None, tk, tn), w_map, **mode)
    operands = [a, w3]
    in_specs = [pl.BlockSpec((tm, tk), lambda j, i, k: (i, k)), w_spec]
    if row_ss is not None:
        operands.append(row_ss)
        in_specs.append(pl.BlockSpec((row_ss.shape[0], tm, LANES), lambda j, i, k: (0, i, 0)))
    for arr, bshape, imap in extras:
        operands.append(arr)
        in_specs.append(pl.BlockSpec(bshape, functools.partial(lambda j, i, k, f: f(i, j), f=imap)))
    out_specs = [pl.BlockSpec((tm, tn), lambda j, i, k: (i, j))]
    out_shape = [jax.ShapeDtypeStruct((m, n), out_dtype)]
    if norm_gain is not None:
        operands.append(norm_gain)
        in_specs.append(pl.BlockSpec((1, tn), lambda j, i, k: (0, j)))
        out_specs += [pl.BlockSpec((tm, tn), lambda j, i, k: (i, j)),
                      pl.BlockSpec((None, tm, LANES), lambda j, i, k: (j, i, 0))]
        out_shape += [jax.ShapeDtypeStruct((m, n), BF16), jax.ShapeDtypeStruct((n // tn, m, LANES), F32)]
    if side_cast is not None:
        other3, other_layer = side_cast
        steps, ni = (n // tn) * (m // tm), m // tm
        _, rows_o, cols_o = other3.shape
        assert nk == 1 and rows_o % steps == 0 and (rows_o // steps) % 16 == 0
        slab = rows_o // steps
        operands.append(other3)
        in_specs.append(pl.BlockSpec((None, slab, cols_o), lambda j, i, k: (other_layer, j * ni + i, 0)))
        out_specs.append(pl.BlockSpec((slab, cols_o), lambda j, i, k: (j * ni + i, 0)))
        out_shape.append(jax.ShapeDtypeStruct((rows_o, cols_o), BF16))
    scratch = []
    if cache_w:
        scratch.append(pltpu.VMEM((nk, tk, tn), BF16))
    if nk > 1:
        scratch.append(pltpu.VMEM((tm, tn), F32))
    res = pl.pallas_call(
        functools.partial(_mm_kernel, nk=nk, n_extra=len(extras), epilogue=epilogue, cache_w=cache_w,
                          w_is_nk=w_is_nk, row_scaled=row_ss is not None, norm_out=norm_gain is not None,
                          side_cast=side_cast is not None, d_norm=kdim),
        grid=(n // tn, m // tm, nk),
        in_specs=in_specs,
        out_specs=out_specs,
        out_shape=out_shape,
        scratch_shapes=scratch,
        compiler_params=_params(3),
        name=name,
    )(*operands)
    return res if len(res) > 1 else res[0]


def _epi_plain(acc):
    return acc


def _epi_headnorm(acc, g_ref):
    g = g_ref[...]
    outs = []
    for h in range(acc.shape[1] // HEAD_DIM):
        a = acc[:, h * HEAD_DIM:(h + 1) * HEAD_DIM]
        r = lax.rsqrt(jnp.mean(a * a, axis=-1, keepdims=True) + EPS)
        outs.append((a * r) * g)
    return jnp.concatenate(outs, axis=1)


def _epi_kv(acc, g_ref):
    return jnp.concatenate([_epi_headnorm(acc[:, :KV_W], g_ref), acc[:, KV_W:]], axis=1)


def _epi_rotary(acc, cos_ref, sin_ref):
    c = cos_ref[...]
    s = sin_ref[...]
    scale = jnp.where(pl.program_id(0) == 0, 1.0, RET_QK_DIM ** -0.5).astype(F32)
    outs = []
    for h in range(acc.shape[1] // RET_QK_DIM):
        a = acc[:, h * RET_QK_DIM:(h + 1) * RET_QK_DIM]
        swapped = pltpu.roll(a, RET_QK_DIM // 2, axis=1)
        outs.append((a * c + swapped * s) * scale)
    return jnp.concatenate(outs, axis=1)


def _epi_residual(acc, x_ref):
    return x_ref[...] + acc


def _epi_relu2(acc):
    r = jnp.maximum(acc, 0.0)
    return r * r


def _topk_tile(iq_ref, qpack_ref, kpack_ref, bias_ref, score_ref, sel_ref, *, qbase, keys, tq, topk):
    rows = COUNT_ROWS
    neg_inf = float("-inf")

    ik = kpack_ref[:keys, :IDX_DIM].astype(BF16)
    w_t = qpack_ref[...].T * (IDX_HEADS ** -0.5)
    score = None
    for h in range(IDX_HEADS):
        iqh = iq_ref[:, h * IDX_DIM:(h + 1) * IDX_DIM].astype(BF16)
        logits = lax.dot_general(ik, iqh, (((1,), (1,)), ((), ())), preferred_element_type=F32)
        term = jnp.maximum(logits, 0.0) * w_t[IDX_DIM + h:IDX_DIM + h + 1, :]
        score = term if score is None else score + term

    def causal_mask():
        kpos = lax.broadcasted_iota(jnp.int32, (keys, tq), 0)
        qcol = lax.broadcasted_iota(jnp.int32, (keys, tq), 1) + qbase
        return kpos <= qcol

    causal = causal_mask()
    score = jnp.where(score == 0.0, 0.0, score)
    lo_fill = jnp.where(causal, score, neg_inf)
    score_ref[:keys, :] = lo_fill
    smax = jnp.max(jnp.max(lo_fill.reshape(keys // rows, rows, tq), axis=0), axis=0, keepdims=True)
    hi_fill = jnp.where(causal, score, float("inf"))
    smin = jnp.min(jnp.min(hi_fill.reshape(keys // rows, rows, tq), axis=0), axis=0, keepdims=True)
    n_causal = jnp.minimum(lax.broadcasted_iota(jnp.int32, (1, tq), 1) + (qbase + 1), keys)

    def count_part(thr, c0, width):
        thr_b = jnp.broadcast_to(thr, (rows, width))
        part = jnp.zeros((rows, width), F32)
        for i in range(keys // rows):
            part = part + jnp.where(score_ref[i * rows:(i + 1) * rows, c0:c0 + width] >= thr_b, 1.0, 0.0)
        return part

    kf = float(topk)
    hw = tq // 2

    def step(part, state, first):
        lo, hi, tau, act, mid, odd = state
        cnt = jnp.sum(part, axis=0, keepdims=True)
        ge = cnt >= kf
        hit = (cnt == kf) | (first & ge)
        lo = jnp.where(ge, mid, lo)
        hi = jnp.where(ge, hi, mid)
        nxt = 0.5 * lo + 0.5 * hi
        stuck = (nxt <= lo) | (nxt >= hi)
        live = act > 0
        tau = jnp.where(live & hit, mid, jnp.where(live & stuck, lo, tau))
        inexact = (hit & (cnt != kf)) | (jnp.logical_not(hit) & stuck)
        odd = jnp.where(live & inexact, 1, odd)
        act = jnp.where(hit | stuck, 0, act)
        return lo, hi, tau, act, nxt, odd

    def cond(c):
        return (c[1] > 0) & (c[0] < BISECT_MAX_STEPS)

    def body(c):
        it, _, part_a, st_a, st_b = c
        for u in range(BISECT_UNROLL):
            first = ((jnp.zeros((1, hw), jnp.int32) + it) == 0) if u == 0 else jnp.zeros((1, hw), jnp.bool_)
            st_a = step(part_a, st_a, first)
            part_b = count_part(st_b[4], hw, hw)
            st_b = step(part_b, st_b, first)
            part_a = count_part(st_a[4], 0, hw)
        return it + BISECT_UNROLL, jnp.sum(st_a[3]) + jnp.sum(st_b[3]), part_a, st_a, st_b

    lo0 = jnp.where(n_causal >= topk, smin, neg_inf)

    def init_state(c0):
        return (lo0[:, c0:c0 + hw], smax[:, c0:c0 + hw], jnp.full((1, hw), neg_inf, F32),
                jnp.ones((1, hw), jnp.int32), smax[:, c0:c0 + hw], jnp.zeros((1, hw), jnp.int32))

    st_a0, st_b0 = init_state(0), init_state(hw)
    out = lax.while_loop(cond, body, (jnp.int32(0), jnp.int32(tq), count_part(st_a0[4], 0, hw), st_a0, st_b0))
    tau = jnp.concatenate([out[3][2], out[4][2]], axis=1)

    odd = jnp.concatenate([out[3][5], out[4][5]], axis=1)

    picked = jnp.where(score_ref[:keys, :] >= tau, 0.0, NEG_BIG)
    bias_ref[:keys, :] = jnp.where(causal_mask(), picked, NEG_BIG).astype(bias_ref.dtype)

    @pl.when(jnp.max(jnp.where(tau > neg_inf, odd, 0)) > 0)
    def _():
        tc = 256
        row = lax.broadcasted_iota(jnp.int32, (tc, tc), 0)
        col = lax.broadcasted_iota(jnp.int32, (tc, tc), 1)
        lower = jnp.where(col <= row, 1.0, 0.0).astype(BF16)
        n_gt = jnp.sum(jnp.where(score_ref[:keys, :] > tau, 1.0, 0.0), axis=0, keepdims=True)
        need = kf - n_gt
        carry = jnp.zeros((1, tq), F32)
        for c in range(keys // tc):
            sc = score_ref[c * tc:(c + 1) * tc, :]
            eq = jnp.where(sc == tau, 1.0, 0.0)
            prefix = jnp.dot(lower, eq.astype(BF16), preferred_element_type=F32) + carry
            carry = carry + jnp.sum(eq, axis=0, keepdims=True)
            keep = (sc > tau) | ((sc == tau) & (prefix <= need))
            sel_ref[c * tc:(c + 1) * tc, :] = jnp.where(keep, 0.0, NEG_BIG)
        bias_ref[:keys, :] = jnp.where(causal_mask(), sel_ref[:keys, :], NEG_BIG).astype(bias_ref.dtype)


def _topk_mask_kernel(iq_ref, qpack_ref, kpack_ref, bias_ref, score_ref, sel_ref, *, q0, kend, tq, topk):
    half = pl.program_id(1)
    for which, keys in ((0, kend - tq), (1, kend)):
        @pl.when(half == which)
        def _():
            _topk_tile(iq_ref, qpack_ref, kpack_ref, bias_ref, score_ref, sel_ref,
                       qbase=q0 + which * tq, keys=keys, tq=tq, topk=topk)
            if keys < kend:
                bias_ref[keys:, :] = jnp.full((kend - keys, tq), NEG_BIG, bias_ref.dtype)


def _topk_mask(idx3, *, j, topk):
    b = idx3.shape[0]
    kend = Q_SUPER * (j + 1)
    q0 = Q_SUPER * j
    tq = HALF_ROWS
    nq = Q_SUPER // tq
    pack_blk = IDX_Q_W // LANES
    return pl.pallas_call(
        functools.partial(_topk_mask_kernel, q0=q0, kend=kend, tq=tq, topk=topk),
        grid=(b, nq),
        in_specs=[pl.BlockSpec((None, tq, IDX_Q_W), lambda bi, qi: (bi, q0 // tq + qi, 0)),
                  pl.BlockSpec((None, tq, LANES), lambda bi, qi: (bi, q0 // tq + qi, pack_blk)),
                  pl.BlockSpec((None, kend, LANES), lambda bi, qi: (bi, 0, pack_blk))],
        out_specs=pl.BlockSpec((None, kend, tq), lambda bi, qi: (bi, 0, qi)),
        out_shape=jax.ShapeDtypeStruct((b, kend, Q_SUPER), BF16),
        scratch_shapes=[pltpu.VMEM((kend, tq), F32), pltpu.VMEM((kend, tq), F32)],
        compiler_params=_params(2),
        name=f"topk_mask_{j}",
    )(idx3, idx3, idx3)


def _attn_tile(q_ref, k_ref, v_ref, bias_t_ref, o_ref, *, row0, keys):
    tq = HEAD_DIM
    rows = slice(row0, row0 + tq)
    exp2_scale = (HEAD_DIM ** -0.5) * math.log2(math.e)
    bias_t = bias_t_ref[:keys, rows]
    row = lax.broadcasted_iota(jnp.int32, (tq, tq), 0)
    col = lax.broadcasted_iota(jnp.int32, (tq, tq), 1)
    eye = jnp.where(row == col, 1.0, 0.0).astype(BF16)
    eye_rows = jnp.concatenate([eye] * ATTN_GROUP, axis=0)
    ones = jnp.ones((keys, HEAD_DIM), BF16)
    for g in range(ATTN_KV_HEADS):
        kg = k_ref[:keys, g * HEAD_DIM:(g + 1) * HEAD_DIM]
        vg = v_ref[:keys, g * HEAD_DIM:(g + 1) * HEAD_DIM]
        h0 = g * ATTN_GROUP
        qg = jnp.concatenate([q_ref[rows, (h0 + r) * HEAD_DIM:(h0 + r + 1) * HEAD_DIM]
                              for r in range(ATTN_GROUP)], axis=0)
        q_ext = jnp.concatenate([qg, eye_rows], axis=1)
        k_ext = jnp.concatenate([kg, bias_t], axis=1)
        t = lax.dot_general(q_ext, k_ext, (((1,), (1,)), ((), ())), preferred_element_type=F32)
        m = jnp.max(t, axis=-1, keepdims=True)
        p = jnp.exp2((t - m) * exp2_scale).astype(BF16)
        v_ext = jnp.concatenate([vg, ones], axis=1)
        o_ext = jnp.dot(p, v_ext, preferred_element_type=F32)
        o = o_ext[:, :HEAD_DIM] / o_ext[:, HEAD_DIM:]
        for r in range(ATTN_GROUP):
            o_ref[rows, (h0 + r) * HEAD_DIM:(h0 + r + 1) * HEAD_DIM] = o[r * tq:(r + 1) * tq].astype(o_ref.dtype)


def _attn_kernel(q_ref, k_ref, v_ref, bias_t_ref, o_ref, *, kend):
    for row0 in range(0, Q_SUPER, HEAD_DIM):
        keys = kend - HALF_ROWS if row0 < HALF_ROWS else kend
        _attn_tile(q_ref, k_ref, v_ref, bias_t_ref, o_ref, row0=row0, keys=keys)


def _attention(qo3, kv3, bias_t, *, j):
    b, s, _ = qo3.shape
    kend = Q_SUPER * (j + 1)
    return pl.pallas_call(
        functools.partial(_attn_kernel, kend=kend),
        grid=(b,),
        in_specs=[pl.BlockSpec((None, Q_SUPER, ATTN_W), lambda bi: (bi, j, 0)),
                  pl.BlockSpec((None, kend, KV_W), lambda bi: (bi, 0, 0)),
                  pl.BlockSpec((None, kend, KV_W), lambda bi: (bi, 0, 1)),
                  pl.BlockSpec((None, kend, Q_SUPER), lambda bi: (bi, 0, 0))],
        out_specs=pl.BlockSpec((None, Q_SUPER, ATTN_W), lambda bi: (bi, j, 0)),
        out_shape=jax.ShapeDtypeStruct((b, s, ATTN_W), BF16),
        input_output_aliases={0: 0},
        compiler_params=_params(1),
        name=f"attention_{j}",
    )(qo3, kv3, kv3, bias_t)


def _retention_kernel(q_ref, k_ref, v_ref, gate_ref, decay_ref, xi_ref, zeta_ref, cd_ref, g_ref, b_ref,
                      o_ref, *, n_chunks):
    c = RET_CHUNK
    decay = decay_ref[...]
    xi = xi_ref[...]
    zeta = zeta_ref[...]
    cd = cd_ref[...]
    gn_g = g_ref[...]
    gn_b = b_ref[...]
    state = jnp.zeros((RET_QK_DIM, RET_V_DIM), F32)
    for n in range(n_chunks):
        sl = slice(n * c, (n + 1) * c)
        q = q_ref[sl, :]
        k = k_ref[sl, :]
        v = v_ref[sl, :]
        qk = lax.dot_general(q, k, (((1,), (1,)), ((), ())), preferred_element_type=F32) * decay
        inner = jnp.dot(qk.astype(BF16), v, preferred_element_type=F32)
        cross = jnp.dot(q, state.astype(BF16), preferred_element_type=F32) * xi
        kz_t = (k.astype(F32) * zeta).T.astype(BF16)
        upd = jnp.dot(kz_t, v, preferred_element_type=F32)
        state = upd + cd * state
        y = inner + cross
        mu = jnp.mean(y, axis=-1, keepdims=True)
        d = y - mu
        var = jnp.mean(d * d, axis=-1, keepdims=True)
        yn = d * lax.rsqrt(var + EPS)
        z = yn * gn_g + gn_b
        gate = gate_ref[sl, :].astype(F32)
        silu = gate * (1.0 / (1.0 + jnp.exp(-gate)))
        o_ref[sl, :] = (silu * z).astype(o_ref.dtype)


def _retention_tables():
    c = RET_CHUNK
    lg = np.log(1.0 - np.exp2(-5.0 - np.arange(RET_HEADS, dtype=np.float32))).astype(np.float32)
    pos = np.arange(c, dtype=np.float32)
    diff = pos[:, None] - pos[None, :]
    decay = np.where(diff[None] >= 0, np.exp(lg[:, None, None] * np.maximum(diff, 0.0)[None]), 0.0)
    xi = np.exp(lg[:, None] * (pos[None, :] + 1.0))
    zeta = np.exp(lg[:, None] * (c - 1.0 - pos[None, :]))
    cd = np.exp(lg * c)
    f = lambda a: jnp.asarray(a.astype(np.float32))
    return (f(decay),
            f(np.broadcast_to(xi[:, :, None], (RET_HEADS, c, RET_V_DIM))),
            f(np.broadcast_to(zeta[:, :, None], (RET_HEADS, c, RET_QK_DIM))),
            f(np.broadcast_to(cd[:, None, None], (RET_HEADS, 1, RET_V_DIM))))


def _retention(rot3, plain3, gn_g, gn_b):
    b, s, _ = rot3.shape
    decay, xi, zeta, cd = _retention_tables()
    v_blk0 = 0
    gate_blk0 = RET_V_W // RET_V_DIM
    return pl.pallas_call(
        functools.partial(_retention_kernel, n_chunks=s // RET_CHUNK),
        grid=(b, RET_HEADS),
        in_specs=[pl.BlockSpec((None, s, RET_QK_DIM), lambda bi, h: (bi, 0, h)),
                  pl.BlockSpec((None, s, RET_QK_DIM), lambda bi, h: (bi, 0, RET_HEADS + h)),
                  pl.BlockSpec((None, s, RET_V_DIM), lambda bi, h: (bi, 0, v_blk0 + h)),
                  pl.BlockSpec((None, s, RET_V_DIM), lambda bi, h: (bi, 0, gate_blk0 + h)),
                  pl.BlockSpec((None, RET_CHUNK, RET_CHUNK), lambda bi, h: (h, 0, 0)),
                  pl.BlockSpec((None, RET_CHUNK, RET_V_DIM), lambda bi, h: (h, 0, 0)),
                  pl.BlockSpec((None, RET_CHUNK, RET_QK_DIM), lambda bi, h: (h, 0, 0)),
                  pl.BlockSpec((None, 1, RET_V_DIM), lambda bi, h: (h, 0, 0)),
                  pl.BlockSpec((1, RET_V_DIM), lambda bi, h: (0, h)),
                  pl.BlockSpec((1, RET_V_DIM), lambda bi, h: (0, h))],
        out_specs=pl.BlockSpec((None, s, RET_V_DIM), lambda bi, h: (bi, 0, h)),
        out_shape=jax.ShapeDtypeStruct((b, s, RET_V_W), BF16),
        compiler_params=_params(2),
        name="retention",
    )(rot3, rot3, plain3, plain3, decay, xi, zeta, cd, gn_g.reshape(1, -1), gn_b.reshape(1, -1))


def _merge_kernel(oa_ref, or_ref, wa_ref, wr_ref, ga_ref, gb_ref, o_ref, wa_bf, wr_bf):
    @pl.when(pl.program_id(1) == 0)
    def _():
        wa_bf[...] = wa_ref[...].astype(BF16)
        wr_bf[...] = wr_ref[...].astype(BF16)

    ya = jnp.dot(oa_ref[...], wa_bf[...], preferred_element_type=F32)
    yr = jnp.dot(or_ref[...], wr_bf[...], preferred_element_type=F32)
    ga = ga_ref[...].astype(F32)
    gb = gb_ref[...].astype(F32)
    sa = 1.0 / (1.0 + jnp.exp(-ga))
    sb = 1.0 / (1.0 + jnp.exp(-gb))
    o_ref[...] = (sa * ya + sb * yr).astype(o_ref.dtype)


def _merge(o_attn, o_ret, w_ua3, w_ur3, layer, plain, tm=1024, tn=512):
    m = o_attn.shape[0]
    ga_blk0 = (2 * RET_V_W) // tn
    gb_blk0 = (2 * RET_V_W + D_MODEL) // tn
    return pl.pallas_call(
        _merge_kernel,
        grid=(D_MODEL // tn, m // tm),
        in_specs=[pl.BlockSpec((tm, ATTN_W), lambda j, i: (i, 0)),
                  pl.BlockSpec((tm, RET_V_W), lambda j, i: (i, 0)),
                  pl.BlockSpec((None, ATTN_W, tn), lambda j, i: (layer, 0, j)),
                  pl.BlockSpec((None, RET_V_W, tn), lambda j, i: (layer, 0, j)),
                  pl.BlockSpec((tm, tn), lambda j, i: (i, ga_blk0 + j)),
                  pl.BlockSpec((tm, tn), lambda j, i: (i, gb_blk0 + j))],
        out_specs=pl.BlockSpec((tm, tn), lambda j, i: (i, j)),
        out_shape=jax.ShapeDtypeStruct((m, D_MODEL), BF16),
        scratch_shapes=[pltpu.VMEM((ATTN_W, tn), BF16), pltpu.VMEM((RET_V_W, tn), BF16)],
        compiler_params=_params(2),
        name="gated_merge",
    )(o_attn, o_ret, w_ua3, w_ur3, plain, plain)


def _rotary_tables(s):
    pos = jnp.arange(s, dtype=F32)
    inv_freq = ROPE_BASE ** (-jnp.arange(0, RET_QK_DIM, 2, dtype=F32) / RET_QK_DIM)
    ang = pos[:, None] * inv_freq[None, :]
    cos, sin = jnp.cos(ang), jnp.sin(ang)
    return jnp.concatenate([cos, cos], axis=1), jnp.concatenate([-sin, sin], axis=1)


def kernel(x, ln1_g, w_in, q_norm_g, k_norm_g, ret_gn_g, ret_gn_b, w_up_attn, w_up_ret, w_out, ln2_g,
           w_ff1, w_ff2):
    b, s, d = x.shape
    m = b * s
    depth = w_in.shape[0]
    topk = min(TOPK_MAX, s // 4)
    assert s % Q_SUPER == 0 and topk <= Q_SUPER
    cos2, sin2 = _rotary_tables(s)
    tm = 1024
    s_tiles = s // tm

    w_in_t = jnp.swapaxes(w_in, 1, 2)
    c_kv = ATTN_W
    c_iq = c_kv + 2 * KV_W
    c_rot = c_iq + IDX_Q_W + IDX_DIM + IDX_HEADS
    c_plain = c_rot + 2 * RET_QK_W
    in_proj = functools.partial(_matmul, w3=w_in_t, w_is_nk=True, tm=tm)

    xf = x.reshape(m, d)
    xg, ss = _norm_prep(xf, ln1_g[0])
    for l in range(depth):
        aq =in_proj(xg, layer=l, row_ss=ss, out_dtype=BF16, epilogue=_epi_headnorm, tn=1024, col0=0, n=ATTN_W,
                     name="proj_q",
                     extras=[(q_norm_g[l].reshape(1, HEAD_DIM), (1, HEAD_DIM), lambda i, j: (0, 0))])
        kv = in_proj(xg, layer=l, row_ss=ss, out_dtype=BF16, epilogue=_epi_kv, tn=2 * KV_W, col0=c_kv, n=2 * KV_W,
                     name="proj_kv",
                     extras=[(k_norm_g[l].reshape(1, HEAD_DIM), (1, HEAD_DIM), lambda i, j: (0, 0))])
        idx = in_proj(xg, layer=l, row_ss=ss, out_dtype=F32, epilogue=_epi_plain, tn=IDX_PACK_W, col0=c_iq,
                      n=IDX_PACK_W, name="proj_idx")
        rot = in_proj(xg, layer=l, row_ss=ss, out_dtype=BF16, epilogue=_epi_rotary, tn=RET_QK_W, col0=c_rot,
                      n=2 * RET_QK_W, name="proj_rot",
                      extras=[(cos2, (tm, RET_QK_DIM), lambda i, j: (i % s_tiles, 0)),
                              (sin2, (tm, RET_QK_DIM), lambda i, j: (i % s_tiles, 0))])
        plain = in_proj(xg, layer=l, row_ss=ss, out_dtype=BF16, epilogue=_epi_plain, tn=1024, col0=c_plain,
                        n=PLAIN_W, name="proj_plain")

        aq3 = aq.reshape(b, s, ATTN_W)
        kv3 = kv.reshape(b, s, 2 * KV_W)
        idx3 = idx.reshape(b, s, IDX_PACK_W)
        plain3 = plain.reshape(b, s, PLAIN_W)
        rot3 = rot.reshape(b, s, 2 * RET_QK_W)

        qo3 = aq3
        for j in range(s // Q_SUPER):
            bias = _topk_mask(idx3, j=j, topk=topk)
            qo3 = _attention(qo3, kv3, bias, j=j)
        o_attn = qo3.reshape(m, ATTN_W)

        o_ret = _retention(rot3, plain3, ret_gn_g[l], ret_gn_b[l]).reshape(m, RET_V_W)

        merged = _merge(o_attn, o_ret, w_up_attn, w_up_ret, l, plain)
        xf, xg2, ss2 = _matmul(merged, w_out, l, out_dtype=F32, epilogue=_epi_residual, tm=tm, tn=1024,
                               name="proj_out", extras=[(xf, (tm, 1024), lambda i, j: (i, j))],
                               norm_gain=ln2_g[l].reshape(1, d))

        f, w_ff2_bf = _matmul(xg2, w_ff1, l, out_dtype=BF16, epilogue=_epi_relu2, tm=tm, tn=1024, row_ss=ss2,
                              side_cast=(w_ff2, l), name="ff1")
        next_gain = ln1_g[l + 1].reshape(1, d) if l + 1 < depth else None
        res = _matmul(f, w_ff2_bf[None], 0, out_dtype=F32, epilogue=_epi_residual, tm=tm, tn=1024, tk=2048,
                      name="ff2", extras=[(xf, (tm, 1024), lambda i, j: (i, j))], norm_gain=next_gain)
        xf, xg, ss = res if next_gain is not None else (res, None, None)
    return xf.reshape(b, s, d)
```
